```python
import jax, jax.numpy as jnp
from jax import lax
import numpy as np

D_MODEL = 2048
BATCH = 2
SEQ = 4096
DEPTH = 4
DEC_BATCH = 32
DEC_SEQ = 8
PAST_LEN = 16384
PAGE_SIZE = 128

WINDOW = 128
A_HEADS = 16
A_KV_HEADS = 4
A_HEAD_DIM = 64
A_GROUP = A_HEADS // A_KV_HEADS
A_WIDTH = A_HEADS * A_HEAD_DIM
A_KV_WIDTH = A_KV_HEADS * A_HEAD_DIM
ROPE_THETA = 10000.0
R_HEADS = 8
R_KEY_DIM = 128
R_VAL_DIM = 128
R_QK_WIDTH = R_HEADS * R_KEY_DIM
R_WIDTH = R_HEADS * R_VAL_DIM
R_CHUNK = 128
D_FF = 5632
CONV_W = 3
EPS = 1e-6
IN_SIZES = (A_WIDTH, A_KV_WIDTH, A_KV_WIDTH, R_QK_WIDTH, R_QK_WIDTH, R_WIDTH, R_WIDTH, D_MODEL, D_MODEL)
N_IN = A_WIDTH + 2 * A_KV_WIDTH + 2 * R_QK_WIDTH + 2 * R_WIDTH + 2 * D_MODEL

kernel_name = "hybrid_swa_sink_retention_convffn_step"

F32 = jnp.float32


def rmsnorm(x, g):
    xf = x.astype(F32)
    y = xf * lax.rsqrt(jnp.mean(xf * xf, axis=-1, keepdims=True) + EPS)
    return (y * g.astype(F32)).astype(x.dtype)


def rope(x, pos):
    dh = x.shape[-1]
    half = dh // 2
    inv = ROPE_THETA ** (-2.0 * jnp.arange(half, dtype=F32) / dh)
    ang = pos[:, None] * inv[None, :]
    c = jnp.cos(ang)[:, None, :]
    s = jnp.sin(ang)[:, None, :]
    xf = x.astype(F32)
    x1, x2 = xf[..., :half], xf[..., half:]
    return jnp.concatenate([x1 * c - x2 * s, x2 * c + x1 * s], axis=-1).astype(x.dtype)


def retnet_rotate(x, pos):
    dk = x.shape[-1]
    half = dk // 2
    inv = 1.0 / (10000.0 ** jnp.linspace(0.0, 1.0, half, dtype=F32))
    ang = pos[:, None] * inv[None, :]
    c = jnp.cos(ang)[:, None, :]
    s = jnp.sin(ang)[:, None, :]
    xf = x.astype(F32)
    xe, xo = xf[..., 0::2], xf[..., 1::2]
    out = jnp.stack([xe * c - xo * s, xo * c + xe * s], axis=-1)
    return out.reshape(x.shape).astype(x.dtype)


def sink_softmax(s, valid, sinks):
    sink = sinks.astype(F32).reshape(A_KV_HEADS, A_GROUP)[:, :, None, None]
    s = jnp.where(valid, s, -jnp.inf)
    m = jnp.maximum(jnp.max(s, axis=-1, keepdims=True), sink)
    p = jnp.exp(s - m)
    denom = jnp.sum(p, axis=-1, keepdims=True) + jnp.exp(sink - m)
    return p / denom


def swa_prompt(q, k, v, sinks):
    B, T = q.shape[0], q.shape[1]
    nb = T // WINDOW
    qb = q.reshape(B, nb, WINDOW, A_KV_HEADS, A_GROUP, A_HEAD_DIM)
    pad = jnp.zeros((B, WINDOW, A_KV_HEADS, A_HEAD_DIM), k.dtype)
    kp = jnp.concatenate([pad, k], axis=1).reshape(B, nb + 1, WINDOW, A_KV_HEADS, A_HEAD_DIM)
    vp = jnp.concatenate([pad, v], axis=1).reshape(B, nb + 1, WINDOW, A_KV_HEADS, A_HEAD_DIM)
    kb = jnp.concatenate([kp[:, :-1], kp[:, 1:]], axis=2)
    vb = jnp.concatenate([vp[:, :-1], vp[:, 1:]], axis=2)
    blk = jnp.arange(nb)[:, None, None]
    qi = jnp.arange(WINDOW)[None, :, None] + WINDOW
    ki = jnp.arange(2 * WINDOW)[None, None, :]
    diff = qi - ki
    valid = (diff >= 0) & (diff < WINDOW) & (blk * WINDOW + ki - WINDOW >= 0)
    s = jnp.einsum('bnqhgd,bnkhd->bnhgqk', qb.astype(F32), kb.astype(F32)) * (A_HEAD_DIM ** -0.5)
    p = sink_softmax(s, valid[None, :, None, None], sinks)
    o = jnp.einsum('bnhgqk,bnkhd->bnqhgd', p, vb.astype(F32))
    return o.reshape(B, T, A_WIDTH).astype(q.dtype)


def swa_sample(q, k_new, v_new, k_cache, v_cache, sinks):
    Bd, L = q.shape[0], q.shape[1]
    wc = k_cache.shape[1]
    kk = jnp.concatenate([k_cache.astype(k_new.dtype), k_new], axis=1)
    vv = jnp.concatenate([v_cache.astype(v_new.dtype), v_new], axis=1)
    qpos = PAST_LEN + jnp.arange(L)
    kpos = jnp.concatenate([PAST_LEN - wc + jnp.arange(wc), PAST_LEN + jnp.arange(L)])
    diff = qpos[:, None] - kpos[None, :]
    valid = (diff >= 0) & (diff < WINDOW)
    qg = q.reshape(Bd, L, A_KV_HEADS, A_GROUP, A_HEAD_DIM)
    s = jnp.einsum('bqhgd,bkhd->bhgqk', qg.astype(F32), kk.astype(F32)) * (A_HEAD_DIM ** -0.5)
    p = sink_softmax(s, valid, sinks)
    o = jnp.einsum('bhgqk,bkhd->bqhgd', p, vv.astype(F32))
    return o.reshape(Bd, L, A_WIDTH).astype(q.dtype)


def retention_chunk(q, k, v, S, log_g):
    L = q.shape[1]
    i = jnp.arange(L, dtype=F32)
    diff = i[:, None] - i[None, :]
    D = jnp.where(diff >= 0, jnp.exp(log_g[:, None, None] * jnp.maximum(diff, 0.0)), 0.0)
    scores = jnp.einsum('bihd,bjhd->bhij', q, k) * D[None]
    o = jnp.einsum('bhij,bjhe->bihe', scores, v)
    q_dec = jnp.exp(log_g[None, :] * (i[:, None] + 1.0))
    o = o + jnp.einsum('bihd,bhde->bihe', q, S) * q_dec[None, :, :, None]
    k_dec = jnp.exp(log_g[None, :] * (L - 1.0 - i)[:, None])
    S_new = jnp.exp(log_g * L)[None, :, None, None] * S + jnp.einsum('bjhd,bjhe->bhde', k * k_dec[None, :, :, None], v)
    return o, S_new


def retention_prompt(q, k, v, log_g):
    B, T, H, dk = q.shape
    dv = v.shape[-1]
    nc = T // R_CHUNK

    def to_chunks(a):
        return a.reshape(B, nc, R_CHUNK, H, a.shape[-1]).swapaxes(0, 1)

    def step(S, qkv):
        qc, kc, vc = qkv
        o, S = retention_chunk(qc, kc, vc, S, log_g)
        return S, o

    S0 = jnp.zeros((B, H, dk, dv), F32)
    S, o = lax.scan(step, S0, (to_chunks(q), to_chunks(k), to_chunks(v)))
    return o.swapaxes(0, 1).reshape(B, T, H, dv), S


def layer(x, pos, log_g, k_win, v_win, S_in, conv_buf,
          g_mix, w_in, sinks, w_proj_a, w_proj_b, w_o, g_ffn, w_up, conv_w, conv_b, w_down):
    B, T, _ = x.shape
    h = rmsnorm(x, g_mix)
    z = h @ w_in
    idx = list(np.cumsum(IN_SIZES)[:-1])
    qa, ka, va, qr, kr, vr, gr, ga, gb = jnp.split(z, idx, axis=-1)

    qa = rope(qa.reshape(B, T, A_HEADS, A_HEAD_DIM), pos)
    ka = rope(ka.reshape(B, T, A_KV_HEADS, A_HEAD_DIM), pos)
    va = va.reshape(B, T, A_KV_HEADS, A_HEAD_DIM)
    if k_win is None:
        oa = swa_prompt(qa, ka, va, sinks)
        new_k, new_v = ka[:, -WINDOW:], va[:, -WINDOW:]
    else:
        oa = swa_sample(qa, ka, va, k_win, v_win, sinks)
        new_k, new_v = ka, va

    qr = retnet_rotate(qr.reshape(B, T, R_HEADS, R_KEY_DIM), pos).astype(F32)
    kr = retnet_rotate(kr.reshape(B, T, R_HEADS, R_KEY_DIM), pos).astype(F32) * (R_KEY_DIM ** -0.5)
    vr = vr.reshape(B, T, R_HEADS, R_VAL_DIM).astype(F32)
    if S_in is None:
        orr, S_new = retention_prompt(qr, kr, vr, log_g)
    else:
        orr, S_new = retention_chunk(qr, kr, vr, S_in.astype(F32), log_g)
    orr = orr * lax.rsqrt(jnp.mean(orr * orr, axis=-1, keepdims=True) + EPS)
    ob = orr.reshape(B, T, R_WIDTH).astype(x.dtype) * jax.nn.silu(gr)

    merged = jax.nn.sigmoid(ga) * (oa @ w_proj_a) + jax.nn.sigmoid(gb) * (ob @ w_proj_b)
    x = x + merged @ w_o

    h2 = rmsnorm(x, g_ffn)
    u, g = jnp.split(h2 @ w_up, [D_FF], axis=-1)
    ubuf = jnp.concatenate([conv_buf.astype(u.dtype), u], axis=1)
    conv = conv_b
    for j in range(CONV_W):
        conv = conv + conv_w[j] * ubuf[:, j:j + T]
    f = jax.nn.gelu(conv) * g
    x = x + f @ w_down
    conv_new = ubuf[:, -(CONV_W - 1):]
    return x, new_k, new_v, S_new, conv_new


def setup_inputs(seed: int = 0) -> dict:
    key = jax.random.key(seed)
    ks = jax.random.split(key, 20)
    nrm = jax.random.normal
    return {
        "x_prompt": nrm(ks[0], (BATCH, SEQ, D_MODEL), F32),
        "x_sample": nrm(ks[1], (DEC_BATCH, DEC_SEQ, D_MODEL), F32),
        "cache_win_k": nrm(ks[2], (DEPTH, DEC_BATCH, WINDOW, A_KV_HEADS, A_HEAD_DIM), F32),
        "cache_win_v": nrm(ks[3], (DEPTH, DEC_BATCH, WINDOW, A_KV_HEADS, A_HEAD_DIM), F32),
        "state_ret": 0.1 * nrm(ks[4], (DEPTH, DEC_BATCH, R_HEADS, R_KEY_DIM, R_VAL_DIM), F32),
        "state_conv": nrm(ks[5], (DEPTH, DEC_BATCH, CONV_W - 1, D_FF), F32),
        "g_mix": 1.0 + 0.01 * nrm(ks[6], (DEPTH, D_MODEL), F32),
        "w_in": nrm(ks[7], (DEPTH, D_MODEL, N_IN), F32) * D_MODEL ** -0.5,
        "sinks": 0.5 * nrm(ks[8], (DEPTH, A_HEADS), F32),
        "w_proj_a": nrm(ks[9], (DEPTH, A_WIDTH, D_MODEL), F32) * A_WIDTH ** -0.5,
        "w_proj_b": nrm(ks[10], (DEPTH, R_WIDTH, D_MODEL), F32) * R_WIDTH ** -0.5,
        "w_o": nrm(ks[11], (DEPTH, D_MODEL, D_MODEL), F32) * D_MODEL ** -0.5,
        "g_ffn": 1.0 + 0.01 * nrm(ks[12], (DEPTH, D_MODEL), F32),
        "w_up": nrm(ks[13], (DEPTH, D_MODEL, 2 * D_FF), F32) * D_MODEL ** -0.5,
        "conv_w": nrm(ks[14], (DEPTH, CONV_W, D_FF), F32) * CONV_W ** -0.5,
        "conv_b": 0.01 * nrm(ks[15], (DEPTH, D_FF), F32),
        "w_down": nrm(ks[16], (DEPTH, D_FF, D_MODEL), F32) * D_FF ** -0.5,
        "g_final": 1.0 + 0.01 * nrm(ks[17], (D_MODEL,), F32),
    }


def reference(x_prompt, x_sample, cache_win_k, cache_win_v, state_ret, state_conv,
              g_mix, w_in, sinks, w_proj_a, w_proj_b, w_o, g_ffn, w_up, conv_w, conv_b, w_down, g_final):
    log_g = jnp.log1p(-jnp.exp2(-5.0 - jnp.arange(R_HEADS, dtype=F32)))
    pos_p = jnp.arange(x_prompt.shape[1], dtype=F32)
    pos_s = PAST_LEN + jnp.arange(x_sample.shape[1], dtype=F32)
    xp, xs = x_prompt, x_sample
    kp_l, vp_l, sp_l, cp_l = [], [], [], []
    ks_l, vs_l, ss_l, cs_l = [], [], [], []
    for l in range(DEPTH):
        w = (g_mix[l], w_in[l], sinks[l], w_proj_a[l], w_proj_b[l], w_o[l],
             g_ffn[l], w_up[l], conv_w[l], conv_b[l], w_down[l])
        conv0 = jnp.zeros((xp.shape[0], CONV_W - 1, D_FF), xp.dtype)
        xp, k1, v1, s1, c1 = layer(xp, pos_p, log_g, None, None, None, conv0, *w)
        xs, k2, v2, s2, c2 = layer(xs, pos_s, log_g, cache_win_k[l], cache_win_v[l],
                                   state_ret[l], state_conv[l], *w)
        kp_l.append(k1); vp_l.append(v1); sp_l.append(s1.astype(xp.dtype)); cp_l.append(c1)
        ks_l.append(k2); vs_l.append(v2); ss_l.append(s2.astype(state_ret.dtype)); cs_l.append(c2)
    y_prompt = rmsnorm(xp, g_final)
    y_sample = rmsnorm(xs, g_final)
    return (y_prompt, y_sample,
            jnp.stack(kp_l), jnp.stack(vp_l), jnp.stack(sp_l), jnp.stack(cp_l),
            jnp.stack(ks_l), jnp.stack(vs_l), jnp.stack(ss_l), jnp.stack(cs_l))
```

```python
import functools

import jax
import jax.numpy as jnp
from jax import lax
from jax.experimental import pallas as pl
from jax.experimental.pallas import tpu as pltpu

F32 = jnp.float32
BF16 = jnp.bfloat16

LANES = 128
SUBLANES = 8
VMEM_LIMIT_CAP = 56 * 1024 * 1024

WINDOW = 128
A_HEADS = 16
A_KV_HEADS = 4
A_HEAD_DIM = 64
A_GROUP = A_HEADS // A_KV_HEADS
R_HEADS = 8
R_DIM = 128
R_CHUNK = 128
CONV_W = 3
EPS = 1e-6
ROPE_THETA = 10000.0
PAST_LEN = 16384

NT_DIMS = (((1,), (1,)), ((), ()))
TN_DIMS = (((0,), (0,)), ((), ()))


def _vmem_limit(nbytes):
    return int(min(VMEM_LIMIT_CAP, max(32 * 1024 * 1024, nbytes)))


def _params(semantics, nbytes):
    return pltpu.CompilerParams(dimension_semantics=semantics,
                                vmem_limit_bytes=_vmem_limit(nbytes))


def _rmsnorm_body(x_ref, g_ref, o_ref):
    x = x_ref[...]
    ms = jnp.mean(x * x, axis=-1, keepdims=True)
    o_ref[...] = (x * lax.rsqrt(ms + EPS) * g_ref[...]).astype(o_ref.dtype)


def _rmsnorm(x, g, out_dtype, *, tm, name):
    m, d = x.shape
    return pl.pallas_call(
        _rmsnorm_body,
        grid=(m // tm,),
        in_specs=[pl.BlockSpec((tm, d), lambda i: (i, 0)),
                  pl.BlockSpec((1, d), lambda i: (0, 0))],
        out_specs=pl.BlockSpec((tm, d), lambda i: (i, 0)),
        out_shape=jax.ShapeDtypeStruct((m, d), out_dtype),
        compiler_params=_params(("arbitrary",), 6 * tm * d * 4),
        name=name,
    )(x, g.reshape(1, d))


def _mm_body(x_ref, w_ref, o_ref, wbf_ref):
    @pl.when(pl.program_id(1) == 0)
    def _():
        wbf_ref[...] = w_ref[...].astype(BF16)

    o_ref[...] = jnp.dot(x_ref[...], wbf_ref[...],
                         preferred_element_type=F32).astype(o_ref.dtype)


def _mm_res_body(x_ref, w_ref, r_ref, o_ref, wbf_ref):
    @pl.when(pl.program_id(1) == 0)
    def _():
        wbf_ref[...] = w_ref[...].astype(BF16)

    o_ref[...] = r_ref[...] + jnp.dot(x_ref[...], wbf_ref[...],
                                      preferred_element_type=F32)


def _matmul(x, w, *, tm, tn, out_dtype, residual=None, name):
    m, k = x.shape
    n = w.shape[1]
    grid = (n // tn, m // tm)
    in_specs = [pl.BlockSpec((tm, k), lambda j, i: (i, 0)),
                pl.BlockSpec((k, tn), lambda j, i: (0, j))]
    args = [x, w]
    body = _mm_body
    if residual is not None:
        in_specs.append(pl.BlockSpec((tm, tn), lambda j, i: (i, j)))
        args.append(residual)
        body = _mm_res_body
    nbytes = (2 * tm * k * 2 + 2 * k * tn * 4 + k * tn * 2
              + 6 * tm * tn * 4 + (4 << 20))
    return pl.pallas_call(
        body,
        grid=grid,
        in_specs=in_specs,
        out_specs=pl.BlockSpec((tm, tn), lambda j, i: (i, j)),
        out_shape=jax.ShapeDtypeStruct((m, n), out_dtype),
        scratch_shapes=[pltpu.VMEM((k, tn), BF16)],
        compiler_params=_params(("arbitrary", "arbitrary"), nbytes),
        name=name,
    )(*args)


def _merge_body(oa_ref, ob_ref, ga_ref, gb_ref, wa_ref, wb_ref, o_ref,
                wabf_ref, wbbf_ref):
    @pl.when(pl.program_id(1) == 0)
    def _():
        wabf_ref[...] = wa_ref[...].astype(BF16)
        wbbf_ref[...] = wb_ref[...].astype(BF16)

    a = jnp.dot(oa_ref[...], wabf_ref[...], preferred_element_type=F32)
    b = jnp.dot(ob_ref[...], wbbf_ref[...], preferred_element_type=F32)
    ga = jax.nn.sigmoid(ga_ref[...].astype(F32))
    gb = jax.nn.sigmoid(gb_ref[...].astype(F32))
    o_ref[...] = (ga * a + gb * b).astype(o_ref.dtype)


def _merge(oa, ob, z, wa, wb, *, ga_col, gb_col, tm, tn, name):
    m, k = oa.shape
    n = wa.shape[1]
    ga_blk = ga_col // tn
    gb_blk = gb_col // tn
    nbytes = (4 * tm * k * 2 + 4 * k * tn * 4 + 2 * k * tn * 2
              + 4 * tm * tn * 2 + 8 * tm * tn * 4 + (4 << 20))
    return pl.pallas_call(
        _merge_body,
        grid=(n // tn, m // tm),
        in_specs=[pl.BlockSpec((tm, k), lambda j, i: (i, 0)),
                  pl.BlockSpec((tm, k), lambda j, i: (i, 0)),
                  pl.BlockSpec((tm, tn), lambda j, i: (i, ga_blk + j)),
                  pl.BlockSpec((tm, tn), lambda j, i: (i, gb_blk + j)),
                  pl.BlockSpec((k, tn), lambda j, i: (0, j)),
                  pl.BlockSpec((k, tn), lambda j, i: (0, j))],
        out_specs=pl.BlockSpec((tm, tn), lambda j, i: (i, j)),
        out_shape=jax.ShapeDtypeStruct((m, n), BF16),
        scratch_shapes=[pltpu.VMEM((k, tn), BF16), pltpu.VMEM((k, tn), BF16)],
        compiler_params=_params(("arbitrary", "arbitrary"), nbytes),
        name=name,
    )(oa, ob, z, z, wa, wb)


def _rope_tables(pos):
    half = A_HEAD_DIM // 2
    inv = ROPE_THETA ** (-2.0 * jnp.arange(half, dtype=F32) / A_HEAD_DIM)
    ang = pos[:, None] * inv[None, :]
    c, s = jnp.cos(ang), jnp.sin(ang)
    cos = jnp.concatenate([c, c, c, c], axis=-1)
    sin = jnp.concatenate([-s, s, -s, s], axis=-1)
    return cos, sin


def _retrot_tables(pos):
    half = R_DIM // 2
    inv = 1.0 / (10000.0 ** jnp.linspace(0.0, 1.0, half, dtype=F32))
    ang = pos[:, None] * inv[None, :]
    c, s = jnp.cos(ang), jnp.sin(ang)
    cos = jnp.stack([c, c], axis=-1).reshape(pos.shape[0], R_DIM)
    sin = jnp.stack([-s, s], axis=-1).reshape(pos.shape[0], R_DIM)
    return cos, sin


def _rope128(x, cos, sin):
    lane = lax.broadcasted_iota(jnp.int32, x.shape, 1)
    first = (lane & (A_HEAD_DIM - 1)) < (A_HEAD_DIM // 2)
    partner = jnp.where(first, pltpu.roll(x, LANES - 32, 1), pltpu.roll(x, 32, 1))
    return x * cos + partner * sin


def _retrot128(x, cos, sin):
    lane = lax.broadcasted_iota(jnp.int32, x.shape, 1)
    even = (lane & 1) == 0
    partner = jnp.where(even, pltpu.roll(x, LANES - 1, 1), pltpu.roll(x, 1, 1))
    return x * cos + partner * sin


def _swa_heads(q_chunks, k_chunks, v_chunks, valid, sink_ref):
    rows = q_chunks[0].shape[0]
    lane = lax.broadcasted_iota(jnp.int32, (rows, LANES), 1)
    low = lane < A_HEAD_DIM
    outs = []
    for pair in range(A_HEADS // 2):
        halves = []
        for hpos in range(2):
            h = 2 * pair + hpos
            kv = h // A_GROUP
            c, kpos = kv // 2, kv % 2
            qc = q_chunks[pair]
            if hpos != kpos:
                qc = pltpu.roll(qc, A_HEAD_DIM, 1)
            keep = low if kpos == 0 else jnp.logical_not(low)
            qm = jnp.where(keep, qc, 0.0).astype(BF16)
            s = lax.dot_general(qm, k_chunks[c], NT_DIMS, preferred_element_type=F32)
            s = jnp.where(valid, s, -jnp.inf)
            sink = sink_ref[h]
            m = jnp.maximum(jnp.max(s, axis=-1, keepdims=True), sink)
            p = jnp.exp(s - m)
            denom = jnp.sum(p, axis=-1, keepdims=True) + jnp.exp(sink - m)
            o = jnp.dot(p.astype(BF16), v_chunks[c], preferred_element_type=F32)
            o = o / denom
            if hpos != kpos:
                o = pltpu.roll(o, A_HEAD_DIM, 1)
            halves.append(o)
        outs.append(jnp.where(low, halves[0], halves[1]))
    return outs


def _swa_prompt_body(sink_ref, q_ref, k_ref, vc_ref, vp_ref, cos_ref, sin_ref,
                     o_ref, kr_ref, kprev_ref):
    n = pl.program_id(1)
    cos = cos_ref[...]
    sin = sin_ref[...]

    @pl.when(n == 0)
    def _():
        kprev_ref[...] = jnp.zeros_like(kprev_ref)

    k_chunks, v_chunks = [], []
    k_new = []
    for c in range(2):
        sl = slice(c * LANES, (c + 1) * LANES)
        kc = _rope128(k_ref[:, sl].astype(F32), cos, sin)
        kr_ref[:, sl] = kc
        kc = kc.astype(BF16)
        k_new.append(kc)
        k_chunks.append(jnp.concatenate([kprev_ref[:, sl], kc], axis=0))
        v_chunks.append(jnp.concatenate([vp_ref[:, sl], vc_ref[:, sl]], axis=0))

    scale = A_HEAD_DIM ** -0.5
    q_chunks = [
        _rope128(q_ref[:, c * LANES:(c + 1) * LANES].astype(F32), cos, sin) * scale
        for c in range(A_HEADS // 2)]

    qi = lax.broadcasted_iota(jnp.int32, (WINDOW, 2 * WINDOW), 0) + WINDOW
    ki = lax.broadcasted_iota(jnp.int32, (WINDOW, 2 * WINDOW), 1)
    diff = qi - ki
    valid = (diff >= 0) & (diff < WINDOW) & ((n * WINDOW + ki - WINDOW) >= 0)

    outs = _swa_heads(q_chunks, k_chunks, v_chunks, valid, sink_ref)
    for c, o in enumerate(outs):
        o_ref[:, c * LANES:(c + 1) * LANES] = o.astype(o_ref.dtype)
    for c in range(2):
        kprev_ref[:, c * LANES:(c + 1) * LANES] = k_new[c]


def _swa_prompt(z, sinks, cos, sin, *, batch, seq, name):
    nb = seq // WINDOW
    a_width = A_HEADS * A_HEAD_DIM
    kv_width = A_KV_HEADS * A_HEAD_DIM
    k_blk = a_width // kv_width
    v_blk = k_blk + 1
    rows = batch * seq
    return pl.pallas_call(
        _swa_prompt_body,
        grid=(batch, nb),
        in_specs=[
            pl.BlockSpec(memory_space=pltpu.SMEM),
            pl.BlockSpec((WINDOW, a_width), lambda b, n: (b * nb + n, 0)),
            pl.BlockSpec((WINDOW, kv_width), lambda b, n: (b * nb + n, k_blk)),
            pl.BlockSpec((WINDOW, kv_width), lambda b, n: (b * nb + n, v_blk)),
            pl.BlockSpec((WINDOW, kv_width),
                         lambda b, n: (b * nb + jnp.maximum(n - 1, 0), v_blk)),
            pl.BlockSpec((WINDOW, LANES), lambda b, n: (n, 0)),
            pl.BlockSpec((WINDOW, LANES), lambda b, n: (n, 0)),
        ],
        out_specs=[
            pl.BlockSpec((WINDOW, a_width), lambda b, n: (b * nb + n, 0)),
            pl.BlockSpec((WINDOW, kv_width), lambda b, n: (b * nb + n, 0)),
        ],
        out_shape=[jax.ShapeDtypeStruct((rows, a_width), BF16),
                   jax.ShapeDtypeStruct((rows, kv_width), F32)],
        scratch_shapes=[pltpu.VMEM((WINDOW, kv_width), BF16)],
        compiler_params=_params(("arbitrary", "arbitrary"), 0),
        name=name,
    )(sinks, z, z, z, z, cos, sin)


SWA_SAMPLE_SEQS = 4


def _swa_sample_body(sink_ref, q_ref, k_ref, v_ref, ck_ref, cv_ref, cos_ref,
                     sin_ref, o_ref, kr_ref, *, dec_seq):
    cos = cos_ref[...]
    sin = sin_ref[...]
    g = SWA_SAMPLE_SEQS
    rows = g * dec_seq
    ncache = g * WINDOW
    k_chunks, v_chunks = [], []
    for c in range(2):
        sl = slice(c * LANES, (c + 1) * LANES)
        kc = _rope128(k_ref[:, sl].astype(F32), cos, sin)
        kr_ref[:, sl] = kc
        k_chunks.append(jnp.concatenate(
            [ck_ref[:, sl].astype(BF16), kc.astype(BF16)], axis=0))
        v_chunks.append(jnp.concatenate(
            [cv_ref[:, sl].astype(BF16), v_ref[:, sl]], axis=0))

    scale = A_HEAD_DIM ** -0.5
    q_chunks = [
        _rope128(q_ref[:, c * LANES:(c + 1) * LANES].astype(F32), cos, sin) * scale
        for c in range(A_HEADS // 2)]

    shape = (rows, ncache + rows)
    r = lax.broadcasted_iota(jnp.int32, shape, 0)
    col = lax.broadcasted_iota(jnp.int32, shape, 1)
    log_l = dec_seq.bit_length() - 1
    log_w = WINDOW.bit_length() - 1
    seq_q = r >> log_l
    i = r & (dec_seq - 1)
    in_cache = col < ncache
    cnew = col - ncache
    seq_k = jnp.where(in_cache, col >> log_w, cnew >> log_l)
    ok = ((in_cache & ((col & (WINDOW - 1)) > i))
          | (jnp.logical_not(in_cache) & ((cnew & (dec_seq - 1)) <= i)))
    valid = (seq_k == seq_q) & ok

    outs = _swa_heads(q_chunks, k_chunks, v_chunks, valid, sink_ref)
    for c, o in enumerate(outs):
        o_ref[:, c * LANES:(c + 1) * LANES] = o.astype(o_ref.dtype)


def _swa_sample(z, cache_k, cache_v, sinks, cos, sin, *, row0, dec_batch, dec_seq, name):
    g = SWA_SAMPLE_SEQS
    rows = g * dec_seq
    a_width = A_HEADS * A_HEAD_DIM
    kv_width = A_KV_HEADS * A_HEAD_DIM
    k_blk = a_width // kv_width
    v_blk = k_blk + 1
    r0 = row0 // rows
    ck = cache_k.reshape(dec_batch * WINDOW, kv_width)
    cv = cache_v.reshape(dec_batch * WINDOW, kv_width)
    return pl.pallas_call(
        functools.partial(_swa_sample_body, dec_seq=dec_seq),
        grid=(dec_batch // g,),
        in_specs=[
            pl.BlockSpec(memory_space=pltpu.SMEM),
            pl.BlockSpec((rows, a_width), lambda s: (r0 + s, 0)),
            pl.BlockSpec((rows, kv_width), lambda s: (r0 + s, k_blk)),
            pl.BlockSpec((rows, kv_width), lambda s: (r0 + s, v_blk)),
            pl.BlockSpec((g * WINDOW, kv_width), lambda s: (s, 0)),
            pl.BlockSpec((g * WINDOW, kv_width), lambda s: (s, 0)),
            pl.BlockSpec((rows, LANES), lambda s: (0, 0)),
            pl.BlockSpec((rows, LANES), lambda s: (0, 0)),
        ],
        out_specs=[
            pl.BlockSpec((rows, a_width), lambda s: (s, 0)),
            pl.BlockSpec((rows, kv_width), lambda s: (s, 0)),
        ],
        out_shape=[jax.ShapeDtypeStruct((dec_batch * dec_seq, a_width), BF16),
                   jax.ShapeDtypeStruct((dec_batch * dec_seq, kv_width), F32)],
        compiler_params=_params(("arbitrary",), 0),
        name=name,
    )(sinks, z, z, z, ck, cv, cos, sin)


RET_HEADS_PER_STEP = 4
RET_BLOCK = RET_HEADS_PER_STEP * R_DIM


def _decay_tables(length, log_g):
    i = jnp.arange(length, dtype=F32)
    diff = i[:, None] - i[None, :]
    d = jnp.where(diff >= 0, jnp.exp(log_g[:, None, None] * jnp.maximum(diff, 0.0)), 0.0)
    q_dec = jnp.exp(log_g[None, :] * (i[:, None] + 1.0))
    k_dec = jnp.exp(log_g[None, :] * (length - 1.0 - i)[:, None])
    qd = jnp.broadcast_to(q_dec.T[:, :, None], (R_HEADS, length, R_DIM))
    kd = jnp.broadcast_to(k_dec.T[:, :, None], (R_HEADS, length, R_DIM))
    g_len = jnp.exp(log_g * length)
    return d, qd, kd, g_len


def _group_norm_gate(o, gate):
    o = o * lax.rsqrt(jnp.mean(o * o, axis=-1, keepdims=True) + EPS)
    return o * (gate * jax.nn.sigmoid(gate))


def _ret_prompt_body(gl_ref, q_ref, k_ref, v_ref, g_ref, cos_ref, sin_ref,
                     d_ref, qd_ref, kd_ref, o_ref, s_ref):
    hh = pl.program_id(1)
    c = pl.program_id(2)
    cos = cos_ref[...]
    sin = sin_ref[...]

    @pl.when(c == 0)
    def _():
        s_ref[...] = jnp.zeros_like(s_ref)

    scale = R_DIM ** -0.5
    for hl in range(RET_HEADS_PER_STEP):
        sl = slice(hl * R_DIM, (hl + 1) * R_DIM)
        q = _retrot128(q_ref[:, sl].astype(F32), cos, sin)
        k = _retrot128(k_ref[:, sl].astype(F32), cos, sin) * scale
        v = v_ref[:, sl]
        qb = q.astype(BF16)
        state = s_ref[0, hl]
        scores = lax.dot_general(qb, k.astype(BF16), NT_DIMS,
                                 preferred_element_type=F32) * d_ref[hl]
        o = jnp.dot(scores.astype(BF16), v, preferred_element_type=F32)
        cross = jnp.dot(qb, state.astype(BF16), preferred_element_type=F32)
        o = o + cross * qd_ref[hl]
        kd = (k * kd_ref[hl]).astype(BF16)
        s_ref[0, hl] = gl_ref[hh * RET_HEADS_PER_STEP + hl] * state + lax.dot_general(
            kd, v, TN_DIMS, preferred_element_type=F32)
        o_ref[:, sl] = _group_norm_gate(o, g_ref[:, sl].astype(F32)).astype(o_ref.dtype)


def _ret_prompt(z, cos, sin, tabs, *, col0, batch, seq, name):
    d, qd, kd, g_len = tabs
    nc = seq // R_CHUNK
    nh = R_HEADS // RET_HEADS_PER_STEP
    width = R_HEADS * R_DIM
    qb, kb, vb, gb = [(col0 + t * width) // RET_BLOCK for t in range(4)]
    hp = RET_HEADS_PER_STEP

    def zspec(blk):
        return pl.BlockSpec((R_CHUNK, RET_BLOCK),
                            lambda b, h, c: (b * nc + c, blk + h))

    def tspec():
        return pl.BlockSpec((hp, R_CHUNK, R_DIM), lambda b, h, c: (h, 0, 0))

    return pl.pallas_call(
        _ret_prompt_body,
        grid=(batch, nh, nc),
        in_specs=[
            pl.BlockSpec(memory_space=pltpu.SMEM),
            zspec(qb), zspec(kb), zspec(vb), zspec(gb),
            pl.BlockSpec((R_CHUNK, R_DIM), lambda b, h, c: (c, 0)),
            pl.BlockSpec((R_CHUNK, R_DIM), lambda b, h, c: (c, 0)),
            tspec(), tspec(), tspec(),
        ],
        out_specs=[
            pl.BlockSpec((R_CHUNK, RET_BLOCK), lambda b, h, c: (b * nc + c, h)),
            pl.BlockSpec((1, hp, R_DIM, R_DIM), lambda b, h, c: (b, h, 0, 0)),
        ],
        out_shape=[jax.ShapeDtypeStruct((batch * seq, width), BF16),
                   jax.ShapeDtypeStruct((batch, R_HEADS, R_DIM, R_DIM), F32)],
        compiler_params=_params(("arbitrary", "arbitrary", "arbitrary"), 0),
        name=name,
    )(g_len, z, z, z, z, cos, sin, d, qd, kd)


def _ret_sample_body(gl_ref, q_ref, k_ref, v_ref, g_ref, cos_ref, sin_ref,
                     d_ref, qd_ref, kd_ref, s_ref, o_ref, so_ref, *, dec_seq):
    hh = pl.program_id(1)
    cos = cos_ref[...]
    sin = sin_ref[...]
    nseq = R_CHUNK // dec_seq
    log_l = dec_seq.bit_length() - 1
    row_seq = lax.broadcasted_iota(jnp.int32, (R_CHUNK, R_DIM), 0) >> log_l
    scale = R_DIM ** -0.5
    for hl in range(RET_HEADS_PER_STEP):
        sl = slice(hl * R_DIM, (hl + 1) * R_DIM)
        q = _retrot128(q_ref[:, sl].astype(F32), cos, sin)
        k = _retrot128(k_ref[:, sl].astype(F32), cos, sin) * scale
        v = v_ref[:, sl]
        qb = q.astype(BF16)
        scores = lax.dot_general(qb, k.astype(BF16), NT_DIMS,
                                 preferred_element_type=F32) * d_ref[hl]
        o = jnp.dot(scores.astype(BF16), v, preferred_element_type=F32)
        kd = k * kd_ref[hl]
        gl = gl_ref[hh * RET_HEADS_PER_STEP + hl]
        cross = jnp.zeros((R_CHUNK, R_DIM), F32)
        for s in range(nseq):
            mine = row_seq == s
            state = s_ref[s, hl]
            cs = jnp.dot(qb, state.astype(BF16), preferred_element_type=F32)
            cross = jnp.where(mine, cs, cross)
            ks = jnp.where(mine, kd, 0.0).astype(BF16)
            so_ref[s, hl] = gl * state + lax.dot_general(
                ks, v, TN_DIMS, preferred_element_type=F32)
        o = o + cross * qd_ref[hl]
        o_ref[:, sl] = _group_norm_gate(o, g_ref[:, sl].astype(F32)).astype(o_ref.dtype)


def _ret_sample(z, state, cos, sin, tabs, *, row0, col0, dec_batch, dec_seq, name):
    d, qd, kd, g_len = tabs
    nseq = R_CHUNK // dec_seq
    steps = dec_batch // nseq
    nh = R_HEADS // RET_HEADS_PER_STEP
    width = R_HEADS * R_DIM
    qb, kb, vb, gb = [(col0 + t * width) // RET_BLOCK for t in range(4)]
    r0 = row0 // R_CHUNK
    hp = RET_HEADS_PER_STEP

    def zspec(blk):
        return pl.BlockSpec((R_CHUNK, RET_BLOCK), lambda s, h: (r0 + s, blk + h))

    def tspec():
        return pl.BlockSpec((hp, R_CHUNK, R_DIM), lambda s, h: (h, 0, 0))

    sspec = pl.BlockSpec((nseq, hp, R_DIM, R_DIM), lambda s, h: (s, h, 0, 0))
    return pl.pallas_call(
        functools.partial(_ret_sample_body, dec_seq=dec_seq),
        grid=(steps, nh),
        in_specs=[
            pl.BlockSpec(memory_space=pltpu.SMEM),
            zspec(qb), zspec(kb), zspec(vb), zspec(gb),
            pl.BlockSpec((R_CHUNK, R_DIM), lambda s, h: (0, 0)),
            pl.BlockSpec((R_CHUNK, R_DIM), lambda s, h: (0, 0)),
            tspec(), tspec(), tspec(),
            sspec,
        ],
        out_specs=[
            pl.BlockSpec((R_CHUNK, RET_BLOCK), lambda s, h: (s, h)),
            sspec,
        ],
        out_shape=[jax.ShapeDtypeStruct((dec_batch * dec_seq, width), BF16),
                   jax.ShapeDtypeStruct(state.shape, F32)],
        compiler_params=_params(("arbitrary", "arbitrary"),
                                4 * nseq * hp * R_DIM * R_DIM * 4 + (16 << 20)),
        name=name,
    )(g_len, z, z, z, z, cos, sin, d, qd, kd, state)


CONV_ROWS = 256
CONV_COLS = 512


def _gelu_tanh(x):
    c = (2.0 / jnp.pi) ** 0.5
    return 0.5 * x * (1.0 + jnp.tanh(c * (x + 0.044715 * (x * x * x))))


def _conv_body(u_ref, g_ref, w_ref, b_ref, e0_ref, e1_ref, f_ref, carry_ref,
               *, tiles_per_seq, prompt_tiles, dec_seq):
    t = pl.program_id(1)
    u = u_ref[...].astype(F32)
    gate = g_ref[...].astype(F32)
    w = w_ref[...]
    row = lax.broadcasted_iota(jnp.int32, u.shape, 0)
    r1 = pltpu.roll(u, 1, 0)
    r2 = pltpu.roll(u, 2, 0)

    def finish(prev1, prev2):
        conv = b_ref[...] + w[0:1, :] * prev2
        conv = conv + w[1:2, :] * prev1
        conv = conv + w[2:3, :] * u
        f_ref[...] = (_gelu_tanh(conv) * gate).astype(f_ref.dtype)

    @pl.when(t < prompt_tiles)
    def _():
        @pl.when(t % tiles_per_seq == 0)
        def _():
            carry_ref[...] = jnp.zeros_like(carry_ref)

        last1 = carry_ref[SUBLANES - 1:SUBLANES, :]
        last2 = carry_ref[SUBLANES - 2:SUBLANES - 1, :]
        prev1 = jnp.where(row == 0, last1, r1)
        prev2 = jnp.where(row == 0, last2, jnp.where(row == 1, last1, r2))
        finish(prev1, prev2)
        carry_ref[...] = u[CONV_ROWS - SUBLANES:, :]

    @pl.when(t >= prompt_tiles)
    def _():
        i = row & (dec_seq - 1)
        e0 = e0_ref[...]
        e1 = e1_ref[...]
        prev1 = jnp.where(i == 0, e1, r1)
        prev2 = jnp.where(i == 0, e0, jnp.where(i == 1, e1, r2))
        finish(prev1, prev2)


def _conv_gate(ug, conv_w, conv_b, e0, e1, *, d_ff, seq, prompt_rows, dec_seq, name):
    m = ug.shape[0]
    nj = d_ff // CONV_COLS
    nt = m // CONV_ROWS
    body = functools.partial(_conv_body, tiles_per_seq=seq // CONV_ROWS,
                             prompt_tiles=prompt_rows // CONV_ROWS, dec_seq=dec_seq)
    return pl.pallas_call(
        body,
        grid=(nj, nt),
        in_specs=[
            pl.BlockSpec((CONV_ROWS, CONV_COLS), lambda j, t: (t, j)),
            pl.BlockSpec((CONV_ROWS, CONV_COLS), lambda j, t: (t, nj + j)),
            pl.BlockSpec((CONV_W, CONV_COLS), lambda j, t: (0, j)),
            pl.BlockSpec((1, CONV_COLS), lambda j, t: (0, j)),
            pl.BlockSpec((CONV_ROWS, CONV_COLS), lambda j, t: (0, j)),
            pl.BlockSpec((CONV_ROWS, CONV_COLS), lambda j, t: (0, j)),
        ],
        out_specs=pl.BlockSpec((CONV_ROWS, CONV_COLS), lambda j, t: (t, j)),
        out_shape=jax.ShapeDtypeStruct((m, d_ff), BF16),
        scratch_shapes=[pltpu.VMEM((SUBLANES, CONV_COLS), F32)],
        compiler_params=_params(("arbitrary", "arbitrary"), 0),
        name=name,
    )(ug, ug, conv_w, conv_b.reshape(1, d_ff), e0, e1)


TM = 768
TM_DOWN = 384
TN_IN = 512
TN_UP = 1024
TN_OUT = 512


def kernel(x_prompt, x_sample, cache_win_k, cache_win_v, state_ret, state_conv,
           g_mix, w_in, sinks, w_proj_a, w_proj_b, w_o, g_ffn, w_up, conv_w,
           conv_b, w_down, g_final):
    batch, seq, d_model = x_prompt.shape
    dec_batch, dec_seq, _ = x_sample.shape
    depth = w_in.shape[0]
    d_ff = w_down.shape[1]
    mp = batch * seq
    ms = dec_batch * dec_seq
    a_width = A_HEADS * A_HEAD_DIM
    kv_width = A_KV_HEADS * A_HEAD_DIM
    r_width = R_HEADS * R_DIM
    ret_col0 = a_width + 2 * kv_width
    ga_col = ret_col0 + 4 * r_width
    gb_col = ga_col + d_model
    assert dec_seq & (dec_seq - 1) == 0 and R_CHUNK % dec_seq == 0
    assert CONV_ROWS == ms and seq % CONV_ROWS == 0

    log_g = jnp.log1p(-jnp.exp2(-5.0 - jnp.arange(R_HEADS, dtype=F32)))
    pos_p = jnp.arange(seq, dtype=F32)
    pos_s = PAST_LEN + jnp.arange(dec_seq, dtype=F32)
    rope_p = _rope_tables(pos_p)
    rope_s = tuple(jnp.tile(t, (SWA_SAMPLE_SEQS, 1)) for t in _rope_tables(pos_s))
    rrot_p = _retrot_tables(pos_p)
    nseq = R_CHUNK // dec_seq
    rrot_s = tuple(jnp.tile(t, (nseq, 1)) for t in _retrot_tables(pos_s))
    tabs_p = _decay_tables(R_CHUNK, log_g)
    d8, qd8, kd8, gl8 = _decay_tables(dec_seq, log_g)
    eye = jnp.eye(nseq, dtype=F32)
    d_s = jnp.einsum('st,hij->hsitj', eye, d8).reshape(R_HEADS, R_CHUNK, R_CHUNK)
    tabs_s = (d_s, jnp.tile(qd8, (1, nseq, 1)), jnp.tile(kd8, (1, nseq, 1)), gl8)

    x = jnp.concatenate([x_prompt.reshape(mp, d_model), x_sample.reshape(ms, d_model)], axis=0)

    kp_l, vp_l, sp_l, cp_l = [], [], [], []
    ks_l, vs_l, ss_l, cs_l = [], [], [], []
    for l in range(depth):
        h = _rmsnorm(x, g_mix[l], BF16, tm=TM, name=f"norm_mix{l}")
        z = _matmul(h, w_in[l], tm=TM, tn=TN_IN, out_dtype=BF16, name=f"proj_in{l}")

        oa_p, kr_p = _swa_prompt(z, sinks[l], *rope_p, batch=batch, seq=seq,
                                 name=f"swa_prompt{l}")
        oa_s, kr_s = _swa_sample(z, cache_win_k[l], cache_win_v[l], sinks[l], *rope_s,
                                 row0=mp, dec_batch=dec_batch, dec_seq=dec_seq,
                                 name=f"swa_sample{l}")
        ob_p, s_p = _ret_prompt(z, *rrot_p, tabs_p, col0=ret_col0, batch=batch,
                                seq=seq, name=f"ret_prompt{l}")
        ob_s, s_s = _ret_sample(z, state_ret[l], *rrot_s, tabs_s, row0=mp,
                                col0=ret_col0, dec_batch=dec_batch, dec_seq=dec_seq,
                                name=f"ret_sample{l}")
        oa = jnp.concatenate([oa_p, oa_s], axis=0)
        ob = jnp.concatenate([ob_p, ob_s], axis=0)
        merged = _merge(oa, ob, z, w_proj_a[l], w_proj_b[l], ga_col=ga_col,
                        gb_col=gb_col, tm=TM, tn=TN_OUT, name=f"merge{l}")
        x = _matmul(merged, w_o[l], tm=TM, tn=TN_OUT, out_dtype=F32, residual=x,
                    name=f"proj_out{l}")

        h2 = _rmsnorm(x, g_ffn[l], BF16, tm=TM, name=f"norm_ffn{l}")
        ug = _matmul(h2, w_up[l], tm=TM, tn=TN_UP, out_dtype=BF16, name=f"proj_up{l}")
        e0 = jnp.repeat(state_conv[l, :, 0, :], dec_seq, axis=0)
        e1 = jnp.repeat(state_conv[l, :, 1, :], dec_seq, axis=0)
        f = _conv_gate(ug, conv_w[l], conv_b[l], e0, e1, d_ff=d_ff, seq=seq,
                       prompt_rows=mp, dec_seq=dec_seq, name=f"conv_gate{l}")
        x = _matmul(f, w_down[l], tm=TM_DOWN, tn=TN_OUT, out_dtype=F32, residual=x,
                    name=f"proj_down{l}")

        z_p = z[:mp].reshape(batch, seq, -1)
        z_s = z[mp:].reshape(dec_batch, dec_seq, -1)
        kp_l.append(kr_p.reshape(batch, seq, A_KV_HEADS, A_HEAD_DIM)[:, -WINDOW:])
        vp_l.append(z_p[:, -WINDOW:, a_width + kv_width:ret_col0].astype(F32)
                    .reshape(batch, WINDOW, A_KV_HEADS, A_HEAD_DIM))
        sp_l.append(s_p)
        cp_l.append(ug[:mp, :d_ff].reshape(batch, seq, d_ff)[:, -(CONV_W - 1):].astype(F32))
        ks_l.append(kr_s.reshape(dec_batch, dec_seq, A_KV_HEADS, A_HEAD_DIM))
        vs_l.append(z_s[:, :, a_width + kv_width:ret_col0].astype(F32)
                    .reshape(dec_batch, dec_seq, A_KV_HEADS, A_HEAD_DIM))
        ss_l.append(s_s)
        cs_l.append(ug[mp:, :d_ff].reshape(dec_batch, dec_seq, d_ff)[:, -(CONV_W - 1):]
                    .astype(F32))

    y = _rmsnorm(x, g_final, F32, tm=TM, name="norm_final")
    y_prompt = y[:mp].reshape(batch, seq, d_model)
    y_sample = y[mp:].reshape(dec_batch, dec_seq, d_model)
    return (y_prompt, y_sample,
            jnp.stack(kp_l), jnp.stack(vp_l), jnp.stack(sp_l), jnp.stack(cp_l),
            jnp.stack(ks_l), jnp.stack(vs_l), jnp.stack(ss_l), jnp.stack(cs_l))
```

```python
import functools

import jax
import jax.numpy as jnp
import numpy as np
from jax import lax
from jax.experimental import pallas as pl
from jax.experimental.pallas import tpu as pltpu

F32 = jnp.float32
BF16 = jnp.bfloat16

LANES = 128
SUBLANES = 8
MXU_COLS = 256
FIX_ROWS = 2 * SUBLANES
VMEM_LIMIT_CAP = 56 * 1024 * 1024

WINDOW = 128
A_HEADS = 16
A_KV_HEADS = 4
A_HEAD_DIM = 64
A_GROUP = A_HEADS // A_KV_HEADS
R_HEADS = 8
R_DIM = 128
R_CHUNK = 128
CONV_W = 3
EPS = 1e-6
ROPE_THETA = 10000.0
PAST_LEN = 16384

NT_DIMS = (((1,), (1,)), ((), ()))
TN_DIMS = (((0,), (0,)), ((), ()))


def _vmem_limit(nbytes):
    return int(min(VMEM_LIMIT_CAP, max(32 * 1024 * 1024, nbytes)))


def _params(semantics, nbytes):
    return pltpu.CompilerParams(dimension_semantics=semantics,
                                vmem_limit_bytes=_vmem_limit(nbytes))


def _rmsnorm_body(x_ref, g_ref, o_ref):
    x = x_ref[...]
    ms = jnp.mean(x * x, axis=-1, keepdims=True)
    o_ref[...] = (x * lax.rsqrt(ms + EPS) * g_ref[...]).astype(o_ref.dtype)


def _rmsnorm(x, g, layer, out_dtype, *, tm, name):
    m, d = x.shape
    return pl.pallas_call(
        _rmsnorm_body,
        grid=(m // tm,),
        in_specs=[pl.BlockSpec((tm, d), lambda i: (i, 0)),
                  pl.BlockSpec((None, 1, d), lambda i: (layer, 0, 0))],
        out_specs=pl.BlockSpec((tm, d), lambda i: (i, 0)),
        out_shape=jax.ShapeDtypeStruct((m, d), out_dtype),
        compiler_params=_params(("arbitrary",), 6 * tm * d * 4),
        name=name,
    )(x, g)


def _mm_body(x_ref, w_ref, o_ref, wbf_ref):
    @pl.when(pl.program_id(1) == 0)
    def _():
        wbf_ref[...] = w_ref[...].astype(BF16)

    o_ref[...] = jnp.dot(x_ref[...], wbf_ref[...],
                         preferred_element_type=F32).astype(o_ref.dtype)


def _mm_res_body(x_ref, w_ref, r_ref, o_ref, wbf_ref):
    @pl.when(pl.program_id(1) == 0)
    def _():
        wbf_ref[...] = w_ref[...].astype(BF16)

    o_ref[...] = r_ref[...] + jnp.dot(x_ref[...], wbf_ref[...],
                                      preferred_element_type=F32)


def _matmul(x, w, layer, *, tm, tn, out_dtype, residual=None, name):
    m, k = x.shape
    n = w.shape[2]
    grid = (n // tn, m // tm)
    in_specs = [pl.BlockSpec((tm, k), lambda j, i: (i, 0)),
                pl.BlockSpec((None, k, tn), lambda j, i: (layer, 0, j))]
    args = [x, w]
    body = _mm_body
    if residual is not None:
        in_specs.append(pl.BlockSpec((tm, tn), lambda j, i: (i, j)))
        args.append(residual)
        body = _mm_res_body
    nbytes = (2 * tm * k * 2 + 2 * k * tn * 4 + k * tn * 2
              + 6 * tm * tn * 4 + (4 << 20))
    return pl.pallas_call(
        body,
        grid=grid,
        in_specs=in_specs,
        out_specs=pl.BlockSpec((tm, tn), lambda j, i: (i, j)),
        out_shape=jax.ShapeDtypeStruct((m, n), out_dtype),
        scratch_shapes=[pltpu.VMEM((k, tn), BF16)],
        compiler_params=_params(("arbitrary", "arbitrary"), nbytes),
        name=name,
    )(*args)


def _merge_body(oa_ref, ob_ref, ga_ref, gb_ref, wa_ref, wb_ref, o_ref,
                wabf_ref, wbbf_ref):
    @pl.when(pl.program_id(1) == 0)
    def _():
        wabf_ref[...] = wa_ref[...].astype(BF16)
        wbbf_ref[...] = wb_ref[...].astype(BF16)

    a = jnp.dot(oa_ref[...], wabf_ref[...], preferred_element_type=F32)
    b = jnp.dot(ob_ref[...], wbbf_ref[...], preferred_element_type=F32)
    ga = jax.nn.sigmoid(ga_ref[...].astype(F32))
    gb = jax.nn.sigmoid(gb_ref[...].astype(F32))
    o_ref[...] = (ga * a + gb * b).astype(o_ref.dtype)


def _merge(oa, ob, z, wa, wb, layer, *, ga_col, gb_col, tm, tn, name):
    m, k = oa.shape
    n = wa.shape[2]
    ga_blk = ga_col // tn
    gb_blk = gb_col // tn
    nbytes = (4 * tm * k * 2 + 4 * k * tn * 4 + 2 * k * tn * 2
              + 4 * tm * tn * 2 + 8 * tm * tn * 4 + (4 << 20))
    return pl.pallas_call(
        _merge_body,
        grid=(n // tn, m // tm),
        in_specs=[pl.BlockSpec((tm, k), lambda j, i: (i, 0)),
                  pl.BlockSpec((tm, k), lambda j, i: (i, 0)),
                  pl.BlockSpec((tm, tn), lambda j, i: (i, ga_blk + j)),
                  pl.BlockSpec((tm, tn), lambda j, i: (i, gb_blk + j)),
                  pl.BlockSpec((None, k, tn), lambda j, i: (layer, 0, j)),
                  pl.BlockSpec((None, k, tn), lambda j, i: (layer, 0, j))],
        out_specs=pl.BlockSpec((tm, tn), lambda j, i: (i, j)),
        out_shape=jax.ShapeDtypeStruct((m, n), BF16),
        scratch_shapes=[pltpu.VMEM((k, tn), BF16), pltpu.VMEM((k, tn), BF16)],
        compiler_params=_params(("arbitrary", "arbitrary"), nbytes),
        name=name,
    )(oa, ob, z, z, wa, wb)


def _rope_tables(pos):
    half = A_HEAD_DIM // 2
    inv = ROPE_THETA ** (-2.0 * jnp.arange(half, dtype=F32) / A_HEAD_DIM)
    ang = pos[:, None] * inv[None, :]
    c, s = jnp.cos(ang), jnp.sin(ang)
    cos = jnp.concatenate([c, c, c, c], axis=-1)
    sin = jnp.concatenate([-s, s, -s, s], axis=-1)
    return cos, sin


def _retrot_tables(pos):
    half = R_DIM // 2
    inv = 1.0 / (10000.0 ** jnp.linspace(0.0, 1.0, half, dtype=F32))
    ang = pos[:, None] * inv[None, :]
    c, s = jnp.cos(ang), jnp.sin(ang)
    cos = jnp.stack([c, c], axis=-1).reshape(pos.shape[0], R_DIM)
    sin = jnp.stack([-s, s], axis=-1).reshape(pos.shape[0], R_DIM)
    return cos, sin


def _rope128(x, cos, sin):
    lane = lax.broadcasted_iota(jnp.int32, x.shape, 1)
    first = (lane & (A_HEAD_DIM - 1)) < (A_HEAD_DIM // 2)
    partner = jnp.where(first, pltpu.roll(x, LANES - 32, 1), pltpu.roll(x, 32, 1))
    return x * cos + partner * sin


def _retrot128(x, cos, sin):
    lane = lax.broadcasted_iota(jnp.int32, x.shape, 1)
    even = (lane & 1) == 0
    partner = jnp.where(even, pltpu.roll(x, LANES - 1, 1), pltpu.roll(x, 1, 1))
    return x * cos + partner * sin


def _swa_heads(q_chunks, k_chunks, v_chunks, valid, sink_ref, layer):
    rows = q_chunks[0].shape[0]
    lane = lax.broadcasted_iota(jnp.int32, (rows, LANES), 1)
    low = lane < A_HEAD_DIM
    outs = []
    for pair in range(A_HEADS // 2):
        halves = []
        for hpos in range(2):
            h = 2 * pair + hpos
            kv = h // A_GROUP
            c, kpos = kv // 2, kv % 2
            qc = q_chunks[pair]
            if hpos != kpos:
                qc = pltpu.roll(qc, A_HEAD_DIM, 1)
            keep = low if kpos == 0 else jnp.logical_not(low)
            qm = jnp.where(keep, qc, 0.0).astype(BF16)
            s = lax.dot_general(qm, k_chunks[c], NT_DIMS, preferred_element_type=F32)
            s = jnp.where(valid, s, -jnp.inf)
            sink = sink_ref[layer, h]
            m = jnp.maximum(jnp.max(s, axis=-1, keepdims=True), sink)
            p = jnp.exp(s - m)
            denom = jnp.sum(p, axis=-1, keepdims=True) + jnp.exp(sink - m)
            o = jnp.dot(p.astype(BF16), v_chunks[c], preferred_element_type=F32)
            o = o / denom
            if hpos != kpos:
                o = pltpu.roll(o, A_HEAD_DIM, 1)
            halves.append(o)
        outs.append(jnp.where(low, halves[0], halves[1]))
    return outs


def _swa_prompt_body(sink_ref, q_ref, k_ref, vc_ref, vp_ref, cos_ref, sin_ref,
                     o_ref, kr_ref, kprev_ref, *, layer):
    n = pl.program_id(1)
    cos = cos_ref[...]
    sin = sin_ref[...]

    @pl.when(n == 0)
    def _():
        kprev_ref[...] = jnp.zeros_like(kprev_ref)

    k_chunks, v_chunks = [], []
    k_new = []
    for c in range(2):
        sl = slice(c * LANES, (c + 1) * LANES)
        kc = _rope128(k_ref[:, sl].astype(F32), cos, sin)
        kr_ref[:, sl] = kc
        kc = kc.astype(BF16)
        k_new.append(kc)
        k_chunks.append(jnp.concatenate([kprev_ref[:, sl], kc], axis=0))
        v_chunks.append(jnp.concatenate([vp_ref[:, sl], vc_ref[:, sl]], axis=0))

    scale = A_HEAD_DIM ** -0.5
    q_chunks = [
        _rope128(q_ref[:, c * LANES:(c + 1) * LANES].astype(F32), cos, sin) * scale
        for c in range(A_HEADS // 2)]

    qi = lax.broadcasted_iota(jnp.int32, (WINDOW, 2 * WINDOW), 0) + WINDOW
    ki = lax.broadcasted_iota(jnp.int32, (WINDOW, 2 * WINDOW), 1)
    diff = qi - ki
    valid = (diff >= 0) & (diff < WINDOW) & ((n * WINDOW + ki - WINDOW) >= 0)

    outs = _swa_heads(q_chunks, k_chunks, v_chunks, valid, sink_ref, layer)
    for c, o in enumerate(outs):
        o_ref[:, c * LANES:(c + 1) * LANES] = o.astype(o_ref.dtype)
    for c in range(2):
        kprev_ref[:, c * LANES:(c + 1) * LANES] = k_new[c]


def _swa_prompt(z, sinks, layer, cos, sin, *, batch, seq, name):
    nb = seq // WINDOW
    a_width = A_HEADS * A_HEAD_DIM
    kv_width = A_KV_HEADS * A_HEAD_DIM
    k_blk = a_width // kv_width
    v_blk = k_blk + 1
    rows = batch * seq
    return pl.pallas_call(
        functools.partial(_swa_prompt_body, layer=layer),
        grid=(batch, nb),
        in_specs=[
            pl.BlockSpec(memory_space=pltpu.SMEM),
            pl.BlockSpec((WINDOW, a_width), lambda b, n: (b * nb + n, 0)),
            pl.BlockSpec((WINDOW, kv_width), lambda b, n: (b * nb + n, k_blk)),
            pl.BlockSpec((WINDOW, kv_width), lambda b, n: (b * nb + n, v_blk)),
            pl.BlockSpec((WINDOW, kv_width),
                         lambda b, n: (b * nb + jnp.maximum(n - 1, 0), v_blk)),
            pl.BlockSpec((WINDOW, LANES), lambda b, n: (n, 0)),
            pl.BlockSpec((WINDOW, LANES), lambda b, n: (n, 0)),
        ],
        out_specs=[
            pl.BlockSpec((WINDOW, a_width), lambda b, n: (b * nb + n, 0)),
            pl.BlockSpec((WINDOW, kv_width), lambda b, n: (b * nb + n, 0)),
        ],
        out_shape=[jax.ShapeDtypeStruct((rows, a_width), BF16),
                   jax.ShapeDtypeStruct((rows, kv_width), F32)],
        scratch_shapes=[pltpu.VMEM((WINDOW, kv_width), BF16)],
        compiler_params=_params(("arbitrary", "arbitrary"), 0),
        name=name,
    )(sinks, z, z, z, z, cos, sin)


SWA_SAMPLE_SEQS = 4


def _swa_sample_body(sink_ref, q_ref, k_ref, v_ref, ck_ref, cv_ref, cos_ref,
                     sin_ref, o_ref, kr_ref, *, dec_seq, layer):
    cos = cos_ref[...]
    sin = sin_ref[...]
    g = SWA_SAMPLE_SEQS
    rows = g * dec_seq
    ncache = g * WINDOW
    k_chunks, v_chunks = [], []
    for c in range(2):
        sl = slice(c * LANES, (c + 1) * LANES)
        kc = _rope128(k_ref[:, sl].astype(F32), cos, sin)
        kr_ref[:, sl] = kc
        k_chunks.append(jnp.concatenate(
            [ck_ref[:, sl].astype(BF16), kc.astype(BF16)], axis=0))
        v_chunks.append(jnp.concatenate(
            [cv_ref[:, sl].astype(BF16), v_ref[:, sl]], axis=0))

    scale = A_HEAD_DIM ** -0.5
    q_chunks = [
        _rope128(q_ref[:, c * LANES:(c + 1) * LANES].astype(F32), cos, sin) * scale
        for c in range(A_HEADS // 2)]

    shape = (rows, ncache + rows)
    r = lax.broadcasted_iota(jnp.int32, shape, 0)
    col = lax.broadcasted_iota(jnp.int32, shape, 1)
    log_l = dec_seq.bit_length() - 1
    log_w = WINDOW.bit_length() - 1
    seq_q = r >> log_l
    i = r & (dec_seq - 1)
    in_cache = col < ncache
    cnew = col - ncache
    seq_k = jnp.where(in_cache, col >> log_w, cnew >> log_l)
    ok = ((in_cache & ((col & (WINDOW - 1)) > i))
          | (jnp.logical_not(in_cache) & ((cnew & (dec_seq - 1)) <= i)))
    valid = (seq_k == seq_q) & ok

    outs = _swa_heads(q_chunks, k_chunks, v_chunks, valid, sink_ref, layer)
    for c, o in enumerate(outs):
        o_ref[:, c * LANES:(c + 1) * LANES] = o.astype(o_ref.dtype)


def _swa_sample(z, cache_k, cache_v, sinks, layer, cos, sin, *, row0, dec_batch, dec_seq, name):
    g = SWA_SAMPLE_SEQS
    rows = g * dec_seq
    a_width = A_HEADS * A_HEAD_DIM
    kv_width = A_KV_HEADS * A_HEAD_DIM
    k_blk = a_width // kv_width
    v_blk = k_blk + 1
    r0 = row0 // rows
    depth = cache_k.shape[0]
    ck = cache_k.reshape(depth, dec_batch * WINDOW, kv_width)
    cv = cache_v.reshape(depth, dec_batch * WINDOW, kv_width)
    return pl.pallas_call(
        functools.partial(_swa_sample_body, dec_seq=dec_seq, layer=layer),
        grid=(dec_batch // g,),
        in_specs=[
            pl.BlockSpec(memory_space=pltpu.SMEM),
            pl.BlockSpec((rows, a_width), lambda s: (r0 + s, 0)),
            pl.BlockSpec((rows, kv_width), lambda s: (r0 + s, k_blk)),
            pl.BlockSpec((rows, kv_width), lambda s: (r0 + s, v_blk)),
            pl.BlockSpec((None, g * WINDOW, kv_width), lambda s: (layer, s, 0)),
            pl.BlockSpec((None, g * WINDOW, kv_width), lambda s: (layer, s, 0)),
            pl.BlockSpec((rows, LANES), lambda s: (0, 0)),
            pl.BlockSpec((rows, LANES), lambda s: (0, 0)),
        ],
        out_specs=[
            pl.BlockSpec((rows, a_width), lambda s: (s, 0)),
            pl.BlockSpec((rows, kv_width), lambda s: (s, 0)),
        ],
        out_shape=[jax.ShapeDtypeStruct((dec_batch * dec_seq, a_width), BF16),
                   jax.ShapeDtypeStruct((dec_batch * dec_seq, kv_width), F32)],
        compiler_params=_params(("arbitrary",), 0),
        name=name,
    )(sinks, z, z, z, ck, cv, cos, sin)


RET_HEADS_PER_STEP = 4
RET_BLOCK = RET_HEADS_PER_STEP * R_DIM


def _decay_tables(length, nseq):
    log_g = np.log1p(-np.exp2(-5.0 - np.arange(R_HEADS, dtype=np.float64)))
    i = np.arange(length, dtype=np.float64)
    diff = i[:, None] - i[None, :]
    d = np.where(diff >= 0, np.exp(log_g[:, None, None] * np.maximum(diff, 0.0)), 0.0)
    d = np.einsum('st,hij->hsitj', np.eye(nseq), d).reshape(
        R_HEADS, nseq * length, nseq * length)
    q_dec = np.exp(log_g[:, None] * (i[None, :] + 1.0))
    k_dec = np.exp(log_g[:, None] * (length - 1.0 - i)[None, :])
    shape = (R_HEADS, nseq * length, R_DIM)
    qd = np.broadcast_to(np.tile(q_dec, (1, nseq))[:, :, None], shape)
    kd = np.broadcast_to(np.tile(k_dec, (1, nseq))[:, :, None], shape)
    g_len = np.exp(log_g * length)
    return tuple(jnp.asarray(t, F32) for t in (d, qd, kd, g_len))


def _group_norm_gate(o, gate):
    o = o * lax.rsqrt(jnp.mean(o * o, axis=-1, keepdims=True) + EPS)
    return o * (gate * jax.nn.sigmoid(gate))


def _ret_prompt_body(gl_ref, q_ref, k_ref, v_ref, g_ref, cos_ref, sin_ref,
                     d_ref, qd_ref, kd_ref, o_ref, s_ref):
    hh = pl.program_id(1)
    c = pl.program_id(2)
    cos = cos_ref[...]
    sin = sin_ref[...]

    @pl.when(c == 0)
    def _():
        s_ref[...] = jnp.zeros_like(s_ref)

    scale = R_DIM ** -0.5
    for hl in range(RET_HEADS_PER_STEP):
        sl = slice(hl * R_DIM, (hl + 1) * R_DIM)
        q = _retrot128(q_ref[:, sl].astype(F32), cos, sin)
        k = _retrot128(k_ref[:, sl].astype(F32), cos, sin) * scale
        v = v_ref[:, sl]
        qb = q.astype(BF16)
        state = s_ref[0, hl]
        scores = lax.dot_general(qb, k.astype(BF16), NT_DIMS,
                                 preferred_element_type=F32) * d_ref[hl]
        o = jnp.dot(scores.astype(BF16), v, preferred_element_type=F32)
        cross = jnp.dot(qb, state.astype(BF16), preferred_element_type=F32)
        o = o + cross * qd_ref[hl]
        kd = (k * kd_ref[hl]).astype(BF16)
        s_ref[0, hl] = gl_ref[hh * RET_HEADS_PER_STEP + hl] * state + lax.dot_general(
            kd, v, TN_DIMS, preferred_element_type=F32)
        o_ref[:, sl] = _group_norm_gate(o, g_ref[:, sl].astype(F32)).astype(o_ref.dtype)


def _ret_prompt(z, cos, sin, tabs, *, col0, batch, seq, name):
    d, qd, kd, g_len = tabs
    nc = seq // R_CHUNK
    nh = R_HEADS // RET_HEADS_PER_STEP
    width = R_HEADS * R_DIM
    qb, kb, vb, gb = [(col0 + t * width) // RET_BLOCK for t in range(4)]
    hp = RET_HEADS_PER_STEP

    def zspec(blk):
        return pl.BlockSpec((R_CHUNK, RET_BLOCK),
                            lambda b, h, c: (b * nc + c, blk + h))

    def tspec():
        return pl.BlockSpec((hp, R_CHUNK, R_DIM), lambda b, h, c: (h, 0, 0))

    return pl.pallas_call(
        _ret_prompt_body,
        grid=(batch, nh, nc),
        in_specs=[
            pl.BlockSpec(memory_space=pltpu.SMEM),
            zspec(qb), zspec(kb), zspec(vb), zspec(gb),
            pl.BlockSpec((R_CHUNK, R_DIM), lambda b, h, c: (c, 0)),
            pl.BlockSpec((R_CHUNK, R_DIM), lambda b, h, c: (c, 0)),
            tspec(), tspec(), tspec(),
        ],
        out_specs=[
            pl.BlockSpec((R_CHUNK, RET_BLOCK), lambda b, h, c: (b * nc + c, h)),
            pl.BlockSpec((1, hp, R_DIM, R_DIM), lambda b, h, c: (b, h, 0, 0)),
        ],
        out_shape=[jax.ShapeDtypeStruct((batch * seq, width), BF16),
                   jax.ShapeDtypeStruct((batch, R_HEADS, R_DIM, R_DIM), F32)],
        compiler_params=_params(("arbitrary", "arbitrary", "arbitrary"), 0),
        name=name,
    )(g_len, z, z, z, z, cos, sin, d, qd, kd)


def _ret_sample_body(gl_ref, q_ref, k_ref, v_ref, g_ref, cos_ref, sin_ref,
                     d_ref, qd_ref, kd_ref, s_ref, o_ref, so_ref, *, dec_seq):
    hh = pl.program_id(1)
    cos = cos_ref[...]
    sin = sin_ref[...]
    nseq = R_CHUNK // dec_seq
    log_l = dec_seq.bit_length() - 1
    row_seq = lax.broadcasted_iota(jnp.int32, (R_CHUNK, R_DIM), 0) >> log_l
    scale = R_DIM ** -0.5
    for hl in range(RET_HEADS_PER_STEP):
        sl = slice(hl * R_DIM, (hl + 1) * R_DIM)
        q = _retrot128(q_ref[:, sl].astype(F32), cos, sin)
        k = _retrot128(k_ref[:, sl].astype(F32), cos, sin) * scale
        v = v_ref[:, sl]
        qb = q.astype(BF16)
        scores = lax.dot_general(qb, k.astype(BF16), NT_DIMS,
                                 preferred_element_type=F32) * d_ref[hl]
        o = jnp.dot(scores.astype(BF16), v, preferred_element_type=F32)
        kd = k * kd_ref[hl]
        gl = gl_ref[hh * RET_HEADS_PER_STEP + hl]
        cross = jnp.zeros((R_CHUNK, R_DIM), F32)
        for s in range(nseq):
            mine = row_seq == s
            state = s_ref[s, hl]
            cs = jnp.dot(qb, state.astype(BF16), preferred_element_type=F32)
            cross = jnp.where(mine, cs, cross)
            ks = jnp.where(mine, kd, 0.0).astype(BF16)
            so_ref[s, hl] = gl * state + lax.dot_general(
                ks, v, TN_DIMS, preferred_element_type=F32)
        o = o + cross * qd_ref[hl]
        o_ref[:, sl] = _group_norm_gate(o, g_ref[:, sl].astype(F32)).astype(o_ref.dtype)


def _ret_sample(z, state, layer, cos, sin, tabs, *, row0, col0, dec_batch, dec_seq, name):
    d, qd, kd, g_len = tabs
    nseq = R_CHUNK // dec_seq
    steps = dec_batch // nseq
    nh = R_HEADS // RET_HEADS_PER_STEP
    width = R_HEADS * R_DIM
    qb, kb, vb, gb = [(col0 + t * width) // RET_BLOCK for t in range(4)]
    r0 = row0 // R_CHUNK
    hp = RET_HEADS_PER_STEP

    def zspec(blk):
        return pl.BlockSpec((R_CHUNK, RET_BLOCK), lambda s, h: (r0 + s, blk + h))

    def tspec():
        return pl.BlockSpec((hp, R_CHUNK, R_DIM), lambda s, h: (h, 0, 0))

    sspec_in = pl.BlockSpec((None, nseq, hp, R_DIM, R_DIM), lambda s, h: (layer, s, h, 0, 0))
    sspec = pl.BlockSpec((nseq, hp, R_DIM, R_DIM), lambda s, h: (s, h, 0, 0))
    return pl.pallas_call(
        functools.partial(_ret_sample_body, dec_seq=dec_seq),
        grid=(steps, nh),
        in_specs=[
            pl.BlockSpec(memory_space=pltpu.SMEM),
            zspec(qb), zspec(kb), zspec(vb), zspec(gb),
            pl.BlockSpec((R_CHUNK, R_DIM), lambda s, h: (0, 0)),
            pl.BlockSpec((R_CHUNK, R_DIM), lambda s, h: (0, 0)),
            tspec(), tspec(), tspec(),
            sspec_in,
        ],
        out_specs=[
            pl.BlockSpec((R_CHUNK, RET_BLOCK), lambda s, h: (s, h)),
            sspec,
        ],
        out_shape=[jax.ShapeDtypeStruct((dec_batch * dec_seq, width), BF16),
                   jax.ShapeDtypeStruct(state.shape[1:], F32)],
        compiler_params=_params(("arbitrary", "arbitrary"),
                                4 * nseq * hp * R_DIM * R_DIM * 4 + (16 << 20)),
        name=name,
    )(g_len, z, z, z, z, cos, sin, d, qd, kd, state)


def _gelu_tanh(x):
    c = (2.0 / jnp.pi) ** 0.5
    return 0.5 * x * (1.0 + jnp.tanh(c * (x + 0.044715 * (x * x * x))))


def _ffn_up_body(h_ref, wu_ref, wg_ref, cw_ref, cb_ref, e0_ref, e1_ref,
                 f_ref, tail_ref, us_ref, wubf_ref, wgbf_ref, carry_ref, ug_ref,
                 *, tm, n_tiles, seq, batch, ms, dec_seq):
    i = pl.program_id(1)
    tn = f_ref.shape[1]
    mp = batch * seq

    @pl.when(i == 0)
    def _():
        wubf_ref[...] = wu_ref[...].astype(BF16)
        wgbf_ref[...] = wg_ref[...].astype(BF16)
        carry_ref[...] = jnp.zeros_like(carry_ref)

    last = n_tiles - 1
    s0 = mp - last * tm
    assert 0 <= s0 and s0 + ms == tm and s0 % FIX_ROWS == 0
    starts = [k * seq for k in range(1, batch)]
    for r in starts:
        assert r % tm != 0 and r % FIX_ROWS == 0, "mid-tile, packed-tile aligned starts only"

    def conv_gate(u, gate, p1, p2, w, bias):
        conv = bias + w[0:1, :] * p2
        conv = conv + w[1:2, :] * p1
        conv = conv + w[2:3, :] * u
        return (_gelu_tanh(conv) * gate).astype(f_ref.dtype)

    h = h_ref[...]
    for c in range(tn // MXU_COLS):
        cs = slice(c * MXU_COLS, (c + 1) * MXU_COLS)
        ug_ref[2 * c] = jnp.dot(h, wubf_ref[:, cs], preferred_element_type=F32)
        ug_ref[2 * c + 1] = jnp.dot(h, wgbf_ref[:, cs], preferred_element_type=F32)
        u = ug_ref[2 * c]
        gate = ug_ref[2 * c + 1]
        w = cw_ref[:, cs]
        bias = cb_ref[:, cs]

        row = lax.broadcasted_iota(jnp.int32, u.shape, 0)
        last1 = carry_ref[SUBLANES - 1:SUBLANES, cs]
        last2 = carry_ref[SUBLANES - 2:SUBLANES - 1, cs]
        prev1 = jnp.where(row == 0, last1, pltpu.roll(u, 1, 0))
        prev2 = jnp.where(row == 0, last2, jnp.where(row == 1, last1, pltpu.roll(u, 2, 0)))
        f_ref[:, cs] = conv_gate(u, gate, prev1, prev2, w, bias)
        carry_ref[:, cs] = u[tm - SUBLANES:, :]

    def rewrite(rows, p1_fn, p2_fn):
        for c in range(tn // MXU_COLS):
            cs = slice(c * MXU_COLS, (c + 1) * MXU_COLS)
            us = ug_ref[2 * c, rows, :]
            f_ref[rows, cs] = conv_gate(us, ug_ref[2 * c + 1, rows, :], p1_fn(us, cs),
                                        p2_fn(us, cs), cw_ref[:, cs], cb_ref[:, cs])

    for r in starts:
        @pl.when(i == r // tm)
        def _(off=r % tm):
            pos = lax.broadcasted_iota(jnp.int32, (FIX_ROWS, MXU_COLS), 0)
            rewrite(slice(off, off + FIX_ROWS),
                    lambda us, cs: jnp.where(pos == 0, 0.0, pltpu.roll(us, 1, 0)),
                    lambda us, cs: jnp.where(pos <= 1, 0.0, pltpu.roll(us, 2, 0)))

    @pl.when(i == last)
    def _():
        pos = lax.broadcasted_iota(jnp.int32, (ms, MXU_COLS), 0) & (dec_seq - 1)
        rewrite(slice(s0, tm),
                lambda us, cs: jnp.where(pos == 0, e1_ref[:, cs], pltpu.roll(us, 1, 0)),
                lambda us, cs: jnp.where(
                    pos == 0, e0_ref[:, cs],
                    jnp.where(pos == 1, e1_ref[:, cs], pltpu.roll(us, 2, 0))))
        for c in range(tn // MXU_COLS):
            us_ref[:, c * MXU_COLS:(c + 1) * MXU_COLS] = ug_ref[2 * c, s0:, :]

    for b in range(batch):
        r = (b + 1) * seq - SUBLANES

        @pl.when(i == r // tm)
        def _(b=b, off=r % tm):
            for c in range(tn // MXU_COLS):
                tail_ref[b * SUBLANES:(b + 1) * SUBLANES, c * MXU_COLS:(c + 1) * MXU_COLS] = (
                    ug_ref[2 * c, off:off + SUBLANES, :])


def _ffn_up(h, w_up, conv_w, conv_b, layer, e0, e1, *, tm, tn, seq, batch, dec_seq, name):
    m, k = h.shape
    d_ff = w_up.shape[2] // 2
    ms = e0.shape[0]
    nj = d_ff // tn
    nt = m // tm
    body = functools.partial(_ffn_up_body, tm=tm, n_tiles=nt, seq=seq, batch=batch,
                             ms=ms, dec_seq=dec_seq)
    nbytes = (2 * tm * k * 2 + 4 * k * tn * 4 + 2 * k * tn * 2
              + 16 * tm * tn * 4 + 4 * ms * tn * 4 + (4 << 20))
    return pl.pallas_call(
        body,
        grid=(nj, nt),
        in_specs=[
            pl.BlockSpec((tm, k), lambda j, i: (i, 0)),
            pl.BlockSpec((None, k, tn), lambda j, i: (layer, 0, j)),
            pl.BlockSpec((None, k, tn), lambda j, i: (layer, 0, nj + j)),
            pl.BlockSpec((None, CONV_W, tn), lambda j, i: (layer, 0, j)),
            pl.BlockSpec((None, 1, tn), lambda j, i: (layer, 0, j)),
            pl.BlockSpec((ms, tn), lambda j, i: (0, j)),
            pl.BlockSpec((ms, tn), lambda j, i: (0, j)),
        ],
        out_specs=[
            pl.BlockSpec((tm, tn), lambda j, i: (i, j)),
            pl.BlockSpec((batch * SUBLANES, tn), lambda j, i: (0, j)),
            pl.BlockSpec((ms, tn), lambda j, i: (0, j)),
        ],
        out_shape=[jax.ShapeDtypeStruct((m, d_ff), BF16),
                   jax.ShapeDtypeStruct((batch * SUBLANES, d_ff), F32),
                   jax.ShapeDtypeStruct((ms, d_ff), F32)],
        scratch_shapes=[pltpu.VMEM((k, tn), BF16), pltpu.VMEM((k, tn), BF16),
                        pltpu.VMEM((SUBLANES, tn), F32),
                        pltpu.VMEM((2 * tn // MXU_COLS, tm, MXU_COLS), F32)],
        compiler_params=_params(("arbitrary", "arbitrary"), nbytes),
        name=name,
    )(h, w_up, w_up, conv_w, conv_b, e0, e1)


TM = 768
TM_DOWN = 384
TN_IN = 512
TN_UP = 512
TN_OUT = 512


def kernel(x_prompt, x_sample, cache_win_k, cache_win_v, state_ret, state_conv,
           g_mix, w_in, sinks, w_proj_a, w_proj_b, w_o, g_ffn, w_up, conv_w,
           conv_b, w_down, g_final):
    batch, seq, d_model = x_prompt.shape
    dec_batch, dec_seq, _ = x_sample.shape
    depth = w_in.shape[0]
    d_ff = w_down.shape[1]
    mp = batch * seq
    ms = dec_batch * dec_seq
    a_width = A_HEADS * A_HEAD_DIM
    kv_width = A_KV_HEADS * A_HEAD_DIM
    r_width = R_HEADS * R_DIM
    ret_col0 = a_width + 2 * kv_width
    ga_col = ret_col0 + 4 * r_width
    gb_col = ga_col + d_model
    assert dec_seq & (dec_seq - 1) == 0 and R_CHUNK % dec_seq == 0

    pos_p = jnp.arange(seq, dtype=F32)
    pos_s = PAST_LEN + jnp.arange(dec_seq, dtype=F32)
    rope_p = _rope_tables(pos_p)
    rope_s = tuple(jnp.tile(t, (SWA_SAMPLE_SEQS, 1)) for t in _rope_tables(pos_s))
    rrot_p = _retrot_tables(pos_p)
    nseq = R_CHUNK // dec_seq
    rrot_s = tuple(jnp.tile(t, (nseq, 1)) for t in _retrot_tables(pos_s))
    tabs_p = _decay_tables(R_CHUNK, 1)
    tabs_s = _decay_tables(dec_seq, nseq)
    g_mix3 = g_mix.reshape(depth, 1, d_model)
    g_ffn3 = g_ffn.reshape(depth, 1, d_model)
    conv_b3 = conv_b.reshape(depth, 1, d_ff)

    x = jnp.concatenate([x_prompt.reshape(mp, d_model), x_sample.reshape(ms, d_model)], axis=0)

    kp_l, vp_l, sp_l, cp_l = [], [], [], []
    ks_l, vs_l, ss_l, cs_l = [], [], [], []
    for l in range(depth):
        h = _rmsnorm(x, g_mix3, l, BF16, tm=TM, name=f"norm_mix{l}")
        z = _matmul(h, w_in, l, tm=TM, tn=TN_IN, out_dtype=BF16, name=f"proj_in{l}")

        oa_p, kr_p = _swa_prompt(z, sinks, l, *rope_p, batch=batch, seq=seq,
                                 name=f"swa_prompt{l}")
        oa_s, kr_s = _swa_sample(z, cache_win_k, cache_win_v, sinks, l, *rope_s,
                                 row0=mp, dec_batch=dec_batch, dec_seq=dec_seq,
                                 name=f"swa_sample{l}")
        ob_p, s_p = _ret_prompt(z, *rrot_p, tabs_p, col0=ret_col0, batch=batch,
                                seq=seq, name=f"ret_prompt{l}")
        ob_s, s_s = _ret_sample(z, state_ret, l, *rrot_s, tabs_s, row0=mp,
                                col0=ret_col0, dec_batch=dec_batch, dec_seq=dec_seq,
                                name=f"ret_sample{l}")
        oa = jnp.concatenate([oa_p, oa_s], axis=0)
        ob = jnp.concatenate([ob_p, ob_s], axis=0)
        merged = _merge(oa, ob, z, w_proj_a, w_proj_b, l, ga_col=ga_col,
                        gb_col=gb_col, tm=TM, tn=TN_OUT, name=f"merge{l}")
        x = _matmul(merged, w_o, l, tm=TM, tn=TN_OUT, out_dtype=F32, residual=x,
                    name=f"proj_out{l}")

        h2 = _rmsnorm(x, g_ffn3, l, BF16, tm=TM, name=f"norm_ffn{l}")
        e0 = jnp.repeat(state_conv[l, :, 0, :], dec_seq, axis=0)
        e1 = jnp.repeat(state_conv[l, :, 1, :], dec_seq, axis=0)
        f, u_tail, u_s = _ffn_up(h2, w_up, conv_w, conv_b3, l, e0, e1, tm=TM, tn=TN_UP,
                                 seq=seq, batch=batch, dec_seq=dec_seq, name=f"ffn_up{l}")
        x = _matmul(f, w_down, l, tm=TM_DOWN, tn=TN_OUT, out_dtype=F32, residual=x,
                    name=f"proj_down{l}")

        z_p = z[:mp].reshape(batch, seq, -1)
        z_s = z[mp:].reshape(dec_batch, dec_seq, -1)
        kp_l.append(kr_p.reshape(batch, seq, A_KV_HEADS, A_HEAD_DIM)[:, -WINDOW:])
        vp_l.append(z_p[:, -WINDOW:, a_width + kv_width:ret_col0].astype(F32)
                    .reshape(batch, WINDOW, A_KV_HEADS, A_HEAD_DIM))
        sp_l.append(s_p)
        cp_l.append(u_tail.reshape(batch, SUBLANES, d_ff)[:, -(CONV_W - 1):])
        ks_l.append(kr_s.reshape(dec_batch, dec_seq, A_KV_HEADS, A_HEAD_DIM))
        vs_l.append(z_s[:, :, a_width + kv_width:ret_col0].astype(F32)
                    .reshape(dec_batch, dec_seq, A_KV_HEADS, A_HEAD_DIM))
        ss_l.append(s_s)
        cs_l.append(u_s.reshape(dec_batch, dec_seq, d_ff)[:, -(CONV_W - 1):])

    y = _rmsnorm(x, g_final.reshape(1, 1, d_model), 0, F32, tm=TM, name="norm_final")
    y_prompt = y[:mp].reshape(batch, seq, d_model)
    y_sample = y[mp:].reshape(dec_batch, dec_seq, d_model)
    return (y_prompt, y_sample,
            jnp.stack(kp_l), jnp.stack(vp_l), jnp.stack(sp_l), jnp.stack(cp_l),
            jnp.stack(ks_l), jnp.stack(vs_l), jnp.stack(ss_l), jnp.stack(cs_l))
```

```python
import functools

import jax
import jax.numpy as jnp
import numpy as np
from jax import lax
from jax.experimental import pallas as pl
from jax.experimental.pallas import tpu as pltpu

F32 = jnp.float32
BF16 = jnp.bfloat16

LANES = 128
SUBLANES = 8
MXU_COLS = 256
FIX_ROWS = 2 * SUBLANES
VMEM_LIMIT_CAP = 56 * 1024 * 1024

WINDOW = 128
A_HEADS = 16
A_KV_HEADS = 4
A_HEAD_DIM = 64
A_GROUP = A_HEADS // A_KV_HEADS
R_HEADS = 8
R_DIM = 128
R_CHUNK = 128
CONV_W = 3
EPS = 1e-6
ROPE_THETA = 10000.0
PAST_LEN = 16384

NT_DIMS = (((1,), (1,)), ((), ()))
TN_DIMS = (((0,), (0,)), ((), ()))


def _vmem_limit(nbytes):
    return int(min(VMEM_LIMIT_CAP, max(32 * 1024 * 1024, nbytes)))


def _params(semantics, nbytes):
    return pltpu.CompilerParams(dimension_semantics=semantics,
                                vmem_limit_bytes=_vmem_limit(nbytes))


def _rmsnorm_body(x_ref, g_ref, o_ref):
    x = x_ref[...]
    ms = jnp.mean(x * x, axis=-1, keepdims=True)
    o_ref[...] = (x * lax.rsqrt(ms + EPS) * g_ref[...]).astype(o_ref.dtype)


def _rmsnorm(x, g, layer, out_dtype, *, tm, name):
    m, d = x.shape
    return pl.pallas_call(
        _rmsnorm_body,
        grid=(m // tm,),
        in_specs=[pl.BlockSpec((tm, d), lambda i: (i, 0)),
                  pl.BlockSpec((None, 1, d), lambda i: (layer, 0, 0))],
        out_specs=pl.BlockSpec((tm, d), lambda i: (i, 0)),
        out_shape=jax.ShapeDtypeStruct((m, d), out_dtype),
        compiler_params=_params(("arbitrary",), 6 * tm * d * 4),
        name=name,
    )(x, g)


def _join_norm_body(xp_ref, xs_ref, g_ref, x_ref, h_ref, *, prompt_tiles):
    i = pl.program_id(0)

    def emit(x):
        ms = jnp.mean(x * x, axis=-1, keepdims=True)
        x_ref[...] = x
        h_ref[...] = (x * lax.rsqrt(ms + EPS) * g_ref[...]).astype(h_ref.dtype)

    @pl.when(i < prompt_tiles)
    def _():
        emit(xp_ref[...])

    @pl.when(i >= prompt_tiles)
    def _():
        emit(xs_ref[...])


def _join_norm(xp, xs, g, layer, *, name):
    mp, d = xp.shape
    ms = xs.shape[0]
    assert mp % ms == 0
    pt = mp // ms
    return pl.pallas_call(
        functools.partial(_join_norm_body, prompt_tiles=pt),
        grid=(pt + 1,),
        in_specs=[pl.BlockSpec((ms, d), lambda i: (jnp.minimum(i, pt - 1), 0)),
                  pl.BlockSpec((ms, d), lambda i: (0, 0)),
                  pl.BlockSpec((None, 1, d), lambda i: (layer, 0, 0))],
        out_specs=[pl.BlockSpec((ms, d), lambda i: (i, 0)),
                   pl.BlockSpec((ms, d), lambda i: (i, 0))],
        out_shape=[jax.ShapeDtypeStruct((mp + ms, d), F32),
                   jax.ShapeDtypeStruct((mp + ms, d), BF16)],
        compiler_params=_params(("arbitrary",), 0),
        name=name,
    )(xp, xs, g)


def _split_norm_body(x_ref, g_ref, yp_ref, ys_ref, *, prompt_tiles):
    i = pl.program_id(0)
    x = x_ref[...]
    ms = jnp.mean(x * x, axis=-1, keepdims=True)
    y = x * lax.rsqrt(ms + EPS) * g_ref[...]

    @pl.when(i < prompt_tiles)
    def _():
        yp_ref[...] = y

    @pl.when(i >= prompt_tiles)
    def _():
        ys_ref[...] = y


def _split_norm(x, g, *, mp, name):
    m, d = x.shape
    ms = m - mp
    assert mp % ms == 0
    pt = mp // ms
    return pl.pallas_call(
        functools.partial(_split_norm_body, prompt_tiles=pt),
        grid=(pt + 1,),
        in_specs=[pl.BlockSpec((ms, d), lambda i: (i, 0)),
                  pl.BlockSpec((1, d), lambda i: (0, 0))],
        out_specs=[pl.BlockSpec((ms, d), lambda i: (jnp.minimum(i, pt - 1), 0)),
                   pl.BlockSpec((ms, d), lambda i: (0, 0))],
        out_shape=[jax.ShapeDtypeStruct((mp, d), F32),
                   jax.ShapeDtypeStruct((ms, d), F32)],
        compiler_params=_params(("arbitrary",), 0),
        name=name,
    )(x, g.reshape(1, d))


def _mm_body(x_ref, w_ref, o_ref, wbf_ref):
    @pl.when(pl.program_id(1) == 0)
    def _():
        wbf_ref[...] = w_ref[...].astype(BF16)

    o_ref[...] = jnp.dot(x_ref[...], wbf_ref[...],
                         preferred_element_type=F32).astype(o_ref.dtype)


def _mm_res_body(x_ref, w_ref, r_ref, o_ref, wbf_ref):
    @pl.when(pl.program_id(1) == 0)
    def _():
        wbf_ref[...] = w_ref[...].astype(BF16)

    o_ref[...] = r_ref[...] + jnp.dot(x_ref[...], wbf_ref[...],
                                      preferred_element_type=F32)


def _matmul(x, w, layer, *, tm, tn, out_dtype, residual=None, name):
    m, k = x.shape
    n = w.shape[2]
    grid = (n // tn, m // tm)
    in_specs = [pl.BlockSpec((tm, k), lambda j, i: (i, 0)),
                pl.BlockSpec((None, k, tn), lambda j, i: (layer, 0, j))]
    args = [x, w]
    body = _mm_body
    if residual is not None:
        in_specs.append(pl.BlockSpec((tm, tn), lambda j, i: (i, j)))
        args.append(residual)
        body = _mm_res_body
    nbytes = (2 * tm * k * 2 + 2 * k * tn * 4 + k * tn * 2
              + 6 * tm * tn * 4 + (4 << 20))
    return pl.pallas_call(
        body,
        grid=grid,
        in_specs=in_specs,
        out_specs=pl.BlockSpec((tm, tn), lambda j, i: (i, j)),
        out_shape=jax.ShapeDtypeStruct((m, n), out_dtype),
        scratch_shapes=[pltpu.VMEM((k, tn), BF16)],
        compiler_params=_params(("arbitrary", "arbitrary"), nbytes),
        name=name,
    )(*args)


def _merge_body(oa_ref, ob_ref, ga_ref, gb_ref, wa_ref, wb_ref, o_ref,
                wabf_ref, wbbf_ref):
    @pl.when(pl.program_id(1) == 0)
    def _():
        wabf_ref[...] = wa_ref[...].astype(BF16)
        wbbf_ref[...] = wb_ref[...].astype(BF16)

    a = jnp.dot(oa_ref[...], wabf_ref[...], preferred_element_type=F32)
    b = jnp.dot(ob_ref[...], wbbf_ref[...], preferred_element_type=F32)
    ga = jax.nn.sigmoid(ga_ref[...].astype(F32))
    gb = jax.nn.sigmoid(gb_ref[...].astype(F32))
    o_ref[...] = (ga * a + gb * b).astype(o_ref.dtype)


def _merge(oa, ob, z, wa, wb, layer, *, ga_col, gb_col, tm, tn, name):
    m, k = oa.shape
    n = wa.shape[2]
    ga_blk = ga_col // tn
    gb_blk = gb_col // tn
    nbytes = (4 * tm * k * 2 + 4 * k * tn * 4 + 2 * k * tn * 2
              + 4 * tm * tn * 2 + 8 * tm * tn * 4 + (4 << 20))
    return pl.pallas_call(
        _merge_body,
        grid=(n // tn, m // tm),
        in_specs=[pl.BlockSpec((tm, k), lambda j, i: (i, 0)),
                  pl.BlockSpec((tm, k), lambda j, i: (i, 0)),
                  pl.BlockSpec((tm, tn), lambda j, i: (i, ga_blk + j)),
                  pl.BlockSpec((tm, tn), lambda j, i: (i, gb_blk + j)),
                  pl.BlockSpec((None, k, tn), lambda j, i: (layer, 0, j)),
                  pl.BlockSpec((None, k, tn), lambda j, i: (layer, 0, j))],
        out_specs=pl.BlockSpec((tm, tn), lambda j, i: (i, j)),
        out_shape=jax.ShapeDtypeStruct((m, n), BF16),
        scratch_shapes=[pltpu.VMEM((k, tn), BF16), pltpu.VMEM((k, tn), BF16)],
        compiler_params=_params(("arbitrary", "arbitrary"), nbytes),
        name=name,
    )(oa, ob, z, z, wa, wb)


def _rope_tables(pos):
    half = A_HEAD_DIM // 2
    inv = ROPE_THETA ** (-2.0 * jnp.arange(half, dtype=F32) / A_HEAD_DIM)
    ang = pos[:, None] * inv[None, :]
    c, s = jnp.cos(ang), jnp.sin(ang)
    cos = jnp.concatenate([c, c, c, c], axis=-1)
    sin = jnp.concatenate([-s, s, -s, s], axis=-1)
    return cos, sin


def _retrot_tables(pos):
    half = R_DIM // 2
    inv = 1.0 / (10000.0 ** jnp.linspace(0.0, 1.0, half, dtype=F32))
    ang = pos[:, None] * inv[None, :]
    c, s = jnp.cos(ang), jnp.sin(ang)
    cos = jnp.stack([c, c], axis=-1).reshape(pos.shape[0], R_DIM)
    sin = jnp.stack([-s, s], axis=-1).reshape(pos.shape[0], R_DIM)
    return cos, sin


def _rope128(x, cos, sin):
    lane = lax.broadcasted_iota(jnp.int32, x.shape, 1)
    first = (lane & (A_HEAD_DIM - 1)) < (A_HEAD_DIM // 2)
    partner = jnp.where(first, pltpu.roll(x, LANES - 32, 1), pltpu.roll(x, 32, 1))
    return x * cos + partner * sin


def _retrot128(x, cos, sin):
    lane = lax.broadcasted_iota(jnp.int32, x.shape, 1)
    even = (lane & 1) == 0
    partner = jnp.where(even, pltpu.roll(x, LANES - 1, 1), pltpu.roll(x, 1, 1))
    return x * cos + partner * sin


def _swa_heads(q_chunks, k_chunks, v_chunks, valid, sink_ref, layer):
    rows = q_chunks[0].shape[0]
    lane = lax.broadcasted_iota(jnp.int32, (rows, LANES), 1)
    low = lane < A_HEAD_DIM
    outs = []
    for pair in range(A_HEADS // 2):
        halves = []
        for hpos in range(2):
            h = 2 * pair + hpos
            kv = h // A_GROUP
            c, kpos = kv // 2, kv % 2
            qc = q_chunks[pair]
            if hpos != kpos:
                qc = pltpu.roll(qc, A_HEAD_DIM, 1)
            keep = low if kpos == 0 else jnp.logical_not(low)
            qm = jnp.where(keep, qc, 0.0).astype(BF16)
            s = lax.dot_general(qm, k_chunks[c], NT_DIMS, preferred_element_type=F32)
            s = jnp.where(valid, s, -jnp.inf)
            sink = sink_ref[layer, h]
            m = jnp.maximum(jnp.max(s, axis=-1, keepdims=True), sink)
            p = jnp.exp(s - m)
            denom = jnp.sum(p, axis=-1, keepdims=True) + jnp.exp(sink - m)
            o = jnp.dot(p.astype(BF16), v_chunks[c], preferred_element_type=F32)
            o = o / denom
            if hpos != kpos:
                o = pltpu.roll(o, A_HEAD_DIM, 1)
            halves.append(o)
        outs.append(jnp.where(low, halves[0], halves[1]))
    return outs


def _swa_prompt_body(sink_ref, q_ref, k_ref, vc_ref, vp_ref, cos_ref, sin_ref,
                     o_ref, kr_ref, kprev_ref, *, layer):
    n = pl.program_id(1)
    cos = cos_ref[...]
    sin = sin_ref[...]

    @pl.when(n == 0)
    def _():
        kprev_ref[...] = jnp.zeros_like(kprev_ref)

    k_chunks, v_chunks = [], []
    k_new = []
    for c in range(2):
        sl = slice(c * LANES, (c + 1) * LANES)
        kc = _rope128(k_ref[:, sl].astype(F32), cos, sin)
        kr_ref[:, sl] = kc
        kc = kc.astype(BF16)
        k_new.append(kc)
        k_chunks.append(jnp.concatenate([kprev_ref[:, sl], kc], axis=0))
        v_chunks.append(jnp.concatenate([vp_ref[:, sl], vc_ref[:, sl]], axis=0))

    scale = A_HEAD_DIM ** -0.5
    q_chunks = [
        _rope128(q_ref[:, c * LANES:(c + 1) * LANES].astype(F32), cos, sin) * scale
        for c in range(A_HEADS // 2)]

    qi = lax.broadcasted_iota(jnp.int32, (WINDOW, 2 * WINDOW), 0) + WINDOW
    ki = lax.broadcasted_iota(jnp.int32, (WINDOW, 2 * WINDOW), 1)
    diff = qi - ki
    valid = (diff >= 0) & (diff < WINDOW) & ((n * WINDOW + ki - WINDOW) >= 0)

    outs = _swa_heads(q_chunks, k_chunks, v_chunks, valid, sink_ref, layer)
    for c, o in enumerate(outs):
        o_ref[:, c * LANES:(c + 1) * LANES] = o.astype(o_ref.dtype)
    for c in range(2):
        kprev_ref[:, c * LANES:(c + 1) * LANES] = k_new[c]


def _swa_prompt(z, sinks, layer, cos, sin, *, batch, seq, name):
    nb = seq // WINDOW
    a_width = A_HEADS * A_HEAD_DIM
    kv_width = A_KV_HEADS * A_HEAD_DIM
    k_blk = a_width // kv_width
    v_blk = k_blk + 1
    rows = batch * seq
    return pl.pallas_call(
        functools.partial(_swa_prompt_body, layer=layer),
        grid=(batch, nb),
        in_specs=[
            pl.BlockSpec(memory_space=pltpu.SMEM),
            pl.BlockSpec((WINDOW, a_width), lambda b, n: (b * nb + n, 0)),
            pl.BlockSpec((WINDOW, kv_width), lambda b, n: (b * nb + n, k_blk)),
            pl.BlockSpec((WINDOW, kv_width), lambda b, n: (b * nb + n, v_blk)),
            pl.BlockSpec((WINDOW, kv_width),
                         lambda b, n: (b * nb + jnp.maximum(n - 1, 0), v_blk)),
            pl.BlockSpec((WINDOW, LANES), lambda b, n: (n, 0)),
            pl.BlockSpec((WINDOW, LANES), lambda b, n: (n, 0)),
        ],
        out_specs=[
            pl.BlockSpec((WINDOW, a_width), lambda b, n: (b * nb + n, 0)),
            pl.BlockSpec((WINDOW, kv_width), lambda b, n: (b, 0)),
        ],
        out_shape=[jax.ShapeDtypeStruct((rows, a_width), BF16),
                   jax.ShapeDtypeStruct((batch * WINDOW, kv_width), F32)],
        scratch_shapes=[pltpu.VMEM((WINDOW, kv_width), BF16)],
        compiler_params=_params(("arbitrary", "arbitrary"), 0),
        name=name,
    )(sinks, z, z, z, z, cos, sin)


SWA_SAMPLE_SEQS = 4


def _swa_sample_body(sink_ref, q_ref, k_ref, v_ref, ck_ref, cv_ref, cos_ref,
                     sin_ref, o_ref, kr_ref, *, dec_seq, layer):
    cos = cos_ref[...]
    sin = sin_ref[...]
    g = SWA_SAMPLE_SEQS
    rows = g * dec_seq
    ncache = g * WINDOW
    k_chunks, v_chunks = [], []
    for c in range(2):
        sl = slice(c * LANES, (c + 1) * LANES)
        kc = _rope128(k_ref[:, sl].astype(F32), cos, sin)
        kr_ref[:, sl] = kc
        k_chunks.append(jnp.concatenate(
            [ck_ref[:, sl].astype(BF16), kc.astype(BF16)], axis=0))
        v_chunks.append(jnp.concatenate(
            [cv_ref[:, sl].astype(BF16), v_ref[:, sl]], axis=0))

    scale = A_HEAD_DIM ** -0.5
    q_chunks = [
        _rope128(q_ref[:, c * LANES:(c + 1) * LANES].astype(F32), cos, sin) * scale
        for c in range(A_HEADS // 2)]

    shape = (rows, ncache + rows)
    r = lax.broadcasted_iota(jnp.int32, shape, 0)
    col = lax.broadcasted_iota(jnp.int32, shape, 1)
    log_l = dec_seq.bit_length() - 1
    log_w = WINDOW.bit_length() - 1
    seq_q = r >> log_l
    i = r & (dec_seq - 1)
    in_cache = col < ncache
    cnew = col - ncache
    seq_k = jnp.where(in_cache, col >> log_w, cnew >> log_l)
    ok = ((in_cache & ((col & (WINDOW - 1)) > i))
          | (jnp.logical_not(in_cache) & ((cnew & (dec_seq - 1)) <= i)))
    valid = (seq_k == seq_q) & ok

    outs = _swa_heads(q_chunks, k_chunks, v_chunks, valid, sink_ref, layer)
    for c, o in enumerate(outs):
        o_ref[:, c * LANES:(c + 1) * LANES] = o.astype(o_ref.dtype)


def _swa_sample(z, cache_k, cache_v, sinks, layer, cos, sin, *, row0, dec_batch, dec_seq, name):
    g = SWA_SAMPLE_SEQS
    rows = g * dec_seq
    a_width = A_HEADS * A_HEAD_DIM
    kv_width = A_KV_HEADS * A_HEAD_DIM
    k_blk = a_width // kv_width
    v_blk = k_blk + 1
    r0 = row0 // rows
    depth = cache_k.shape[0]
    ck = cache_k.reshape(depth, dec_batch * WINDOW, kv_width)
    cv = cache_v.reshape(depth, dec_batch * WINDOW, kv_width)
    return pl.pallas_call(
        functools.partial(_swa_sample_body, dec_seq=dec_seq, layer=layer),
        grid=(dec_batch // g,),
        in_specs=[
            pl.BlockSpec(memory_space=pltpu.SMEM),
            pl.BlockSpec((rows, a_width), lambda s: (r0 + s, 0)),
            pl.BlockSpec((rows, kv_width), lambda s: (r0 + s, k_blk)),
            pl.BlockSpec((rows, kv_width), lambda s: (r0 + s, v_blk)),
            pl.BlockSpec((None, g * WINDOW, kv_width), lambda s: (layer, s, 0)),
            pl.BlockSpec((None, g * WINDOW, kv_width), lambda s: (layer, s, 0)),
            pl.BlockSpec((rows, LANES), lambda s: (0, 0)),
            pl.BlockSpec((rows, LANES), lambda s: (0, 0)),
        ],
        out_specs=[
            pl.BlockSpec((rows, a_width), lambda s: (s, 0)),
            pl.BlockSpec((rows, kv_width), lambda s: (s, 0)),
        ],
        out_shape=[jax.ShapeDtypeStruct((dec_batch * dec_seq, a_width), BF16),
                   jax.ShapeDtypeStruct((dec_batch * dec_seq, kv_width), F32)],
        compiler_params=_params(("arbitrary",), 0),
        name=name,
    )(sinks, z, z, z, ck, cv, cos, sin)


RET_HEADS_PER_STEP = 4
RET_BLOCK = RET_HEADS_PER_STEP * R_DIM


def _decay_tables(length, nseq):
    log_g = np.log1p(-np.exp2(-5.0 - np.arange(R_HEADS, dtype=np.float64)))
    i = np.arange(length, dtype=np.float64)
    diff = i[:, None] - i[None, :]
    d = np.where(diff >= 0, np.exp(log_g[:, None, None] * np.maximum(diff, 0.0)), 0.0)
    d = np.einsum('st,hij->hsitj', np.eye(nseq), d).reshape(
        R_HEADS, nseq * length, nseq * length)
    q_dec = np.exp(log_g[:, None] * (i[None, :] + 1.0))
    k_dec = np.exp(log_g[:, None] * (length - 1.0 - i)[None, :])
    shape = (R_HEADS, nseq * length, R_DIM)
    qd = np.broadcast_to(np.tile(q_dec, (1, nseq))[:, :, None], shape)
    kd = np.broadcast_to(np.tile(k_dec, (1, nseq))[:, :, None], shape)
    g_len = np.exp(log_g * length)
    return tuple(jnp.asarray(t, F32) for t in (d, qd, kd, g_len))


def _group_norm_gate(o, gate):
    o = o * lax.rsqrt(jnp.mean(o * o, axis=-1, keepdims=True) + EPS)
    return o * (gate * jax.nn.sigmoid(gate))


def _ret_prompt_body(gl_ref, q_ref, k_ref, v_ref, g_ref, cos_ref, sin_ref,
                     d_ref, qd_ref, kd_ref, o_ref, s_ref):
    hh = pl.program_id(1)
    c = pl.program_id(2)
    cos = cos_ref[...]
    sin = sin_ref[...]

    @pl.when(c == 0)
    def _():
        s_ref[...] = jnp.zeros_like(s_ref)

    scale = R_DIM ** -0.5
    for hl in range(RET_HEADS_PER_STEP):
        sl = slice(hl * R_DIM, (hl + 1) * R_DIM)
        q = _retrot128(q_ref[:, sl].astype(F32), cos, sin)
        k = _retrot128(k_ref[:, sl].astype(F32), cos, sin) * scale
        v = v_ref[:, sl]
        qb = q.astype(BF16)
        state = s_ref[0, hl]
        scores = lax.dot_general(qb, k.astype(BF16), NT_DIMS,
                                 preferred_element_type=F32) * d_ref[hl]
        o = jnp.dot(scores.astype(BF16), v, preferred_element_type=F32)
        cross = jnp.dot(qb, state.astype(BF16), preferred_element_type=F32)
        o = o + cross * qd_ref[hl]
        kd = (k * kd_ref[hl]).astype(BF16)
        s_ref[0, hl] = gl_ref[hh * RET_HEADS_PER_STEP + hl] * state + lax.dot_general(
            kd, v, TN_DIMS, preferred_element_type=F32)
        o_ref[:, sl] = _group_norm_gate(o, g_ref[:, sl].astype(F32)).astype(o_ref.dtype)


def _ret_prompt(z, cos, sin, tabs, *, col0, batch, seq, name):
    d, qd, kd, g_len = tabs
    nc = seq // R_CHUNK
    nh = R_HEADS // RET_HEADS_PER_STEP
    width = R_HEADS * R_DIM
    qb, kb, vb, gb = [(col0 + t * width) // RET_BLOCK for t in range(4)]
    hp = RET_HEADS_PER_STEP

    def zspec(blk):
        return pl.BlockSpec((R_CHUNK, RET_BLOCK),
                            lambda b, h, c: (b * nc + c, blk + h))

    def tspec():
        return pl.BlockSpec((hp, R_CHUNK, R_DIM), lambda b, h, c: (h, 0, 0))

    return pl.pallas_call(
        _ret_prompt_body,
        grid=(batch, nh, nc),
        in_specs=[
            pl.BlockSpec(memory_space=pltpu.SMEM),
            zspec(qb), zspec(kb), zspec(vb), zspec(gb),
            pl.BlockSpec((R_CHUNK, R_DIM), lambda b, h, c: (c, 0)),
            pl.BlockSpec((R_CHUNK, R_DIM), lambda b, h, c: (c, 0)),
            tspec(), tspec(), tspec(),
        ],
        out_specs=[
            pl.BlockSpec((R_CHUNK, RET_BLOCK), lambda b, h, c: (b * nc + c, h)),
            pl.BlockSpec((1, hp, R_DIM, R_DIM), lambda b, h, c: (b, h, 0, 0)),
        ],
        out_shape=[jax.ShapeDtypeStruct((batch * seq, width), BF16),
                   jax.ShapeDtypeStruct((batch, R_HEADS, R_DIM, R_DIM), F32)],
        compiler_params=_params(("arbitrary", "arbitrary", "arbitrary"), 0),
        name=name,
    )(g_len, z, z, z, z, cos, sin, d, qd, kd)


def _ret_sample_body(gl_ref, q_ref, k_ref, v_ref, g_ref, cos_ref, sin_ref,
                     d_ref, qd_ref, kd_ref, s_ref, o_ref, so_ref, *, dec_seq):
    hh = pl.program_id(1)
    cos = cos_ref[...]
    sin = sin_ref[...]
    nseq = R_CHUNK // dec_seq
    log_l = dec_seq.bit_length() - 1
    row_seq = lax.broadcasted_iota(jnp.int32, (R_CHUNK, R_DIM), 0) >> log_l
    scale = R_DIM ** -0.5
    for hl in range(RET_HEADS_PER_STEP):
        sl = slice(hl * R_DIM, (hl + 1) * R_DIM)
        q = _retrot128(q_ref[:, sl].astype(F32), cos, sin)
        k = _retrot128(k_ref[:, sl].astype(F32), cos, sin) * scale
        v = v_ref[:, sl]
        qb = q.astype(BF16)
        scores = lax.dot_general(qb, k.astype(BF16), NT_DIMS,
                                 preferred_element_type=F32) * d_ref[hl]
        o = jnp.dot(scores.astype(BF16), v, preferred_element_type=F32)
        kd = k * kd_ref[hl]
        gl = gl_ref[hh * RET_HEADS_PER_STEP + hl]
        cross = jnp.zeros((R_CHUNK, R_DIM), F32)
        for s in range(nseq):
            mine = row_seq == s
            state = s_ref[s, hl]
            cs = jnp.dot(qb, state.astype(BF16), preferred_element_type=F32)
            cross = jnp.where(mine, cs, cross)
            ks = jnp.where(mine, kd, 0.0).astype(BF16)
            so_ref[s, hl] = gl * state + lax.dot_general(
                ks, v, TN_DIMS, preferred_element_type=F32)
        o = o + cross * qd_ref[hl]
        o_ref[:, sl] = _group_norm_gate(o, g_ref[:, sl].astype(F32)).astype(o_ref.dtype)


def _ret_sample(z, state, layer, cos, sin, tabs, *, row0, col0, dec_batch, dec_seq, name):
    d, qd, kd, g_len = tabs
    nseq = R_CHUNK // dec_seq
    steps = dec_batch // nseq
    nh = R_HEADS // RET_HEADS_PER_STEP
    width = R_HEADS * R_DIM
    qb, kb, vb, gb = [(col0 + t * width) // RET_BLOCK for t in range(4)]
    r0 = row0 // R_CHUNK
    hp = RET_HEADS_PER_STEP

    def zspec(blk):
        return pl.BlockSpec((R_CHUNK, RET_BLOCK), lambda s, h: (r0 + s, blk + h))

    def tspec():
        return pl.BlockSpec((hp, R_CHUNK, R_DIM), lambda s, h: (h, 0, 0))

    sspec_in = pl.BlockSpec((None, nseq, hp, R_DIM, R_DIM), lambda s, h: (layer, s, h, 0, 0))
    sspec = pl.BlockSpec((nseq, hp, R_DIM, R_DIM), lambda s, h: (s, h, 0, 0))
    return pl.pallas_call(
        functools.partial(_ret_sample_body, dec_seq=dec_seq),
        grid=(steps, nh),
        in_specs=[
            pl.BlockSpec(memory_space=pltpu.SMEM),
            zspec(qb), zspec(kb), zspec(vb), zspec(gb),
            pl.BlockSpec((R_CHUNK, R_DIM), lambda s, h: (0, 0)),
            pl.BlockSpec((R_CHUNK, R_DIM), lambda s, h: (0, 0)),
            tspec(), tspec(), tspec(),
            sspec_in,
        ],
        out_specs=[
            pl.BlockSpec((R_CHUNK, RET_BLOCK), lambda s, h: (s, h)),
            sspec,
        ],
        out_shape=[jax.ShapeDtypeStruct((dec_batch * dec_seq, width), BF16),
                   jax.ShapeDtypeStruct(state.shape[1:], F32)],
        compiler_params=_params(("arbitrary", "arbitrary"),
                                4 * nseq * hp * R_DIM * R_DIM * 4 + (16 << 20)),
        name=name,
    )(g_len, z, z, z, z, cos, sin, d, qd, kd, state)


def _gelu_tanh(x):
    c = (2.0 / jnp.pi) ** 0.5
    return 0.5 * x * (1.0 + jnp.tanh(c * (x + 0.044715 * (x * x * x))))


def _ffn_up_body(h_ref, wu_ref, wg_ref, cw_ref, cb_ref, e0_ref, e1_ref,
                 f_ref, tail_ref, us_ref, wubf_ref, wgbf_ref, carry_ref, ug_ref,
                 *, tm, n_tiles, seq, batch, ms, dec_seq):
    i = pl.program_id(1)
    tn = f_ref.shape[1]
    mp = batch * seq

    @pl.when(i == 0)
    def _():
        wubf_ref[...] = wu_ref[...].astype(BF16)
        wgbf_ref[...] = wg_ref[...].astype(BF16)
        carry_ref[...] = jnp.zeros_like(carry_ref)

    last = n_tiles - 1
    s0 = mp - last * tm
    assert 0 <= s0 and s0 + ms == tm and s0 % FIX_ROWS == 0
    starts = [k * seq for k in range(1, batch)]
    for r in starts:
        assert r % tm != 0 and r % FIX_ROWS == 0, "mid-tile, packed-tile aligned starts only"

    def conv_gate(u, gate, p1, p2, w, bias):
        conv = bias + w[0:1, :] * p2
        conv = conv + w[1:2, :] * p1
        conv = conv + w[2:3, :] * u
        return (_gelu_tanh(conv) * gate).astype(f_ref.dtype)

    h = h_ref[...]
    for c in range(tn // MXU_COLS):
        cs = slice(c * MXU_COLS, (c + 1) * MXU_COLS)
        ug_ref[2 * c] = jnp.dot(h, wubf_ref[:, cs], preferred_element_type=F32)
        ug_ref[2 * c + 1] = jnp.dot(h, wgbf_ref[:, cs], preferred_element_type=F32)
        u = ug_ref[2 * c]
        gate = ug_ref[2 * c + 1]
        w = cw_ref[:, cs]
        bias = cb_ref[:, cs]

        row = lax.broadcasted_iota(jnp.int32, u.shape, 0)
        last1 = carry_ref[SUBLANES - 1:SUBLANES, cs]
        last2 = carry_ref[SUBLANES - 2:SUBLANES - 1, cs]
        prev1 = jnp.where(row == 0, last1, pltpu.roll(u, 1, 0))
        prev2 = jnp.where(row == 0, last2, jnp.where(row == 1, last1, pltpu.roll(u, 2, 0)))
        f_ref[:, cs] = conv_gate(u, gate, prev1, prev2, w, bias)
        carry_ref[:, cs] = u[tm - SUBLANES:, :]

    def rewrite(rows, p1_fn, p2_fn):
        for c in range(tn // MXU_COLS):
            cs = slice(c * MXU_COLS, (c + 1) * MXU_COLS)
            us = ug_ref[2 * c, rows, :]
            f_ref[rows, cs] = conv_gate(us, ug_ref[2 * c + 1, rows, :], p1_fn(us, cs),
                                        p2_fn(us, cs), cw_ref[:, cs], cb_ref[:, cs])

    for r in starts:
        @pl.when(i == r // tm)
        def _(off=r % tm):
            pos = lax.broadcasted_iota(jnp.int32, (FIX_ROWS, MXU_COLS), 0)
            rewrite(slice(off, off + FIX_ROWS),
                    lambda us, cs: jnp.where(pos == 0, 0.0, pltpu.roll(us, 1, 0)),
                    lambda us, cs: jnp.where(pos <= 1, 0.0, pltpu.roll(us, 2, 0)))

    @pl.when(i == last)
    def _():
        pos = lax.broadcasted_iota(jnp.int32, (ms, MXU_COLS), 0) & (dec_seq - 1)
        rewrite(slice(s0, tm),
                lambda us, cs: jnp.where(pos == 0, e1_ref[:, cs], pltpu.roll(us, 1, 0)),
                lambda us, cs: jnp.where(
                    pos == 0, e0_ref[:, cs],
                    jnp.where(pos == 1, e1_ref[:, cs], pltpu.roll(us, 2, 0))))
        for c in range(tn // MXU_COLS):
            us_ref[:, c * MXU_COLS:(c + 1) * MXU_COLS] = ug_ref[2 * c, s0:, :]

    for b in range(batch):
        r = (b + 1) * seq - SUBLANES

        @pl.when(i == r // tm)
        def _(b=b, off=r % tm):
            for c in range(tn // MXU_COLS):
                tail_ref[b * SUBLANES:(b + 1) * SUBLANES, c * MXU_COLS:(c + 1) * MXU_COLS] = (
                    ug_ref[2 * c, off:off + SUBLANES, :])


def _ffn_up(h, w_up, conv_w, conv_b, layer, e, *, tm, tn, seq, batch, dec_seq, name):
    m, k = h.shape
    d_ff = w_up.shape[2] // 2
    ms = e.shape[2]
    nj = d_ff // tn
    nt = m // tm
    body = functools.partial(_ffn_up_body, tm=tm, n_tiles=nt, seq=seq, batch=batch,
                             ms=ms, dec_seq=dec_seq)
    nbytes = (2 * tm * k * 2 + 4 * k * tn * 4 + 2 * k * tn * 2
              + 16 * tm * tn * 4 + 4 * ms * tn * 4 + (4 << 20))
    return pl.pallas_call(
        body,
        grid=(nj, nt),
        in_specs=[
            pl.BlockSpec((tm, k), lambda j, i: (i, 0)),
            pl.BlockSpec((None, k, tn), lambda j, i: (layer, 0, j)),
            pl.BlockSpec((None, k, tn), lambda j, i: (layer, 0, nj + j)),
            pl.BlockSpec((None, CONV_W, tn), lambda j, i: (layer, 0, j)),
            pl.BlockSpec((None, 1, tn), lambda j, i: (layer, 0, j)),
            pl.BlockSpec((None, None, ms, tn), lambda j, i: (layer, 0, 0, j)),
            pl.BlockSpec((None, None, ms, tn), lambda j, i: (layer, 1, 0, j)),
        ],
        out_specs=[
            pl.BlockSpec((tm, tn), lambda j, i: (i, j)),
            pl.BlockSpec((batch * SUBLANES, tn), lambda j, i: (0, j)),
            pl.BlockSpec((ms, tn), lambda j, i: (0, j)),
        ],
        out_shape=[jax.ShapeDtypeStruct((m, d_ff), BF16),
                   jax.ShapeDtypeStruct((batch * SUBLANES, d_ff), F32),
                   jax.ShapeDtypeStruct((ms, d_ff), F32)],
        scratch_shapes=[pltpu.VMEM((k, tn), BF16), pltpu.VMEM((k, tn), BF16),
                        pltpu.VMEM((SUBLANES, tn), F32),
                        pltpu.VMEM((2 * tn // MXU_COLS, tm, MXU_COLS), F32)],
        compiler_params=_params(("arbitrary", "arbitrary"), nbytes),
        name=name,
    )(h, w_up, w_up, conv_w, conv_b, e, e)


TM = 1408
TM_NORM = 768
TM_DOWN = 384
TN_IN = 512
TN_UP = 512
TN_OUT = 512


def kernel(x_prompt, x_sample, cache_win_k, cache_win_v, state_ret, state_conv,
           g_mix, w_in, sinks, w_proj_a, w_proj_b, w_o, g_ffn, w_up, conv_w,
           conv_b, w_down, g_final):
    batch, seq, d_model = x_prompt.shape
    dec_batch, dec_seq, _ = x_sample.shape
    depth = w_in.shape[0]
    d_ff = w_down.shape[1]
    mp = batch * seq
    ms = dec_batch * dec_seq
    a_width = A_HEADS * A_HEAD_DIM
    kv_width = A_KV_HEADS * A_HEAD_DIM
    r_width = R_HEADS * R_DIM
    ret_col0 = a_width + 2 * kv_width
    ga_col = ret_col0 + 4 * r_width
    gb_col = ga_col + d_model
    assert dec_seq & (dec_seq - 1) == 0 and R_CHUNK % dec_seq == 0

    pos_p = jnp.arange(seq, dtype=F32)
    pos_s = PAST_LEN + jnp.arange(dec_seq, dtype=F32)
    rope_p = _rope_tables(pos_p)
    rope_s = tuple(jnp.tile(t, (SWA_SAMPLE_SEQS, 1)) for t in _rope_tables(pos_s))
    rrot_p = _retrot_tables(pos_p)
    nseq = R_CHUNK // dec_seq
    rrot_s = tuple(jnp.tile(t, (nseq, 1)) for t in _retrot_tables(pos_s))
    tabs_p = _decay_tables(R_CHUNK, 1)
    tabs_s = _decay_tables(dec_seq, nseq)
    g_mix3 = g_mix.reshape(depth, 1, d_model)
    g_ffn3 = g_ffn.reshape(depth, 1, d_model)
    conv_b3 = conv_b.reshape(depth, 1, d_ff)

    e_conv = jnp.repeat(jnp.swapaxes(state_conv, 1, 2), dec_seq, axis=2)
    v_col0 = a_width + kv_width

    kp_l, vp_l, sp_l, cp_l = [], [], [], []
    ks_l, vs_l, ss_l, cs_l = [], [], [], []
    for l in range(depth):
        if l == 0:
            x, h = _join_norm(x_prompt.reshape(mp, d_model), x_sample.reshape(ms, d_model),
                              g_mix3, l, name="join_norm_mix0")
        else:
            h = _rmsnorm(x, g_mix3, l, BF16, tm=TM_NORM, name=f"norm_mix{l}")
        z = _matmul(h, w_in, l, tm=TM, tn=TN_IN, out_dtype=BF16, name=f"proj_in{l}")

        oa_p, kr_p = _swa_prompt(z, sinks, l, *rope_p, batch=batch, seq=seq,
                                 name=f"swa_prompt{l}")
        oa_s, kr_s = _swa_sample(z, cache_win_k, cache_win_v, sinks, l, *rope_s,
                                 row0=mp, dec_batch=dec_batch, dec_seq=dec_seq,
                                 name=f"swa_sample{l}")
        ob_p, s_p = _ret_prompt(z, *rrot_p, tabs_p, col0=ret_col0, batch=batch,
                                seq=seq, name=f"ret_prompt{l}")
        ob_s, s_s = _ret_sample(z, state_ret, l, *rrot_s, tabs_s, row0=mp,
                                col0=ret_col0, dec_batch=dec_batch, dec_seq=dec_seq,
                                name=f"ret_sample{l}")
        oa = jnp.concatenate([oa_p, oa_s], axis=0)
        ob = jnp.concatenate([ob_p, ob_s], axis=0)
        merged = _merge(oa, ob, z, w_proj_a, w_proj_b, l, ga_col=ga_col,
                        gb_col=gb_col, tm=TM, tn=TN_OUT, name=f"merge{l}")
        x = _matmul(merged, w_o, l, tm=TM, tn=TN_OUT, out_dtype=F32, residual=x,
                    name=f"proj_out{l}")

        h2 = _rmsnorm(x, g_ffn3, l, BF16, tm=TM_NORM, name=f"norm_ffn{l}")
        f, u_tail, u_s = _ffn_up(h2, w_up, conv_w, conv_b3, l, e_conv, tm=TM, tn=TN_UP,
                                 seq=seq, batch=batch, dec_seq=dec_seq, name=f"ffn_up{l}")
        x = _matmul(f, w_down, l, tm=TM_DOWN, tn=TN_OUT, out_dtype=F32, residual=x,
                    name=f"proj_down{l}")

        v_p = jnp.stack([lax.slice(z, ((b + 1) * seq - WINDOW, v_col0), ((b + 1) * seq, ret_col0))
                         for b in range(batch)])
        v_s = lax.slice(z, (mp, v_col0), (mp + ms, ret_col0))
        kp_l.append(kr_p.reshape(batch, WINDOW, A_KV_HEADS, A_HEAD_DIM))
        vp_l.append(v_p.astype(F32).reshape(batch, WINDOW, A_KV_HEADS, A_HEAD_DIM))
        sp_l.append(s_p)
        cp_l.append(u_tail.reshape(batch, SUBLANES, d_ff)[:, -(CONV_W - 1):])
        ks_l.append(kr_s.reshape(dec_batch, dec_seq, A_KV_HEADS, A_HEAD_DIM))
        vs_l.append(v_s.astype(F32).reshape(dec_batch, dec_seq, A_KV_HEADS, A_HEAD_DIM))
        ss_l.append(s_s)
        cs_l.append(u_s.reshape(dec_batch, dec_seq, d_ff)[:, -(CONV_W - 1):])

    y_prompt, y_sample = _split_norm(x, g_final, mp=mp, name="norm_final")
    y_prompt = y_prompt.reshape(batch, seq, d_model)
    y_sample = y_sample.reshape(dec_batch, dec_seq, d_model)
    return (y_prompt, y_sample,
            jnp.stack(kp_l), jnp.stack(vp_l), jnp.stack(sp_l), jnp.stack(cp_l),
            jnp.stack(ks_l), jnp.stack(vs_l), jnp.stack(ss_l), jnp.stack(cs_l))
```

```python
import functools

import jax
import jax.numpy as jnp
import numpy as np
from jax import lax
from jax.experimental import pallas as pl
from jax.experimental.pallas import tpu as pltpu

F32 = jnp.float32
BF16 = jnp.bfloat16

LANES = 128
SUBLANES = 8
MXU_COLS = 256
FIX_ROWS = 2 * SUBLANES
VMEM_LIMIT_CAP = 56 * 1024 * 1024

WINDOW = 128
A_HEADS = 16
A_KV_HEADS = 4
A_HEAD_DIM = 64
A_GROUP = A_HEADS // A_KV_HEADS
R_HEADS = 8
R_DIM = 128
R_CHUNK = 128
CONV_W = 3
EPS = 1e-6
ROPE_THETA = 10000.0
PAST_LEN = 16384

NT_DIMS = (((1,), (1,)), ((), ()))
TN_DIMS = (((0,), (0,)), ((), ()))


def _vmem_limit(nbytes):
    return int(min(VMEM_LIMIT_CAP, max(32 * 1024 * 1024, nbytes)))


def _params(semantics, nbytes):
    return pltpu.CompilerParams(dimension_semantics=semantics,
                                vmem_limit_bytes=_vmem_limit(nbytes))


def _rmsnorm_body(x_ref, g_ref, o_ref):
    x = x_ref[...]
    ms = jnp.mean(x * x, axis=-1, keepdims=True)
    o_ref[...] = (x * lax.rsqrt(ms + EPS) * g_ref[...]).astype(o_ref.dtype)


def _rmsnorm(x, g, layer, out_dtype, *, tm, name):
    m, d = x.shape
    return pl.pallas_call(
        _rmsnorm_body,
        grid=(m // tm,),
        in_specs=[pl.BlockSpec((tm, d), lambda i: (i, 0)),
                  pl.BlockSpec((None, 1, d), lambda i: (layer, 0, 0))],
        out_specs=pl.BlockSpec((tm, d), lambda i: (i, 0)),
        out_shape=jax.ShapeDtypeStruct((m, d), out_dtype),
        compiler_params=_params(("arbitrary",), 6 * tm * d * 4),
        name=name,
    )(x, g)


def _join_norm_body(xp_ref, xs_ref, g_ref, x_ref, h_ref, *, prompt_tiles):
    i = pl.program_id(0)

    def emit(x):
        ms = jnp.mean(x * x, axis=-1, keepdims=True)
        x_ref[...] = x
        h_ref[...] = (x * lax.rsqrt(ms + EPS) * g_ref[...]).astype(h_ref.dtype)

    @pl.when(i < prompt_tiles)
    def _():
        emit(xp_ref[...])

    @pl.when(i >= prompt_tiles)
    def _():
        emit(xs_ref[...])


def _join_norm(xp, xs, g, layer, *, name):
    mp, d = xp.shape
    ms = xs.shape[0]
    assert mp % ms == 0
    pt = mp // ms
    return pl.pallas_call(
        functools.partial(_join_norm_body, prompt_tiles=pt),
        grid=(pt + 1,),
        in_specs=[pl.BlockSpec((ms, d), lambda i: (jnp.minimum(i, pt - 1), 0)),
                  pl.BlockSpec((ms, d), lambda i: (0, 0)),
                  pl.BlockSpec((None, 1, d), lambda i: (layer, 0, 0))],
        out_specs=[pl.BlockSpec((ms, d), lambda i: (i, 0)),
                   pl.BlockSpec((ms, d), lambda i: (i, 0))],
        out_shape=[jax.ShapeDtypeStruct((mp + ms, d), F32),
                   jax.ShapeDtypeStruct((mp + ms, d), BF16)],
        compiler_params=_params(("arbitrary",), 0),
        name=name,
    )(xp, xs, g)


def _split_norm_body(x_ref, g_ref, yp_ref, ys_ref, *, prompt_tiles):
    i = pl.program_id(0)
    x = x_ref[...]
    ms = jnp.mean(x * x, axis=-1, keepdims=True)
    y = x * lax.rsqrt(ms + EPS) * g_ref[...]

    @pl.when(i < prompt_tiles)
    def _():
        yp_ref[...] = y

    @pl.when(i >= prompt_tiles)
    def _():
        ys_ref[...] = y


def _split_norm(x, g, *, mp, name):
    m, d = x.shape
    ms = m - mp
    assert mp % ms == 0
    pt = mp // ms
    return pl.pallas_call(
        functools.partial(_split_norm_body, prompt_tiles=pt),
        grid=(pt + 1,),
        in_specs=[pl.BlockSpec((ms, d), lambda i: (i, 0)),
                  pl.BlockSpec((1, d), lambda i: (0, 0))],
        out_specs=[pl.BlockSpec((ms, d), lambda i: (jnp.minimum(i, pt - 1), 0)),
                   pl.BlockSpec((ms, d), lambda i: (0, 0))],
        out_shape=[jax.ShapeDtypeStruct((mp, d), F32),
                   jax.ShapeDtypeStruct((ms, d), F32)],
        compiler_params=_params(("arbitrary",), 0),
        name=name,
    )(x, g.reshape(1, d))


def _mm_body(x_ref, w_ref, o_ref, wbf_ref):
    @pl.when(pl.program_id(1) == 0)
    def _():
        wbf_ref[...] = w_ref[...].astype(BF16)

    o_ref[...] = jnp.dot(x_ref[...], wbf_ref[...],
                         preferred_element_type=F32).astype(o_ref.dtype)


def _mm_res_body(x_ref, w_ref, r_ref, o_ref, wbf_ref):
    @pl.when(pl.program_id(1) == 0)
    def _():
        wbf_ref[...] = w_ref[...].astype(BF16)

    o_ref[...] = r_ref[...] + jnp.dot(x_ref[...], wbf_ref[...],
                                      preferred_element_type=F32)


def _matmul(x, w, layer, *, tm, tn, out_dtype, residual=None, name):
    m, k = x.shape
    n = w.shape[2]
    grid = (n // tn, m // tm)
    in_specs = [pl.BlockSpec((tm, k), lambda j, i: (i, 0)),
                pl.BlockSpec((None, k, tn), lambda j, i: (layer, 0, j))]
    args = [x, w]
    body = _mm_body
    if residual is not None:
        in_specs.append(pl.BlockSpec((tm, tn), lambda j, i: (i, j)))
        args.append(residual)
        body = _mm_res_body
    nbytes = (2 * tm * k * 2 + 2 * k * tn * 4 + k * tn * 2
              + 6 * tm * tn * 4 + (4 << 20))
    return pl.pallas_call(
        body,
        grid=grid,
        in_specs=in_specs,
        out_specs=pl.BlockSpec((tm, tn), lambda j, i: (i, j)),
        out_shape=jax.ShapeDtypeStruct((m, n), out_dtype),
        scratch_shapes=[pltpu.VMEM((k, tn), BF16)],
        compiler_params=_params(("arbitrary", "arbitrary"), nbytes),
        name=name,
    )(*args)


def _merge_body(oa_ref, ob_ref, ga_ref, gb_ref, wa_ref, wb_ref, o_ref,
                wabf_ref, wbbf_ref):
    @pl.when(pl.program_id(1) == 0)
    def _():
        wabf_ref[...] = wa_ref[...].astype(BF16)
        wbbf_ref[...] = wb_ref[...].astype(BF16)

    a = jnp.dot(oa_ref[...], wabf_ref[...], preferred_element_type=F32)
    b = jnp.dot(ob_ref[...], wbbf_ref[...], preferred_element_type=F32)
    ga = jax.nn.sigmoid(ga_ref[...].astype(F32))
    gb = jax.nn.sigmoid(gb_ref[...].astype(F32))
    o_ref[...] = (ga * a + gb * b).astype(o_ref.dtype)


def _merge(oa, ob, z, wa, wb, layer, *, ga_col, gb_col, tm, tn, name):
    m, k = oa.shape
    n = wa.shape[2]
    ga_blk = ga_col // tn
    gb_blk = gb_col // tn
    nbytes = (4 * tm * k * 2 + 4 * k * tn * 4 + 2 * k * tn * 2
              + 4 * tm * tn * 2 + 8 * tm * tn * 4 + (4 << 20))
    return pl.pallas_call(
        _merge_body,
        grid=(n // tn, m // tm),
        in_specs=[pl.BlockSpec((tm, k), lambda j, i: (i, 0)),
                  pl.BlockSpec((tm, k), lambda j, i: (i, 0)),
                  pl.BlockSpec((tm, tn), lambda j, i: (i, ga_blk + j)),
                  pl.BlockSpec((tm, tn), lambda j, i: (i, gb_blk + j)),
                  pl.BlockSpec((None, k, tn), lambda j, i: (layer, 0, j)),
                  pl.BlockSpec((None, k, tn), lambda j, i: (layer, 0, j))],
        out_specs=pl.BlockSpec((tm, tn), lambda j, i: (i, j)),
        out_shape=jax.ShapeDtypeStruct((m, n), BF16),
        scratch_shapes=[pltpu.VMEM((k, tn), BF16), pltpu.VMEM((k, tn), BF16)],
        compiler_params=_params(("arbitrary", "arbitrary"), nbytes),
        name=name,
    )(oa, ob, z, z, wa, wb)


def _rope_tables(pos):
    half = A_HEAD_DIM // 2
    inv = ROPE_THETA ** (-2.0 * jnp.arange(half, dtype=F32) / A_HEAD_DIM)
    ang = pos[:, None] * inv[None, :]
    c, s = jnp.cos(ang), jnp.sin(ang)
    cos = jnp.concatenate([c, c, c, c], axis=-1)
    sin = jnp.concatenate([-s, s, -s, s], axis=-1)
    return cos, sin


def _retrot_tables(pos):
    half = R_DIM // 2
    inv = 1.0 / (10000.0 ** jnp.linspace(0.0, 1.0, half, dtype=F32))
    ang = pos[:, None] * inv[None, :]
    c, s = jnp.cos(ang), jnp.sin(ang)
    cos = jnp.stack([c, c], axis=-1).reshape(pos.shape[0], R_DIM)
    sin = jnp.stack([-s, s], axis=-1).reshape(pos.shape[0], R_DIM)
    return cos, sin


def _rope128(x, cos, sin):
    lane = lax.broadcasted_iota(jnp.int32, x.shape, 1)
    first = (lane & (A_HEAD_DIM - 1)) < (A_HEAD_DIM // 2)
    partner = jnp.where(first, pltpu.roll(x, LANES - 32, 1), pltpu.roll(x, 32, 1))
    return x * cos + partner * sin


def _retrot128(x, cos, sin):
    lane = lax.broadcasted_iota(jnp.int32, x.shape, 1)
    even = (lane & 1) == 0
    partner = jnp.where(even, pltpu.roll(x, LANES - 1, 1), pltpu.roll(x, 1, 1))
    return x * cos + partner * sin


def _swa_heads(q_chunks, k_chunks, v_chunks, valid, sink_ref, layer):
    rows = q_chunks[0].shape[0]
    lane = lax.broadcasted_iota(jnp.int32, (rows, LANES), 1)
    low = lane < A_HEAD_DIM
    outs = []
    for pair in range(A_HEADS // 2):
        halves = []
        for hpos in range(2):
            h = 2 * pair + hpos
            kv = h // A_GROUP
            c, kpos = kv // 2, kv % 2
            qc = q_chunks[pair]
            if hpos != kpos:
                qc = pltpu.roll(qc, A_HEAD_DIM, 1)
            keep = low if kpos == 0 else jnp.logical_not(low)
            qm = jnp.where(keep, qc, 0.0).astype(BF16)
            s = lax.dot_general(qm, k_chunks[c], NT_DIMS, preferred_element_type=F32)
            s = jnp.where(valid, s, -jnp.inf)
            sink = sink_ref[layer, h]
            m = jnp.maximum(jnp.max(s, axis=-1, keepdims=True), sink)
            p = jnp.exp(s - m)
            denom = jnp.sum(p, axis=-1, keepdims=True) + jnp.exp(sink - m)
            o = jnp.dot(p.astype(BF16), v_chunks[c], preferred_element_type=F32)
            o = o / denom
            if hpos != kpos:
                o = pltpu.roll(o, A_HEAD_DIM, 1)
            halves.append(o)
        outs.append(jnp.where(low, halves[0], halves[1]))
    return outs


def _swa_prompt_body(sink_ref, q_ref, k_ref, vc_ref, vp_ref, cos_ref, sin_ref,
                     o_ref, kr_ref, kprev_ref, *, layer):
    n = pl.program_id(1)
    cos = cos_ref[...]
    sin = sin_ref[...]

    @pl.when(n == 0)
    def _():
        kprev_ref[...] = jnp.zeros_like(kprev_ref)

    k_chunks, v_chunks = [], []
    k_new = []
    for c in range(2):
        sl = slice(c * LANES, (c + 1) * LANES)
        kc = _rope128(k_ref[:, sl].astype(F32), cos, sin)
        kr_ref[:, sl] = kc
        kc = kc.astype(BF16)
        k_new.append(kc)
        k_chunks.append(jnp.concatenate([kprev_ref[:, sl], kc], axis=0))
        v_chunks.append(jnp.concatenate([vp_ref[:, sl], vc_ref[:, sl]], axis=0))

    scale = A_HEAD_DIM ** -0.5
    q_chunks = [
        _rope128(q_ref[:, c * LANES:(c + 1) * LANES].astype(F32), cos, sin) * scale
        for c in range(A_HEADS // 2)]

    qi = lax.broadcasted_iota(jnp.int32, (WINDOW, 2 * WINDOW), 0) + WINDOW
    ki = lax.broadcasted_iota(jnp.int32, (WINDOW, 2 * WINDOW), 1)
    diff = qi - ki
    valid = (diff >= 0) & (diff < WINDOW) & ((n * WINDOW + ki - WINDOW) >= 0)

    outs = _swa_heads(q_chunks, k_chunks, v_chunks, valid, sink_ref, layer)
    for c, o in enumerate(outs):
        o_ref[:, c * LANES:(c + 1) * LANES] = o.astype(o_ref.dtype)
    for c in range(2):
        kprev_ref[:, c * LANES:(c + 1) * LANES] = k_new[c]


def _swa_prompt(z, sinks, layer, cos, sin, *, batch, seq, out_rows, name):
    nb = seq // WINDOW
    a_width = A_HEADS * A_HEAD_DIM
    kv_width = A_KV_HEADS * A_HEAD_DIM
    k_blk = a_width // kv_width
    v_blk = k_blk + 1
    rows = out_rows
    return pl.pallas_call(
        functools.partial(_swa_prompt_body, layer=layer),
        grid=(batch, nb),
        in_specs=[
            pl.BlockSpec(memory_space=pltpu.SMEM),
            pl.BlockSpec((WINDOW, a_width), lambda b, n: (b * nb + n, 0)),
            pl.BlockSpec((WINDOW, kv_width), lambda b, n: (b * nb + n, k_blk)),
            pl.BlockSpec((WINDOW, kv_width), lambda b, n: (b * nb + n, v_blk)),
            pl.BlockSpec((WINDOW, kv_width),
                         lambda b, n: (b * nb + jnp.maximum(n - 1, 0), v_blk)),
            pl.BlockSpec((WINDOW, LANES), lambda b, n: (n, 0)),
            pl.BlockSpec((WINDOW, LANES), lambda b, n: (n, 0)),
        ],
        out_specs=[
            pl.BlockSpec((WINDOW, a_width), lambda b, n: (b * nb + n, 0)),
            pl.BlockSpec((WINDOW, kv_width), lambda b, n: (b, 0)),
        ],
        out_shape=[jax.ShapeDtypeStruct((rows, a_width), BF16),
                   jax.ShapeDtypeStruct((batch * WINDOW, kv_width), F32)],
        scratch_shapes=[pltpu.VMEM((WINDOW, kv_width), BF16)],
        compiler_params=_params(("arbitrary", "arbitrary"), 0),
        name=name,
    )(sinks, z, z, z, z, cos, sin)


SWA_SAMPLE_SEQS = 4


def _swa_sample_body(sink_ref, q_ref, k_ref, v_ref, ck_ref, cv_ref, cos_ref,
                     sin_ref, oa_hbm_ref, o_ref, kr_ref, *, dec_seq, layer):
    del oa_hbm_ref
    cos = cos_ref[...]
    sin = sin_ref[...]
    g = SWA_SAMPLE_SEQS
    rows = g * dec_seq
    ncache = g * WINDOW
    k_chunks, v_chunks = [], []
    for c in range(2):
        sl = slice(c * LANES, (c + 1) * LANES)
        kc = _rope128(k_ref[:, sl].astype(F32), cos, sin)
        kr_ref[:, sl] = kc
        k_chunks.append(jnp.concatenate(
            [ck_ref[:, sl].astype(BF16), kc.astype(BF16)], axis=0))
        v_chunks.append(jnp.concatenate(
            [cv_ref[:, sl].astype(BF16), v_ref[:, sl]], axis=0))

    scale = A_HEAD_DIM ** -0.5
    q_chunks = [
        _rope128(q_ref[:, c * LANES:(c + 1) * LANES].astype(F32), cos, sin) * scale
        for c in range(A_HEADS // 2)]

    shape = (rows, ncache + rows)
    r = lax.broadcasted_iota(jnp.int32, shape, 0)
    col = lax.broadcasted_iota(jnp.int32, shape, 1)
    log_l = dec_seq.bit_length() - 1
    log_w = WINDOW.bit_length() - 1
    seq_q = r >> log_l
    i = r & (dec_seq - 1)
    in_cache = col < ncache
    cnew = col - ncache
    seq_k = jnp.where(in_cache, col >> log_w, cnew >> log_l)
    ok = ((in_cache & ((col & (WINDOW - 1)) > i))
          | (jnp.logical_not(in_cache) & ((cnew & (dec_seq - 1)) <= i)))
    valid = (seq_k == seq_q) & ok

    outs = _swa_heads(q_chunks, k_chunks, v_chunks, valid, sink_ref, layer)
    for c, o in enumerate(outs):
        o_ref[:, c * LANES:(c + 1) * LANES] = o.astype(o_ref.dtype)


def _swa_sample(z, cache_k, cache_v, sinks, layer, oa, cos, sin, *, row0, dec_batch, dec_seq,
                name):
    g = SWA_SAMPLE_SEQS
    rows = g * dec_seq
    a_width = A_HEADS * A_HEAD_DIM
    kv_width = A_KV_HEADS * A_HEAD_DIM
    k_blk = a_width // kv_width
    v_blk = k_blk + 1
    r0 = row0 // rows
    depth = cache_k.shape[0]
    ck = cache_k.reshape(depth, dec_batch * WINDOW, kv_width)
    cv = cache_v.reshape(depth, dec_batch * WINDOW, kv_width)
    return pl.pallas_call(
        functools.partial(_swa_sample_body, dec_seq=dec_seq, layer=layer),
        grid=(dec_batch // g,),
        in_specs=[
            pl.BlockSpec(memory_space=pltpu.SMEM),
            pl.BlockSpec((rows, a_width), lambda s: (r0 + s, 0)),
            pl.BlockSpec((rows, kv_width), lambda s: (r0 + s, k_blk)),
            pl.BlockSpec((rows, kv_width), lambda s: (r0 + s, v_blk)),
            pl.BlockSpec((None, g * WINDOW, kv_width), lambda s: (layer, s, 0)),
            pl.BlockSpec((None, g * WINDOW, kv_width), lambda s: (layer, s, 0)),
            pl.BlockSpec((rows, LANES), lambda s: (0, 0)),
            pl.BlockSpec((rows, LANES), lambda s: (0, 0)),
            pl.BlockSpec(memory_space=pl.ANY),
        ],
        out_specs=[
            pl.BlockSpec((rows, a_width), lambda s: (r0 + s, 0)),
            pl.BlockSpec((rows, kv_width), lambda s: (s, 0)),
        ],
        out_shape=[jax.ShapeDtypeStruct(oa.shape, oa.dtype),
                   jax.ShapeDtypeStruct((dec_batch * dec_seq, kv_width), F32)],
        input_output_aliases={8: 0},
        compiler_params=_params(("arbitrary",), 0),
        name=name,
    )(sinks, z, z, z, ck, cv, cos, sin, oa)


RET_HEADS_PER_STEP = 4
RET_BLOCK = RET_HEADS_PER_STEP * R_DIM


def _decay_tables(length, nseq):
    log_g = np.log1p(-np.exp2(-5.0 - np.arange(R_HEADS, dtype=np.float64)))
    i = np.arange(length, dtype=np.float64)
    diff = i[:, None] - i[None, :]
    d = np.where(diff >= 0, np.exp(log_g[:, None, None] * np.maximum(diff, 0.0)), 0.0)
    d = np.einsum('st,hij->hsitj', np.eye(nseq), d).reshape(
        R_HEADS, nseq * length, nseq * length)
    q_dec = np.exp(log_g[:, None] * (i[None, :] + 1.0))
    k_dec = np.exp(log_g[:, None] * (length - 1.0 - i)[None, :])
    shape = (R_HEADS, nseq * length, R_DIM)
    qd = np.broadcast_to(np.tile(q_dec, (1, nseq))[:, :, None], shape)
    kd = np.broadcast_to(np.tile(k_dec, (1, nseq))[:, :, None], shape)
    g_len = np.exp(log_g * length)
    return tuple(jnp.asarray(t, F32) for t in (d, qd, kd, g_len))


def _group_norm_gate(o, gate):
    o = o * lax.rsqrt(jnp.mean(o * o, axis=-1, keepdims=True) + EPS)
    return o * (gate * jax.nn.sigmoid(gate))


def _ret_prompt_body(gl_ref, *refs):
    nb = R_HEADS // RET_HEADS_PER_STEP
    q_refs, k_refs, v_refs, g_refs = (refs[t * nb:(t + 1) * nb] for t in range(4))
    cos_ref, sin_ref, d_ref, qd_ref, kd_ref, o_ref, s_ref = refs[4 * nb:]
    c = pl.program_id(1)
    cos = cos_ref[...]
    sin = sin_ref[...]

    @pl.when(c == 0)
    def _():
        s_ref[...] = jnp.zeros_like(s_ref)

    scale = R_DIM ** -0.5
    for h in range(R_HEADS):
        blk = h // RET_HEADS_PER_STEP
        sl = slice((h % RET_HEADS_PER_STEP) * R_DIM, (h % RET_HEADS_PER_STEP + 1) * R_DIM)
        q = _retrot128(q_refs[blk][:, sl].astype(F32), cos, sin)
        k = _retrot128(k_refs[blk][:, sl].astype(F32), cos, sin) * scale
        v = v_refs[blk][:, sl]
        qb = q.astype(BF16)
        state = s_ref[0, h]
        scores = lax.dot_general(qb, k.astype(BF16), NT_DIMS,
                                 preferred_element_type=F32) * d_ref[h]
        o = jnp.dot(scores.astype(BF16), v, preferred_element_type=F32)
        cross = jnp.dot(qb, state.astype(BF16), preferred_element_type=F32)
        o = o + cross * qd_ref[h]
        kd = (k * kd_ref[h]).astype(BF16)
        s_ref[0, h] = gl_ref[h] * state + lax.dot_general(
            kd, v, TN_DIMS, preferred_element_type=F32)
        o_ref[:, h * R_DIM:(h + 1) * R_DIM] = _group_norm_gate(
            o, g_refs[blk][:, sl].astype(F32)).astype(o_ref.dtype)


def _ret_prompt(z, cos, sin, tabs, *, col0, batch, seq, out_rows, name):
    d, qd, kd, g_len = tabs
    nc = seq // R_CHUNK
    nb = R_HEADS // RET_HEADS_PER_STEP
    width = R_HEADS * R_DIM
    blks = [(col0 + t * width) // RET_BLOCK + i for t in range(4) for i in range(nb)]

    def zspec(blk):
        return pl.BlockSpec((R_CHUNK, RET_BLOCK), lambda b, c: (b * nc + c, blk))

    def tspec():
        return pl.BlockSpec((R_HEADS, R_CHUNK, R_DIM), lambda b, c: (0, 0, 0))

    return pl.pallas_call(
        _ret_prompt_body,
        grid=(batch, nc),
        in_specs=[pl.BlockSpec(memory_space=pltpu.SMEM)]
        + [zspec(blk) for blk in blks]
        + [pl.BlockSpec((R_CHUNK, R_DIM), lambda b, c: (c, 0)),
           pl.BlockSpec((R_CHUNK, R_DIM), lambda b, c: (c, 0)),
           tspec(), tspec(), tspec()],
        out_specs=[
            pl.BlockSpec((R_CHUNK, width), lambda b, c: (b * nc + c, 0)),
            pl.BlockSpec((1, R_HEADS, R_DIM, R_DIM), lambda b, c: (b, 0, 0, 0)),
        ],
        out_shape=[jax.ShapeDtypeStruct((out_rows, width), BF16),
                   jax.ShapeDtypeStruct((batch, R_HEADS, R_DIM, R_DIM), F32)],
        compiler_params=_params(("arbitrary", "arbitrary"), 0),
        name=name,
    )(g_len, *([z] * len(blks)), cos, sin, d, qd, kd)


def _ret_sample_body(gl_ref, q_ref, k_ref, v_ref, g_ref, cos_ref, sin_ref,
                     d_ref, qd_ref, kd_ref, s_ref, ob_hbm_ref, o_ref, so_ref, *, dec_seq):
    del ob_hbm_ref
    hh = pl.program_id(1)
    cos = cos_ref[...]
    sin = sin_ref[...]
    nseq = R_CHUNK // dec_seq
    log_l = dec_seq.bit_length() - 1
    row_seq = lax.broadcasted_iota(jnp.int32, (R_CHUNK, R_DIM), 0) >> log_l
    scale = R_DIM ** -0.5
    for hl in range(RET_HEADS_PER_STEP):
        sl = slice(hl * R_DIM, (hl + 1) * R_DIM)
        q = _retrot128(q_ref[:, sl].astype(F32), cos, sin)
        k = _retrot128(k_ref[:, sl].astype(F32), cos, sin) * scale
        v = v_ref[:, sl]
        qb = q.astype(BF16)
        scores = lax.dot_general(qb, k.astype(BF16), NT_DIMS,
                                 preferred_element_type=F32) * d_ref[hl]
        o = jnp.dot(scores.astype(BF16), v, preferred_element_type=F32)
        kd = k * kd_ref[hl]
        gl = gl_ref[hh * RET_HEADS_PER_STEP + hl]
        cross = jnp.zeros((R_CHUNK, R_DIM), F32)
        for s in range(nseq):
            mine = row_seq == s
            state = s_ref[s, hl]
            cs = jnp.dot(qb, state.astype(BF16), preferred_element_type=F32)
            cross = jnp.where(mine, cs, cross)
            ks = jnp.where(mine, kd, 0.0).astype(BF16)
            so_ref[s, hl] = gl * state + lax.dot_general(
                ks, v, TN_DIMS, preferred_element_type=F32)
        o = o + cross * qd_ref[hl]
        o_ref[:, sl] = _group_norm_gate(o, g_ref[:, sl].astype(F32)).astype(o_ref.dtype)


def _ret_sample(z, state, layer, ob, cos, sin, tabs, *, row0, col0, dec_batch, dec_seq, name):
    d, qd, kd, g_len = tabs
    nseq = R_CHUNK // dec_seq
    steps = dec_batch // nseq
    nh = R_HEADS // RET_HEADS_PER_STEP
    width = R_HEADS * R_DIM
    qb, kb, vb, gb = [(col0 + t * width) // RET_BLOCK for t in range(4)]
    r0 = row0 // R_CHUNK
    hp = RET_HEADS_PER_STEP

    def zspec(blk):
        return pl.BlockSpec((R_CHUNK, RET_BLOCK), lambda s, h: (r0 + s, blk + h))

    def tspec():
        return pl.BlockSpec((hp, R_CHUNK, R_DIM), lambda s, h: (h, 0, 0))

    sspec_in = pl.BlockSpec((None, nseq, hp, R_DIM, R_DIM), lambda s, h: (layer, s, h, 0, 0))
    sspec = pl.BlockSpec((nseq, hp, R_DIM, R_DIM), lambda s, h: (s, h, 0, 0))
    return pl.pallas_call(
        functools.partial(_ret_sample_body, dec_seq=dec_seq),
        grid=(steps, nh),
        in_specs=[
            pl.BlockSpec(memory_space=pltpu.SMEM),
            zspec(qb), zspec(kb), zspec(vb), zspec(gb),
            pl.BlockSpec((R_CHUNK, R_DIM), lambda s, h: (0, 0)),
            pl.BlockSpec((R_CHUNK, R_DIM), lambda s, h: (0, 0)),
            tspec(), tspec(), tspec(),
            sspec_in,
            pl.BlockSpec(memory_space=pl.ANY),
        ],
        out_specs=[
            pl.BlockSpec((R_CHUNK, RET_BLOCK), lambda s, h: (r0 + s, h)),
            sspec,
        ],
        out_shape=[jax.ShapeDtypeStruct(ob.shape, ob.dtype),
                   jax.ShapeDtypeStruct(state.shape[1:], F32)],
        input_output_aliases={11: 0},
        compiler_params=_params(("arbitrary", "arbitrary"),
                                4 * nseq * hp * R_DIM * R_DIM * 4 + (16 << 20)),
        name=name,
    )(g_len, z, z, z, z, cos, sin, d, qd, kd, state, ob)


def _gelu_tanh(x):
    c = (2.0 / jnp.pi) ** 0.5
    return 0.5 * x * (1.0 + jnp.tanh(c * (x + 0.044715 * (x * x * x))))


def _ffn_up_body(h_ref, wu_ref, wg_ref, cw_ref, cb_ref, e0_ref, e1_ref,
                 f_ref, tail_ref, us_ref, wubf_ref, wgbf_ref, carry_ref, ug_ref,
                 *, tm, n_tiles, seq, batch, ms, dec_seq):
    i = pl.program_id(1)
    tn = f_ref.shape[1]
    mp = batch * seq

    @pl.when(i == 0)
    def _():
        wubf_ref[...] = wu_ref[...].astype(BF16)
        wgbf_ref[...] = wg_ref[...].astype(BF16)
        carry_ref[...] = jnp.zeros_like(carry_ref)

    last = n_tiles - 1
    s0 = mp - last * tm
    assert 0 <= s0 and s0 + ms == tm and s0 % FIX_ROWS == 0
    starts = [k * seq for k in range(1, batch)]
    for r in starts:
        assert r % tm != 0 and r % FIX_ROWS == 0, "mid-tile, packed-tile aligned starts only"

    def conv_gate(u, gate, p1, p2, w, bias):
        conv = bias + w[0:1, :] * p2
        conv = conv + w[1:2, :] * p1
        conv = conv + w[2:3, :] * u
        return (_gelu_tanh(conv) * gate).astype(f_ref.dtype)

    tr = tm // FFN_ROW_PARTS
    assert tm % FFN_ROW_PARTS == 0 and tr % FIX_ROWS == 0
    for c in range(tn // MXU_COLS):
        cs = slice(c * MXU_COLS, (c + 1) * MXU_COLS)
        w = cw_ref[:, cs]
        bias = cb_ref[:, cs]
        for part in range(FFN_ROW_PARTS):
            rs = slice(part * tr, (part + 1) * tr)
            h = h_ref[rs, :]
            ug_ref[2 * c, rs, :] = jnp.dot(h, wubf_ref[:, cs], preferred_element_type=F32)
            ug_ref[2 * c + 1, rs, :] = jnp.dot(h, wgbf_ref[:, cs], preferred_element_type=F32)
            u = ug_ref[2 * c, rs, :]
            gate = ug_ref[2 * c + 1, rs, :]

            if part == 0:
                before = carry_ref[:, cs]
            else:
                before = ug_ref[2 * c, part * tr - SUBLANES:part * tr, :]
            last1 = before[SUBLANES - 1:SUBLANES, :]
            last2 = before[SUBLANES - 2:SUBLANES - 1, :]
            row = lax.broadcasted_iota(jnp.int32, u.shape, 0)
            prev1 = jnp.where(row == 0, last1, pltpu.roll(u, 1, 0))
            prev2 = jnp.where(row == 0, last2,
                              jnp.where(row == 1, last1, pltpu.roll(u, 2, 0)))
            f_ref[rs, cs] = conv_gate(u, gate, prev1, prev2, w, bias)
        carry_ref[:, cs] = ug_ref[2 * c, tm - SUBLANES:, :]

    def rewrite(rows, p1_fn, p2_fn):
        for c in range(tn // MXU_COLS):
            cs = slice(c * MXU_COLS, (c + 1) * MXU_COLS)
            us = ug_ref[2 * c, rows, :]
            f_ref[rows, cs] = conv_gate(us, ug_ref[2 * c + 1, rows, :], p1_fn(us, cs),
                                        p2_fn(us, cs), cw_ref[:, cs], cb_ref[:, cs])

    for r in starts:
        @pl.when(i == r // tm)
        def _(off=r % tm):
            pos = lax.broadcasted_iota(jnp.int32, (FIX_ROWS, MXU_COLS), 0)
            rewrite(slice(off, off + FIX_ROWS),
                    lambda us, cs: jnp.where(pos == 0, 0.0, pltpu.roll(us, 1, 0)),
                    lambda us, cs: jnp.where(pos <= 1, 0.0, pltpu.roll(us, 2, 0)))

    @pl.when(i == last)
    def _():
        pos = lax.broadcasted_iota(jnp.int32, (ms, MXU_COLS), 0) & (dec_seq - 1)
        rewrite(slice(s0, tm),
                lambda us, cs: jnp.where(pos == 0, e1_ref[:, cs], pltpu.roll(us, 1, 0)),
                lambda us, cs: jnp.where(
                    pos == 0, e0_ref[:, cs],
                    jnp.where(pos == 1, e1_ref[:, cs], pltpu.roll(us, 2, 0))))
        for c in range(tn // MXU_COLS):
            us_ref[:, c * MXU_COLS:(c + 1) * MXU_COLS] = ug_ref[2 * c, s0:, :]

    for b in range(batch):
        r = (b + 1) * seq - SUBLANES

        @pl.when(i == r // tm)
        def _(b=b, off=r % tm):
            for c in range(tn // MXU_COLS):
                tail_ref[b * SUBLANES:(b + 1) * SUBLANES, c * MXU_COLS:(c + 1) * MXU_COLS] = (
                    ug_ref[2 * c, off:off + SUBLANES, :])


def _ffn_up(h, w_up, conv_w, conv_b, layer, e, *, tm, tn, seq, batch, dec_seq, name):
    m, k = h.shape
    d_ff = w_up.shape[2] // 2
    ms = e.shape[2]
    nj = d_ff // tn
    nt = m // tm
    body = functools.partial(_ffn_up_body, tm=tm, n_tiles=nt, seq=seq, batch=batch,
                             ms=ms, dec_seq=dec_seq)
    nbytes = (2 * tm * k * 2 + 4 * k * tn * 4 + 2 * k * tn * 2
              + 16 * tm * tn * 4 + 4 * ms * tn * 4 + (4 << 20))
    return pl.pallas_call(
        body,
        grid=(nj, nt),
        in_specs=[
            pl.BlockSpec((tm, k), lambda j, i: (i, 0)),
            pl.BlockSpec((None, k, tn), lambda j, i: (layer, 0, j)),
            pl.BlockSpec((None, k, tn), lambda j, i: (layer, 0, nj + j)),
            pl.BlockSpec((None, CONV_W, tn), lambda j, i: (layer, 0, j)),
            pl.BlockSpec((None, 1, tn), lambda j, i: (layer, 0, j)),
            pl.BlockSpec((None, None, ms, tn), lambda j, i: (layer, 0, 0, j)),
            pl.BlockSpec((None, None, ms, tn), lambda j, i: (layer, 1, 0, j)),
        ],
        out_specs=[
            pl.BlockSpec((tm, tn), lambda j, i: (i, j)),
            pl.BlockSpec((batch * SUBLANES, tn), lambda j, i: (0, j)),
            pl.BlockSpec((ms, tn), lambda j, i: (0, j)),
        ],
        out_shape=[jax.ShapeDtypeStruct((m, d_ff), BF16),
                   jax.ShapeDtypeStruct((batch * SUBLANES, d_ff), F32),
                   jax.ShapeDtypeStruct((ms, d_ff), F32)],
        scratch_shapes=[pltpu.VMEM((k, tn), BF16), pltpu.VMEM((k, tn), BF16),
                        pltpu.VMEM((SUBLANES, tn), F32),
                        pltpu.VMEM((2 * tn // MXU_COLS, tm, MXU_COLS), F32)],
        compiler_params=_params(("arbitrary", "arbitrary"), nbytes),
        name=name,
    )(h, w_up, w_up, conv_w, conv_b, e, e)


TM = 1408
TM_NORM = 768
TM_DOWN = 704
TN_IN = 512
TN_UP = 512
FFN_ROW_PARTS = 2
TN_OUT = 512


def kernel(x_prompt, x_sample, cache_win_k, cache_win_v, state_ret, state_conv,
           g_mix, w_in, sinks, w_proj_a, w_proj_b, w_o, g_ffn, w_up, conv_w,
           conv_b, w_down, g_final):
    batch, seq, d_model = x_prompt.shape
    dec_batch, dec_seq, _ = x_sample.shape
    depth = w_in.shape[0]
    d_ff = w_down.shape[1]
    mp = batch * seq
    ms = dec_batch * dec_seq
    a_width = A_HEADS * A_HEAD_DIM
    kv_width = A_KV_HEADS * A_HEAD_DIM
    r_width = R_HEADS * R_DIM
    ret_col0 = a_width + 2 * kv_width
    ga_col = ret_col0 + 4 * r_width
    gb_col = ga_col + d_model
    assert dec_seq & (dec_seq - 1) == 0 and R_CHUNK % dec_seq == 0

    pos_p = jnp.arange(seq, dtype=F32)
    pos_s = PAST_LEN + jnp.arange(dec_seq, dtype=F32)
    rope_p = _rope_tables(pos_p)
    rope_s = tuple(jnp.tile(t, (SWA_SAMPLE_SEQS, 1)) for t in _rope_tables(pos_s))
    rrot_p = _retrot_tables(pos_p)
    nseq = R_CHUNK // dec_seq
    rrot_s = tuple(jnp.tile(t, (nseq, 1)) for t in _retrot_tables(pos_s))
    tabs_p = _decay_tables(R_CHUNK, 1)
    tabs_s = _decay_tables(dec_seq, nseq)
    g_mix3 = g_mix.reshape(depth, 1, d_model)
    g_ffn3 = g_ffn.reshape(depth, 1, d_model)
    conv_b3 = conv_b.reshape(depth, 1, d_ff)

    e_conv = jnp.repeat(jnp.swapaxes(state_conv, 1, 2), dec_seq, axis=2)
    v_col0 = a_width + kv_width

    kp_l, vp_l, sp_l, cp_l = [], [], [], []
    ks_l, vs_l, ss_l, cs_l = [], [], [], []
    for l in range(depth):
        if l == 0:
            x, h = _join_norm(x_prompt.reshape(mp, d_model), x_sample.reshape(ms, d_model),
                              g_mix3, l, name="join_norm_mix0")
        else:
            h = _rmsnorm(x, g_mix3, l, BF16, tm=TM_NORM, name=f"norm_mix{l}")
        z = _matmul(h, w_in, l, tm=TM, tn=TN_IN, out_dtype=BF16, name=f"proj_in{l}")

        oa, kr_p = _swa_prompt(z, sinks, l, *rope_p, batch=batch, seq=seq,
                               out_rows=mp + ms, name=f"swa_prompt{l}")
        oa, kr_s = _swa_sample(z, cache_win_k, cache_win_v, sinks, l, oa, *rope_s,
                               row0=mp, dec_batch=dec_batch, dec_seq=dec_seq,
                               name=f"swa_sample{l}")
        ob, s_p = _ret_prompt(z, *rrot_p, tabs_p, col0=ret_col0, batch=batch,
                              seq=seq, out_rows=mp + ms, name=f"ret_prompt{l}")
        ob, s_s = _ret_sample(z, state_ret, l, ob, *rrot_s, tabs_s, row0=mp,
                              col0=ret_col0, dec_batch=dec_batch, dec_seq=dec_seq,
                              name=f"ret_sample{l}")
        merged = _merge(oa, ob, z, w_proj_a, w_proj_b, l, ga_col=ga_col,
                        gb_col=gb_col, tm=TM, tn=TN_OUT, name=f"merge{l}")
        x = _matmul(merged, w_o, l, tm=TM, tn=TN_OUT, out_dtype=F32, residual=x,
                    name=f"proj_out{l}")

        h2 = _rmsnorm(x, g_ffn3, l, BF16, tm=TM_NORM, name=f"norm_ffn{l}")
        f, u_tail, u_s = _ffn_up(h2, w_up, conv_w, conv_b3, l, e_conv, tm=TM, tn=TN_UP,
                                 seq=seq, batch=batch, dec_seq=dec_seq, name=f"ffn_up{l}")
        x = _matmul(f, w_down, l, tm=TM_DOWN, tn=TN_OUT, out_dtype=F32, residual=x,
                    name=f"proj_down{l}")

        v_p = jnp.stack([lax.slice(z, ((b + 1) * seq - WINDOW, v_col0), ((b + 1) * seq, ret_col0))
                         for b in range(batch)])
        v_s = lax.slice(z, (mp, v_col0), (mp + ms, ret_col0))
        kp_l.append(kr_p.reshape(batch, WINDOW, A_KV_HEADS, A_HEAD_DIM))
        vp_l.append(v_p.astype(F32).reshape(batch, WINDOW, A_KV_HEADS, A_HEAD_DIM))
        sp_l.append(s_p)
        cp_l.append(u_tail.reshape(batch, SUBLANES, d_ff)[:, -(CONV_W - 1):])
        ks_l.append(kr_s.reshape(dec_batch, dec_seq, A_KV_HEADS, A_HEAD_DIM))
        vs_l.append(v_s.astype(F32).reshape(dec_batch, dec_seq, A_KV_HEADS, A_HEAD_DIM))
        ss_l.append(s_s)
        cs_l.append(u_s.reshape(dec_batch, dec_seq, d_ff)[:, -(CONV_W - 1):])

    y_prompt, y_sample = _split_norm(x, g_final, mp=mp, name="norm_final")
    y_prompt = y_prompt.reshape(batch, seq, d_model)
    y_sample = y_sample.reshape(dec_batch, dec_seq, d_model)
    return (y_prompt, y_sample,
            jnp.stack(kp_l), jnp.stack(vp_l), jnp.stack(sp_l), jnp.stack(cp_l),
            jnp.stack(ks_l), jnp.stack(vs_l), jnp.stack(ss_l), jnp.stack(cs_l))
```

```python
import functools

import jax
import jax.numpy as jnp
import numpy as np
from jax import lax
from jax.experimental import pallas as pl
from jax.experimental.pallas import tpu as pltpu

F32 = jnp.float32
BF16 = jnp.bfloat16

LANES = 128
SUBLANES = 8
MXU_COLS = 256
FIX_ROWS = 2 * SUBLANES
VMEM_LIMIT_CAP = 56 * 1024 * 1024

WINDOW = 128
A_HEADS = 16
A_KV_HEADS = 4
A_HEAD_DIM = 64
A_GROUP = A_HEADS // A_KV_HEADS
R_HEADS = 8
R_DIM = 128
R_CHUNK = 128
CONV_W = 3
EPS = 1e-6
ROPE_THETA = 10000.0
PAST_LEN = 16384

NT_DIMS = (((1,), (1,)), ((), ()))
TN_DIMS = (((0,), (0,)), ((), ()))


def _vmem_limit(nbytes):
    return int(min(VMEM_LIMIT_CAP, max(32 * 1024 * 1024, nbytes)))


def _params(semantics, nbytes):
    return pltpu.CompilerParams(dimension_semantics=semantics,
                                vmem_limit_bytes=_vmem_limit(nbytes))


def _rmsnorm_body(x_ref, g_ref, o_ref):
    x = x_ref[...]
    ms = jnp.mean(x * x, axis=-1, keepdims=True)
    o_ref[...] = (x * lax.rsqrt(ms + EPS) * g_ref[...]).astype(o_ref.dtype)


def _rmsnorm(x, g, layer, out_dtype, *, tm, name):
    m, d = x.shape
    return pl.pallas_call(
        _rmsnorm_body,
        grid=(m // tm,),
        in_specs=[pl.BlockSpec((tm, d), lambda i: (i, 0)),
                  pl.BlockSpec((None, 1, d), lambda i: (layer, 0, 0))],
        out_specs=pl.BlockSpec((tm, d), lambda i: (i, 0)),
        out_shape=jax.ShapeDtypeStruct((m, d), out_dtype),
        compiler_params=_params(("arbitrary",), 6 * tm * d * 4),
        name=name,
    )(x, g)


def _join_norm_body(xp_ref, xs_ref, g_ref, x_ref, h_ref, *, prompt_tiles):
    i = pl.program_id(0)

    def emit(x):
        ms = jnp.mean(x * x, axis=-1, keepdims=True)
        x_ref[...] = x
        h_ref[...] = (x * lax.rsqrt(ms + EPS) * g_ref[...]).astype(h_ref.dtype)

    @pl.when(i < prompt_tiles)
    def _():
        emit(xp_ref[...])

    @pl.when(i >= prompt_tiles)
    def _():
        emit(xs_ref[...])


def _join_norm(xp, xs, g, layer, *, name):
    mp, d = xp.shape
    ms = xs.shape[0]
    assert mp % ms == 0
    pt = mp // ms
    return pl.pallas_call(
        functools.partial(_join_norm_body, prompt_tiles=pt),
        grid=(pt + 1,),
        in_specs=[pl.BlockSpec((ms, d), lambda i: (jnp.minimum(i, pt - 1), 0)),
                  pl.BlockSpec((ms, d), lambda i: (0, 0)),
                  pl.BlockSpec((None, 1, d), lambda i: (layer, 0, 0))],
        out_specs=[pl.BlockSpec((ms, d), lambda i: (i, 0)),
                   pl.BlockSpec((ms, d), lambda i: (i, 0))],
        out_shape=[jax.ShapeDtypeStruct((mp + ms, d), F32),
                   jax.ShapeDtypeStruct((mp + ms, d), BF16)],
        compiler_params=_params(("arbitrary",), 0),
        name=name,
    )(xp, xs, g)


def _split_norm_body(x_ref, g_ref, yp_ref, ys_ref, *, prompt_tiles):
    i = pl.program_id(0)
    x = x_ref[...]
    ms = jnp.mean(x * x, axis=-1, keepdims=True)
    y = x * lax.rsqrt(ms + EPS) * g_ref[...]

    @pl.when(i < prompt_tiles)
    def _():
        yp_ref[...] = y

    @pl.when(i >= prompt_tiles)
    def _():
        ys_ref[...] = y


def _split_norm(x, g, *, mp, name):
    m, d = x.shape
    ms = m - mp
    assert mp % ms == 0
    pt = mp // ms
    return pl.pallas_call(
        functools.partial(_split_norm_body, prompt_tiles=pt),
        grid=(pt + 1,),
        in_specs=[pl.BlockSpec((ms, d), lambda i: (i, 0)),
                  pl.BlockSpec((1, d), lambda i: (0, 0))],
        out_specs=[pl.BlockSpec((ms, d), lambda i: (jnp.minimum(i, pt - 1), 0)),
                   pl.BlockSpec((ms, d), lambda i: (0, 0))],
        out_shape=[jax.ShapeDtypeStruct((mp, d), F32),
                   jax.ShapeDtypeStruct((ms, d), F32)],
        compiler_params=_params(("arbitrary",), 0),
        name=name,
    )(x, g.reshape(1, d))


def _mm_body(x_ref, w_ref, o_ref, wbf_ref):
    @pl.when(pl.program_id(1) == 0)
    def _():
        wbf_ref[...] = w_ref[...].astype(BF16)

    o_ref[...] = jnp.dot(x_ref[...], wbf_ref[...],
                         preferred_element_type=F32).astype(o_ref.dtype)


def _mm_res_body(x_ref, w_ref, r_ref, o_ref, wbf_ref):
    @pl.when(pl.program_id(1) == 0)
    def _():
        wbf_ref[...] = w_ref[...].astype(BF16)

    o_ref[...] = r_ref[...] + jnp.dot(x_ref[...], wbf_ref[...],
                                      preferred_element_type=F32)


def _matmul(x, w, layer, *, tm, tn, out_dtype, residual=None, name):
    m, k = x.shape
    n = w.shape[2]
    grid = (n // tn, m // tm)
    in_specs = [pl.BlockSpec((tm, k), lambda j, i: (i, 0)),
                pl.BlockSpec((None, k, tn), lambda j, i: (layer, 0, j))]
    args = [x, w]
    body = _mm_body
    if residual is not None:
        in_specs.append(pl.BlockSpec((tm, tn), lambda j, i: (i, j)))
        args.append(residual)
        body = _mm_res_body
    nbytes = (2 * tm * k * 2 + 2 * k * tn * 4 + k * tn * 2
              + 6 * tm * tn * 4 + (4 << 20))
    return pl.pallas_call(
        body,
        grid=grid,
        in_specs=in_specs,
        out_specs=pl.BlockSpec((tm, tn), lambda j, i: (i, j)),
        out_shape=jax.ShapeDtypeStruct((m, n), out_dtype),
        scratch_shapes=[pltpu.VMEM((k, tn), BF16)],
        compiler_params=_params(("arbitrary", "arbitrary"), nbytes),
        name=name,
    )(*args)


def _out_norm_body(m_ref, w_ref, r_ref, g_ref, x_ref, h_ref, wbf_ref):
    @pl.when(pl.program_id(0) == 0)
    def _():
        wbf_ref[...] = w_ref[...].astype(BF16)

    x = r_ref[...] + jnp.dot(m_ref[...], wbf_ref[...], preferred_element_type=F32)
    x_ref[...] = x
    ms = jnp.mean(x * x, axis=-1, keepdims=True)
    h_ref[...] = (x * lax.rsqrt(ms + EPS) * g_ref[...]).astype(h_ref.dtype)


def _matmul_res_norm(a, w, layer, residual, g, *, tm, name):
    m, k = a.shape
    n = w.shape[2]
    nbytes = (k * n * 4 + k * n * 2 + 2 * tm * k * 2 + 4 * tm * n * 4
              + 2 * tm * n * 2 + 2 * tm * n * 4 + (4 << 20))
    return pl.pallas_call(
        _out_norm_body,
        grid=(m // tm,),
        in_specs=[pl.BlockSpec((tm, k), lambda i: (i, 0)),
                  pl.BlockSpec((None, k, n), lambda i: (layer, 0, 0),
                               pipeline_mode=pl.Buffered(1)),
                  pl.BlockSpec((tm, n), lambda i: (i, 0)),
                  pl.BlockSpec((None, 1, n), lambda i: (layer, 0, 0))],
        out_specs=[pl.BlockSpec((tm, n), lambda i: (i, 0)),
                   pl.BlockSpec((tm, n), lambda i: (i, 0))],
        out_shape=[jax.ShapeDtypeStruct((m, n), F32),
                   jax.ShapeDtypeStruct((m, n), BF16)],
        scratch_shapes=[pltpu.VMEM((k, n), BF16)],
        compiler_params=_params(("arbitrary",), nbytes),
        name=name,
    )(a, w, residual, g)


def _merge_body(oa_ref, ob_ref, ga_ref, gb_ref, wa_ref, wb_ref, o_ref,
                wabf_ref, wbbf_ref):
    @pl.when(pl.program_id(1) == 0)
    def _():
        wabf_ref[...] = wa_ref[...].astype(BF16)
        wbbf_ref[...] = wb_ref[...].astype(BF16)

    a = jnp.dot(oa_ref[...], wabf_ref[...], preferred_element_type=F32)
    b = jnp.dot(ob_ref[...], wbbf_ref[...], preferred_element_type=F32)
    ga = jax.nn.sigmoid(ga_ref[...].astype(F32))
    gb = jax.nn.sigmoid(gb_ref[...].astype(F32))
    o_ref[...] = (ga * a + gb * b).astype(o_ref.dtype)


def _merge(oa, ob, z, wa, wb, layer, *, ga_col, gb_col, tm, tn, name):
    m, k = oa.shape
    n = wa.shape[2]
    assert ga_col % tn == 0 and gb_col % tn == 0
    ga_blk = ga_col // tn
    gb_blk = gb_col // tn
    nbytes = (4 * tm * k * 2 + 4 * k * tn * 4 + 2 * k * tn * 2
              + 4 * tm * tn * 2 + 8 * tm * tn * 4 + (4 << 20))
    return pl.pallas_call(
        _merge_body,
        grid=(n // tn, m // tm),
        in_specs=[pl.BlockSpec((tm, k), lambda j, i: (i, 0)),
                  pl.BlockSpec((tm, k), lambda j, i: (i, 0)),
                  pl.BlockSpec((tm, tn), lambda j, i: (i, ga_blk + j)),
                  pl.BlockSpec((tm, tn), lambda j, i: (i, gb_blk + j)),
                  pl.BlockSpec((None, k, tn), lambda j, i: (layer, 0, j)),
                  pl.BlockSpec((None, k, tn), lambda j, i: (layer, 0, j))],
        out_specs=pl.BlockSpec((tm, tn), lambda j, i: (i, j)),
        out_shape=jax.ShapeDtypeStruct((m, n), BF16),
        scratch_shapes=[pltpu.VMEM((k, tn), BF16), pltpu.VMEM((k, tn), BF16)],
        compiler_params=_params(("arbitrary", "arbitrary"), nbytes),
        name=name,
    )(oa, ob, z, z, wa, wb)


def _rope_tables(pos):
    half = A_HEAD_DIM // 2
    inv = ROPE_THETA ** (-2.0 * jnp.arange(half, dtype=F32) / A_HEAD_DIM)
    ang = pos[:, None] * inv[None, :]
    c, s = jnp.cos(ang), jnp.sin(ang)
    cos = jnp.concatenate([c, c, c, c], axis=-1)
    sin = jnp.concatenate([-s, s, -s, s], axis=-1)
    return cos, sin


def _retrot_tables(pos):
    half = R_DIM // 2
    inv = 1.0 / (10000.0 ** jnp.linspace(0.0, 1.0, half, dtype=F32))
    ang = pos[:, None] * inv[None, :]
    c, s = jnp.cos(ang), jnp.sin(ang)
    cos = jnp.stack([c, c], axis=-1).reshape(pos.shape[0], R_DIM)
    sin = jnp.stack([-s, s], axis=-1).reshape(pos.shape[0], R_DIM)
    return cos, sin


def _rope128(x, cos, sin):
    lane = lax.broadcasted_iota(jnp.int32, x.shape, 1)
    first = (lane & (A_HEAD_DIM - 1)) < (A_HEAD_DIM // 2)
    partner = jnp.where(first, pltpu.roll(x, LANES - 32, 1), pltpu.roll(x, 32, 1))
    return x * cos + partner * sin


def _retrot128(x, cos, sin):
    lane = lax.broadcasted_iota(jnp.int32, x.shape, 1)
    even = (lane & 1) == 0
    partner = jnp.where(even, pltpu.roll(x, LANES - 1, 1), pltpu.roll(x, 1, 1))
    return x * cos + partner * sin


def _swa_heads(q_chunks, score_fn, out_fn, sink_ref, layer):
    rows = q_chunks[0].shape[0]
    lane = lax.broadcasted_iota(jnp.int32, (rows, LANES), 1)
    low = lane < A_HEAD_DIM
    outs = []
    for pair in range(A_HEADS // 2):
        halves = []
        for hpos in range(2):
            h = 2 * pair + hpos
            kv = h // A_GROUP
            c, kpos = kv // 2, kv % 2
            qc = q_chunks[pair]
            if hpos != kpos:
                qc = pltpu.roll(qc, A_HEAD_DIM, 1)
            keep = low if kpos == 0 else jnp.logical_not(low)
            qm = jnp.where(keep, qc, 0.0).astype(BF16)
            s = score_fn(qm, c)
            sink = sink_ref[layer, h]
            m = jnp.maximum(jnp.max(s, axis=-1, keepdims=True), sink)
            p = jnp.exp(s - m)
            denom = jnp.sum(p, axis=-1, keepdims=True) + jnp.exp(sink - m)
            o = out_fn(p, c) / denom
            if hpos != kpos:
                o = pltpu.roll(o, A_HEAD_DIM, 1)
            halves.append(o)
        outs.append(jnp.where(low, halves[0], halves[1]))
    return outs


def _swa_prompt_body(sink_ref, q_ref, k_ref, vc_ref, vp_ref, cos_ref, sin_ref,
                     o_ref, kr_ref, kprev_ref, *, layer):
    n = pl.program_id(1)
    cos = cos_ref[...]
    sin = sin_ref[...]

    @pl.when(n == 0)
    def _():
        kprev_ref[...] = jnp.zeros_like(kprev_ref)

    k_cur, k_both, v_both = [], [], []
    for c in range(2):
        sl = slice(c * LANES, (c + 1) * LANES)
        kc = _rope128(k_ref[:, sl].astype(F32), cos, sin)
        kr_ref[:, sl] = kc
        k_cur.append(kc.astype(BF16))
        k_both.append(jnp.concatenate([kprev_ref[:, sl], k_cur[c]], axis=0))
        v_both.append(jnp.concatenate([vp_ref[:, sl], vc_ref[:, sl]], axis=0))

    scale = A_HEAD_DIM ** -0.5
    q_chunks = [
        _rope128(q_ref[:, c * LANES:(c + 1) * LANES].astype(F32), cos, sin) * scale
        for c in range(A_HEADS // 2)]

    qi = lax.broadcasted_iota(jnp.int32, (WINDOW, WINDOW), 0)
    kj = lax.broadcasted_iota(jnp.int32, (WINDOW, WINDOW), 1)
    from_cur = kj <= qi
    prev_bias = jnp.where(n > 0, 0.0, -jnp.inf)

    def scores(qm, c):
        s = lax.dot_general(qm, k_both[c], NT_DIMS, preferred_element_type=F32)
        return jnp.where(from_cur, s[:, WINDOW:], s[:, :WINDOW] + prev_bias)

    def weighted_values(p, c):
        p_both = jnp.concatenate([jnp.where(from_cur, 0.0, p), jnp.where(from_cur, p, 0.0)],
                                 axis=1).astype(BF16)
        return jnp.dot(p_both, v_both[c], preferred_element_type=F32)

    outs = _swa_heads(q_chunks, scores, weighted_values, sink_ref, layer)
    for c, o in enumerate(outs):
        o_ref[:, c * LANES:(c + 1) * LANES] = o.astype(o_ref.dtype)
    for c in range(2):
        kprev_ref[:, c * LANES:(c + 1) * LANES] = k_cur[c]


def _swa_prompt(z, sinks, layer, cos, sin, *, batch, seq, out_rows, name):
    nb = seq // WINDOW
    a_width = A_HEADS * A_HEAD_DIM
    kv_width = A_KV_HEADS * A_HEAD_DIM
    k_blk = a_width // kv_width
    v_blk = k_blk + 1
    rows = out_rows
    return pl.pallas_call(
        functools.partial(_swa_prompt_body, layer=layer),
        grid=(batch, nb),
        in_specs=[
            pl.BlockSpec(memory_space=pltpu.SMEM),
            pl.BlockSpec((WINDOW, a_width), lambda b, n: (b * nb + n, 0)),
            pl.BlockSpec((WINDOW, kv_width), lambda b, n: (b * nb + n, k_blk)),
            pl.BlockSpec((WINDOW, kv_width), lambda b, n: (b * nb + n, v_blk)),
            pl.BlockSpec((WINDOW, kv_width),
                         lambda b, n: (b * nb + jnp.maximum(n - 1, 0), v_blk)),
            pl.BlockSpec((WINDOW, LANES), lambda b, n: (n, 0)),
            pl.BlockSpec((WINDOW, LANES), lambda b, n: (n, 0)),
        ],
        out_specs=[
            pl.BlockSpec((WINDOW, a_width), lambda b, n: (b * nb + n, 0)),
            pl.BlockSpec((WINDOW, kv_width), lambda b, n: (b, 0)),
        ],
        out_shape=[jax.ShapeDtypeStruct((rows, a_width), BF16),
                   jax.ShapeDtypeStruct((batch * WINDOW, kv_width), F32)],
        scratch_shapes=[pltpu.VMEM((WINDOW, kv_width), BF16)],
        compiler_params=_params(("arbitrary", "arbitrary"), 0),
        name=name,
    )(sinks, z, z, z, z, cos, sin)


SWA_SAMPLE_SEQS = 4


def _swa_sample_body(sink_ref, q_ref, k_ref, v_ref, ck_ref, cv_ref, cos_ref,
                     sin_ref, oa_hbm_ref, o_ref, kr_ref, *, dec_seq, layer):
    del oa_hbm_ref
    cos = cos_ref[...]
    sin = sin_ref[...]
    g = SWA_SAMPLE_SEQS
    rows = g * dec_seq
    ncache = g * WINDOW
    k_chunks, v_chunks = [], []
    for c in range(2):
        sl = slice(c * LANES, (c + 1) * LANES)
        kc = _rope128(k_ref[:, sl].astype(F32), cos, sin)
        kr_ref[:, sl] = kc
        k_chunks.append(jnp.concatenate(
            [ck_ref[:, sl].astype(BF16), kc.astype(BF16)], axis=0))
        v_chunks.append(jnp.concatenate(
            [cv_ref[:, sl].astype(BF16), v_ref[:, sl]], axis=0))

    scale = A_HEAD_DIM ** -0.5
    q_chunks = [
        _rope128(q_ref[:, c * LANES:(c + 1) * LANES].astype(F32), cos, sin) * scale
        for c in range(A_HEADS // 2)]

    shape = (rows, ncache + rows)
    r = lax.broadcasted_iota(jnp.int32, shape, 0)
    col = lax.broadcasted_iota(jnp.int32, shape, 1)
    log_l = dec_seq.bit_length() - 1
    log_w = WINDOW.bit_length() - 1
    seq_q = r >> log_l
    i = r & (dec_seq - 1)
    in_cache = col < ncache
    cnew = col - ncache
    seq_k = jnp.where(in_cache, col >> log_w, cnew >> log_l)
    ok = ((in_cache & ((col & (WINDOW - 1)) > i))
          | (jnp.logical_not(in_cache) & ((cnew & (dec_seq - 1)) <= i)))
    valid = (seq_k == seq_q) & ok

    def scores(qm, c):
        s = lax.dot_general(qm, k_chunks[c], NT_DIMS, preferred_element_type=F32)
        return jnp.where(valid, s, -jnp.inf)

    def weighted_values(p, c):
        return jnp.dot(p.astype(BF16), v_chunks[c], preferred_element_type=F32)

    outs = _swa_heads(q_chunks, scores, weighted_values, sink_ref, layer)
    for c, o in enumerate(outs):
        o_ref[:, c * LANES:(c + 1) * LANES] = o.astype(o_ref.dtype)


def _swa_sample(z, cache_k, cache_v, sinks, layer, oa, cos, sin, *, row0, dec_batch, dec_seq,
                name):
    g = SWA_SAMPLE_SEQS
    rows = g * dec_seq
    a_width = A_HEADS * A_HEAD_DIM
    kv_width = A_KV_HEADS * A_HEAD_DIM
    k_blk = a_width // kv_width
    v_blk = k_blk + 1
    r0 = row0 // rows
    depth = cache_k.shape[0]
    ck = cache_k.reshape(depth, dec_batch * WINDOW, kv_width)
    cv = cache_v.reshape(depth, dec_batch * WINDOW, kv_width)
    return pl.pallas_call(
        functools.partial(_swa_sample_body, dec_seq=dec_seq, layer=layer),
        grid=(dec_batch // g,),
        in_specs=[
            pl.BlockSpec(memory_space=pltpu.SMEM),
            pl.BlockSpec((rows, a_width), lambda s: (r0 + s, 0)),
            pl.BlockSpec((rows, kv_width), lambda s: (r0 + s, k_blk)),
            pl.BlockSpec((rows, kv_width), lambda s: (r0 + s, v_blk)),
            pl.BlockSpec((None, g * WINDOW, kv_width), lambda s: (layer, s, 0)),
            pl.BlockSpec((None, g * WINDOW, kv_width), lambda s: (layer, s, 0)),
            pl.BlockSpec((rows, LANES), lambda s: (0, 0)),
            pl.BlockSpec((rows, LANES), lambda s: (0, 0)),
            pl.BlockSpec(memory_space=pl.ANY),
        ],
        out_specs=[
            pl.BlockSpec((rows, a_width), lambda s: (r0 + s, 0)),
            pl.BlockSpec((rows, kv_width), lambda s: (s, 0)),
        ],
        out_shape=[jax.ShapeDtypeStruct(oa.shape, oa.dtype),
                   jax.ShapeDtypeStruct((dec_batch * dec_seq, kv_width), F32)],
        input_output_aliases={8: 0},
        compiler_params=_params(("arbitrary",), 0),
        name=name,
    )(sinks, z, z, z, ck, cv, cos, sin, oa)


RET_HEADS_PER_STEP = 4
RET_BLOCK = RET_HEADS_PER_STEP * R_DIM


def _decay_tables(length, nseq):
    log_g = np.log1p(-np.exp2(-5.0 - np.arange(R_HEADS, dtype=np.float64)))
    i = np.arange(length, dtype=np.float64)
    diff = i[:, None] - i[None, :]
    d = np.where(diff >= 0, np.exp(log_g[:, None, None] * np.maximum(diff, 0.0)), 0.0)
    d = np.einsum('st,hij->hsitj', np.eye(nseq), d).reshape(
        R_HEADS, nseq * length, nseq * length)
    q_dec = np.exp(log_g[:, None] * (i[None, :] + 1.0))
    k_dec = np.exp(log_g[:, None] * (length - 1.0 - i)[None, :])
    shape = (R_HEADS, nseq * length, R_DIM)
    qd = np.broadcast_to(np.tile(q_dec, (1, nseq))[:, :, None], shape)
    kd = np.broadcast_to(np.tile(k_dec, (1, nseq))[:, :, None], shape)
    g_len = np.exp(log_g * length)
    return tuple(jnp.asarray(t, F32) for t in (d, qd, kd, g_len))


def _group_norm_gate(o, gate):
    o = o * lax.rsqrt(jnp.mean(o * o, axis=-1, keepdims=True) + EPS)
    return o * (gate * jax.nn.sigmoid(gate))


def _ret_prompt_body(gl_ref, *refs):
    nb = R_HEADS // RET_HEADS_PER_STEP
    q_refs, k_refs, v_refs, g_refs = (refs[t * nb:(t + 1) * nb] for t in range(4))
    cos_ref, sin_ref, d_ref, qd_ref, kd_ref, o_ref, s_ref = refs[4 * nb:]
    c = pl.program_id(1)
    cos = cos_ref[...]
    sin = sin_ref[...]

    @pl.when(c == 0)
    def _():
        s_ref[...] = jnp.zeros_like(s_ref)

    scale = R_DIM ** -0.5
    for h in range(R_HEADS):
        blk = h // RET_HEADS_PER_STEP
        sl = slice((h % RET_HEADS_PER_STEP) * R_DIM, (h % RET_HEADS_PER_STEP + 1) * R_DIM)
        q = _retrot128(q_refs[blk][:, sl].astype(F32), cos, sin)
        k = _retrot128(k_refs[blk][:, sl].astype(F32), cos, sin) * scale
        v = v_refs[blk][:, sl]
        qb = q.astype(BF16)
        state = s_ref[0, h]
        scores = lax.dot_general(qb, k.astype(BF16), NT_DIMS,
                                 preferred_element_type=F32) * d_ref[h]
        o = jnp.dot(scores.astype(BF16), v, preferred_element_type=F32)
        cross = jnp.dot(qb, state.astype(BF16), preferred_element_type=F32)
        o = o + cross * qd_ref[h]
        kd = (k * kd_ref[h]).astype(BF16)
        s_ref[0, h] = gl_ref[h] * state + lax.dot_general(
            kd, v, TN_DIMS, preferred_element_type=F32)
        o_ref[:, h * R_DIM:(h + 1) * R_DIM] = _group_norm_gate(
            o, g_refs[blk][:, sl].astype(F32)).astype(o_ref.dtype)


def _ret_prompt(z, cos, sin, tabs, *, col0, batch, seq, out_rows, name):
    d, qd, kd, g_len = tabs
    nc = seq // R_CHUNK
    nb = R_HEADS // RET_HEADS_PER_STEP
    width = R_HEADS * R_DIM
    blks = [(col0 + t * width) // RET_BLOCK + i for t in range(4) for i in range(nb)]

    def zspec(blk):
        return pl.BlockSpec((R_CHUNK, RET_BLOCK), lambda b, c: (b * nc + c, blk))

    def tspec():
        return pl.BlockSpec((R_HEADS, R_CHUNK, R_DIM), lambda b, c: (0, 0, 0))

    return pl.pallas_call(
        _ret_prompt_body,
        grid=(batch, nc),
        in_specs=[pl.BlockSpec(memory_space=pltpu.SMEM)]
        + [zspec(blk) for blk in blks]
        + [pl.BlockSpec((R_CHUNK, R_DIM), lambda b, c: (c, 0)),
           pl.BlockSpec((R_CHUNK, R_DIM), lambda b, c: (c, 0)),
           tspec(), tspec(), tspec()],
        out_specs=[
            pl.BlockSpec((R_CHUNK, width), lambda b, c: (b * nc + c, 0)),
            pl.BlockSpec((1, R_HEADS, R_DIM, R_DIM), lambda b, c: (b, 0, 0, 0)),
        ],
        out_shape=[jax.ShapeDtypeStruct((out_rows, width), BF16),
                   jax.ShapeDtypeStruct((batch, R_HEADS, R_DIM, R_DIM), F32)],
        compiler_params=_params(("arbitrary", "arbitrary"), 0),
        name=name,
    )(g_len, *([z] * len(blks)), cos, sin, d, qd, kd)


def _ret_sample_body(gl_ref, q_ref, k_ref, v_ref, g_ref, cos_ref, sin_ref,
                     d_ref, qd_ref, kd_ref, s_ref, ob_hbm_ref, o_ref, so_ref, *, dec_seq):
    del ob_hbm_ref
    hh = pl.program_id(1)
    cos = cos_ref[...]
    sin = sin_ref[...]
    nseq = R_CHUNK // dec_seq
    log_l = dec_seq.bit_length() - 1
    row_seq = lax.broadcasted_iota(jnp.int32, (R_CHUNK, R_DIM), 0) >> log_l
    scale = R_DIM ** -0.5
    for hl in range(RET_HEADS_PER_STEP):
        sl = slice(hl * R_DIM, (hl + 1) * R_DIM)
        q = _retrot128(q_ref[:, sl].astype(F32), cos, sin)
        k = _retrot128(k_ref[:, sl].astype(F32), cos, sin) * scale
        v = v_ref[:, sl]
        qb = q.astype(BF16)
        scores = lax.dot_general(qb, k.astype(BF16), NT_DIMS,
                                 preferred_element_type=F32) * d_ref[hl]
        o = jnp.dot(scores.astype(BF16), v, preferred_element_type=F32)
        kd = k * kd_ref[hl]
        gl = gl_ref[hh * RET_HEADS_PER_STEP + hl]
        cross = jnp.zeros((R_CHUNK, R_DIM), F32)
        for s in range(nseq):
            mine = row_seq == s
            state = s_ref[s, hl]
            cs = jnp.dot(qb, state.astype(BF16), preferred_element_type=F32)
            cross = jnp.where(mine, cs, cross)
            ks = jnp.where(mine, kd, 0.0).astype(BF16)
            so_ref[s, hl] = gl * state + lax.dot_general(
                ks, v, TN_DIMS, preferred_element_type=F32)
        o = o + cross * qd_ref[hl]
        o_ref[:, sl] = _group_norm_gate(o, g_ref[:, sl].astype(F32)).astype(o_ref.dtype)


def _ret_sample(z, state, layer, ob, cos, sin, tabs, *, row0, col0, dec_batch, dec_seq, name):
    d, qd, kd, g_len = tabs
    nseq = R_CHUNK // dec_seq
    steps = dec_batch // nseq
    nh = R_HEADS // RET_HEADS_PER_STEP
    width = R_HEADS * R_DIM
    qb, kb, vb, gb = [(col0 + t * width) // RET_BLOCK for t in range(4)]
    r0 = row0 // R_CHUNK
    hp = RET_HEADS_PER_STEP

    def zspec(blk):
        return pl.BlockSpec((R_CHUNK, RET_BLOCK), lambda s, h: (r0 + s, blk + h))

    def tspec():
        return pl.BlockSpec((hp, R_CHUNK, R_DIM), lambda s, h: (h, 0, 0))

    sspec_in = pl.BlockSpec((None, nseq, hp, R_DIM, R_DIM), lambda s, h: (layer, s, h, 0, 0))
    sspec = pl.BlockSpec((nseq, hp, R_DIM, R_DIM), lambda s, h: (s, h, 0, 0))
    return pl.pallas_call(
        functools.partial(_ret_sample_body, dec_seq=dec_seq),
        grid=(steps, nh),
        in_specs=[
            pl.BlockSpec(memory_space=pltpu.SMEM),
            zspec(qb), zspec(kb), zspec(vb), zspec(gb),
            pl.BlockSpec((R_CHUNK, R_DIM), lambda s, h: (0, 0)),
            pl.BlockSpec((R_CHUNK, R_DIM), lambda s, h: (0, 0)),
            tspec(), tspec(), tspec(),
            sspec_in,
            pl.BlockSpec(memory_space=pl.ANY),
        ],
        out_specs=[
            pl.BlockSpec((R_CHUNK, RET_BLOCK), lambda s, h: (r0 + s, h)),
            sspec,
        ],
        out_shape=[jax.ShapeDtypeStruct(ob.shape, ob.dtype),
                   jax.ShapeDtypeStruct(state.shape[1:], F32)],
        input_output_aliases={11: 0},
        compiler_params=_params(("arbitrary", "arbitrary"),
                                4 * nseq * hp * R_DIM * R_DIM * 4 + (16 << 20)),
        name=name,
    )(g_len, z, z, z, z, cos, sin, d, qd, kd, state, ob)


def _gelu_tanh(x):
    c = (2.0 / jnp.pi) ** 0.5
    return 0.5 * x * (1.0 + jnp.tanh(c * (x + 0.044715 * (x * x * x))))


def _ffn_up_body(h_ref, wu_ref, wg_ref, cw_ref, cb_ref, e0_ref, e1_ref,
                 f_ref, tail_ref, us_ref, wubf_ref, wgbf_ref, carry_ref, ug_ref,
                 *, tm, n_tiles, seq, batch, ms, dec_seq):
    i = pl.program_id(1)
    tn = f_ref.shape[1]
    mp = batch * seq

    @pl.when(i == 0)
    def _():
        wubf_ref[...] = wu_ref[...].astype(BF16)
        wgbf_ref[...] = wg_ref[...].astype(BF16)
        carry_ref[...] = jnp.zeros_like(carry_ref)

    last = n_tiles - 1
    s0 = mp - last * tm
    assert 0 <= s0 and s0 + ms == tm and s0 % FIX_ROWS == 0
    starts = [k * seq for k in range(1, batch)]
    for r in starts:
        assert r % tm != 0 and r % FIX_ROWS == 0, "mid-tile, packed-tile aligned starts only"

    def conv_gate(u, gate, p1, p2, w, bias):
        conv = bias + w[0:1, :] * p2
        conv = conv + w[1:2, :] * p1
        conv = conv + w[2:3, :] * u
        return (_gelu_tanh(conv) * gate).astype(f_ref.dtype)

    tr = tm // FFN_ROW_PARTS
    assert tm % FFN_ROW_PARTS == 0 and tr % FIX_ROWS == 0
    for c in range(tn // MXU_COLS):
        cs = slice(c * MXU_COLS, (c + 1) * MXU_COLS)
        w = cw_ref[:, cs]
        bias = cb_ref[:, cs]
        for part in range(FFN_ROW_PARTS):
            rs = slice(part * tr, (part + 1) * tr)
            h = h_ref[rs, :]
            ug_ref[2 * c, rs, :] = jnp.dot(h, wubf_ref[:, cs], preferred_element_type=F32)
            ug_ref[2 * c + 1, rs, :] = jnp.dot(h, wgbf_ref[:, cs], preferred_element_type=F32)
            u = ug_ref[2 * c, rs, :]
            gate = ug_ref[2 * c + 1, rs, :]

            if part == 0:
                before = carry_ref[:, cs]
            else:
                before = ug_ref[2 * c, part * tr - SUBLANES:part * tr, :]
            last1 = before[SUBLANES - 1:SUBLANES, :]
            last2 = before[SUBLANES - 2:SUBLANES - 1, :]
            row = lax.broadcasted_iota(jnp.int32, u.shape, 0)
            prev1 = jnp.where(row == 0, last1, pltpu.roll(u, 1, 0))
            prev2 = jnp.where(row == 0, last2,
                              jnp.where(row == 1, last1, pltpu.roll(u, 2, 0)))
            f_ref[rs, cs] = conv_gate(u, gate, prev1, prev2, w, bias)
        carry_ref[:, cs] = ug_ref[2 * c, tm - SUBLANES:, :]

    def rewrite(rows, p1_fn, p2_fn):
        for c in range(tn // MXU_COLS):
            cs = slice(c * MXU_COLS, (c + 1) * MXU_COLS)
            us = ug_ref[2 * c, rows, :]
            f_ref[rows, cs] = conv_gate(us, ug_ref[2 * c + 1, rows, :], p1_fn(us, cs),
                                        p2_fn(us, cs), cw_ref[:, cs], cb_ref[:, cs])

    for r in starts:
        @pl.when(i == r // tm)
        def _(off=r % tm):
            pos = lax.broadcasted_iota(jnp.int32, (FIX_ROWS, MXU_COLS), 0)
            rewrite(slice(off, off + FIX_ROWS),
                    lambda us, cs: jnp.where(pos == 0, 0.0, pltpu.roll(us, 1, 0)),
                    lambda us, cs: jnp.where(pos <= 1, 0.0, pltpu.roll(us, 2, 0)))

    @pl.when(i == last)
    def _():
        pos = lax.broadcasted_iota(jnp.int32, (ms, MXU_COLS), 0) & (dec_seq - 1)
        rewrite(slice(s0, tm),
                lambda us, cs: jnp.where(pos == 0, e1_ref[:, cs], pltpu.roll(us, 1, 0)),
                lambda us, cs: jnp.where(
                    pos == 0, e0_ref[:, cs],
                    jnp.where(pos == 1, e1_ref[:, cs], pltpu.roll(us, 2, 0))))
        for c in range(tn // MXU_COLS):
            us_ref[:, c * MXU_COLS:(c + 1) * MXU_COLS] = ug_ref[2 * c, s0:, :]

    for b in range(batch):
        r = (b + 1) * seq - SUBLANES

        @pl.when(i == r // tm)
        def _(b=b, off=r % tm):
            for c in range(tn // MXU_COLS):
                tail_ref[b * SUBLANES:(b + 1) * SUBLANES, c * MXU_COLS:(c + 1) * MXU_COLS] = (
                    ug_ref[2 * c, off:off + SUBLANES, :])


def _ffn_up(h, w_up, conv_w, conv_b, layer, e, *, tm, tn, seq, batch, dec_seq, name):
    m, k = h.shape
    d_ff = w_up.shape[2] // 2
    ms = e.shape[2]
    nj = d_ff // tn
    nt = m // tm
    body = functools.partial(_ffn_up_body, tm=tm, n_tiles=nt, seq=seq, batch=batch,
                             ms=ms, dec_seq=dec_seq)
    nbytes = (2 * tm * k * 2 + 4 * k * tn * 4 + 2 * k * tn * 2
              + 16 * tm * tn * 4 + 4 * ms * tn * 4 + (4 << 20))
    return pl.pallas_call(
        body,
        grid=(nj, nt),
        in_specs=[
            pl.BlockSpec((tm, k), lambda j, i: (i, 0)),
            pl.BlockSpec((None, k, tn), lambda j, i: (layer, 0, j)),
            pl.BlockSpec((None, k, tn), lambda j, i: (layer, 0, nj + j)),
            pl.BlockSpec((None, CONV_W, tn), lambda j, i: (layer, 0, j)),
            pl.BlockSpec((None, 1, tn), lambda j, i: (layer, 0, j)),
            pl.BlockSpec((None, None, ms, tn), lambda j, i: (layer, 0, 0, j)),
            pl.BlockSpec((None, None, ms, tn), lambda j, i: (layer, 1, 0, j)),
        ],
        out_specs=[
            pl.BlockSpec((tm, tn), lambda j, i: (i, j)),
            pl.BlockSpec((batch * SUBLANES, tn), lambda j, i: (0, j)),
            pl.BlockSpec((ms, tn), lambda j, i: (0, j)),
        ],
        out_shape=[jax.ShapeDtypeStruct((m, d_ff), BF16),
                   jax.ShapeDtypeStruct((batch * SUBLANES, d_ff), F32),
                   jax.ShapeDtypeStruct((ms, d_ff), F32)],
        scratch_shapes=[pltpu.VMEM((k, tn), BF16), pltpu.VMEM((k, tn), BF16),
                        pltpu.VMEM((SUBLANES, tn), F32),
                        pltpu.VMEM((2 * tn // MXU_COLS, tm, MXU_COLS), F32)],
        compiler_params=_params(("arbitrary", "arbitrary"), nbytes),
        name=name,
    )(h, w_up, w_up, conv_w, conv_b, e, e)


TM = 1408
TM_NORM = 768
TM_DOWN = 704
TN_IN = 512
TN_UP = 512
TM_MERGE = 1408
TN_MERGE = 512
TM_OUT = 352
FFN_ROW_PARTS = 2
TN_OUT = 512


def kernel(x_prompt, x_sample, cache_win_k, cache_win_v, state_ret, state_conv,
           g_mix, w_in, sinks, w_proj_a, w_proj_b, w_o, g_ffn, w_up, conv_w,
           conv_b, w_down, g_final):
    batch, seq, d_model = x_prompt.shape
    dec_batch, dec_seq, _ = x_sample.shape
    depth = w_in.shape[0]
    d_ff = w_down.shape[1]
    mp = batch * seq
    ms = dec_batch * dec_seq
    a_width = A_HEADS * A_HEAD_DIM
    kv_width = A_KV_HEADS * A_HEAD_DIM
    r_width = R_HEADS * R_DIM
    ret_col0 = a_width + 2 * kv_width
    ga_col = ret_col0 + 4 * r_width
    gb_col = ga_col + d_model
    assert dec_seq & (dec_seq - 1) == 0 and R_CHUNK % dec_seq == 0

    pos_p = jnp.arange(seq, dtype=F32)
    pos_s = PAST_LEN + jnp.arange(dec_seq, dtype=F32)
    rope_p = _rope_tables(pos_p)
    rope_s = tuple(jnp.tile(t, (SWA_SAMPLE_SEQS, 1)) for t in _rope_tables(pos_s))
    rrot_p = _retrot_tables(pos_p)
    nseq = R_CHUNK // dec_seq
    rrot_s = tuple(jnp.tile(t, (nseq, 1)) for t in _retrot_tables(pos_s))
    tabs_p = _decay_tables(R_CHUNK, 1)
    tabs_s = _decay_tables(dec_seq, nseq)
    g_mix3 = g_mix.reshape(depth, 1, d_model)
    g_ffn3 = g_ffn.reshape(depth, 1, d_model)
    conv_b3 = conv_b.reshape(depth, 1, d_ff)

    e_conv = jnp.repeat(jnp.swapaxes(state_conv, 1, 2), dec_seq, axis=2)
    v_col0 = a_width + kv_width

    kp_l, vp_l, sp_l, cp_l = [], [], [], []
    ks_l, vs_l, ss_l, cs_l = [], [], [], []
    for l in range(depth):
        if l == 0:
            x, h = _join_norm(x_prompt.reshape(mp, d_model), x_sample.reshape(ms, d_model),
                              g_mix3, l, name="join_norm_mix0")
        else:
            h = _rmsnorm(x, g_mix3, l, BF16, tm=TM_NORM, name=f"norm_mix{l}")
        z = _matmul(h, w_in, l, tm=TM, tn=TN_IN, out_dtype=BF16, name=f"proj_in{l}")

        oa, kr_p = _swa_prompt(z, sinks, l, *rope_p, batch=batch, seq=seq,
                               out_rows=mp + ms, name=f"swa_prompt{l}")
        oa, kr_s = _swa_sample(z, cache_win_k, cache_win_v, sinks, l, oa, *rope_s,
                               row0=mp, dec_batch=dec_batch, dec_seq=dec_seq,
                               name=f"swa_sample{l}")
        ob, s_p = _ret_prompt(z, *rrot_p, tabs_p, col0=ret_col0, batch=batch,
                              seq=seq, out_rows=mp + ms, name=f"ret_prompt{l}")
        ob, s_s = _ret_sample(z, state_ret, l, ob, *rrot_s, tabs_s, row0=mp,
                              col0=ret_col0, dec_batch=dec_batch, dec_seq=dec_seq,
                              name=f"ret_sample{l}")
        merged = _merge(oa, ob, z, w_proj_a, w_proj_b, l, ga_col=ga_col,
                        gb_col=gb_col, tm=TM_MERGE, tn=TN_MERGE, name=f"merge{l}")
        x, h2 = _matmul_res_norm(merged, w_o, l, x, g_ffn3, tm=TM_OUT,
                                 name=f"proj_out_norm{l}")
        f, u_tail, u_s = _ffn_up(h2, w_up, conv_w, conv_b3, l, e_conv, tm=TM, tn=TN_UP,
                                 seq=seq, batch=batch, dec_seq=dec_seq, name=f"ffn_up{l}")
        x = _matmul(f, w_down, l, tm=TM_DOWN, tn=TN_OUT, out_dtype=F32, residual=x,
                    name=f"proj_down{l}")

        v_p = jnp.stack([lax.slice(z, ((b + 1) * seq - WINDOW, v_col0), ((b + 1) * seq, ret_col0))
                         for b in range(batch)])
        v_s = lax.slice(z, (mp, v_col0), (mp + ms, ret_col0))
        kp_l.append(kr_p.reshape(batch, WINDOW, A_KV_HEADS, A_HEAD_DIM))
        vp_l.append(v_p.astype(F32).reshape(batch, WINDOW, A_KV_HEADS, A_HEAD_DIM))
        sp_l.append(s_p)
        cp_l.append(u_tail.reshape(batch, SUBLANES, d_ff)[:, -(CONV_W - 1):])
        ks_l.append(kr_s.reshape(dec_batch, dec_seq, A_KV_HEADS, A_HEAD_DIM))
        vs_l.append(v_s.astype(F32).reshape(dec_batch, dec_seq, A_KV_HEADS, A_HEAD_DIM))
        ss_l.append(s_s)
        cs_l.append(u_s.reshape(dec_batch, dec_seq, d_ff)[:, -(CONV_W - 1):])

    y_prompt, y_sample = _split_norm(x, g_final, mp=mp, name="norm_final")
    y_prompt = y_prompt.reshape(batch, seq, d_model)
    y_sample = y_sample.reshape(dec_batch, dec_seq, d_model)
    return (y_prompt, y_sample,
            jnp.stack(kp_l), jnp.stack(vp_l), jnp.stack(sp_l), jnp.stack(cp_l),
            jnp.stack(ks_l), jnp.stack(vs_l), jnp.stack(ss_l), jnp.stack(cs_l))
```

```python
import functools

import jax
import jax.numpy as jnp
import numpy as np
from jax import lax
from jax.experimental import pallas as pl
from jax.experimental.pallas import tpu as pltpu

F32 = jnp.float32
BF16 = jnp.bfloat16

LANES = 128
SUBLANES = 8
MXU_COLS = 256
FIX_ROWS = 2 * SUBLANES
VMEM_LIMIT_CAP = 56 * 1024 * 1024

WINDOW = 128
A_HEADS = 16
A_KV_HEADS = 4
A_HEAD_DIM = 64
A_GROUP = A_HEADS // A_KV_HEADS
R_HEADS = 8
R_DIM = 128
R_CHUNK = 128
CONV_W = 3
EPS = 1e-6
ROPE_THETA = 10000.0
PAST_LEN = 16384

NT_DIMS = (((1,), (1,)), ((), ()))
TN_DIMS = (((0,), (0,)), ((), ()))


def _vmem_limit(nbytes):
    return int(min(VMEM_LIMIT_CAP, max(32 * 1024 * 1024, nbytes)))


def _params(semantics, nbytes):
    return pltpu.CompilerParams(dimension_semantics=semantics,
                                vmem_limit_bytes=_vmem_limit(nbytes))


def _rmsnorm_body(x_ref, g_ref, o_ref):
    x = x_ref[...]
    ms = jnp.mean(x * x, axis=-1, keepdims=True)
    o_ref[...] = (x * lax.rsqrt(ms + EPS) * g_ref[...]).astype(o_ref.dtype)


def _rmsnorm(x, g, layer, out_dtype, *, tm, name):
    m, d = x.shape
    return pl.pallas_call(
        _rmsnorm_body,
        grid=(m // tm,),
        in_specs=[pl.BlockSpec((tm, d), lambda i: (i, 0)),
                  pl.BlockSpec((None, 1, d), lambda i: (layer, 0, 0))],
        out_specs=pl.BlockSpec((tm, d), lambda i: (i, 0)),
        out_shape=jax.ShapeDtypeStruct((m, d), out_dtype),
        compiler_params=_params(("arbitrary",), 6 * tm * d * 4),
        name=name,
    )(x, g)


def _join_norm_body(xp_ref, xs_ref, g_ref, x_ref, h_ref, *, prompt_tiles):
    i = pl.program_id(0)

    def emit(x):
        ms = jnp.mean(x * x, axis=-1, keepdims=True)
        x_ref[...] = x
        h_ref[...] = (x * lax.rsqrt(ms + EPS) * g_ref[...]).astype(h_ref.dtype)

    @pl.when(i < prompt_tiles)
    def _():
        emit(xp_ref[...])

    @pl.when(i >= prompt_tiles)
    def _():
        emit(xs_ref[...])


def _join_norm(xp, xs, g, layer, *, name):
    mp, d = xp.shape
    ms = xs.shape[0]
    assert mp % ms == 0
    pt = mp // ms
    return pl.pallas_call(
        functools.partial(_join_norm_body, prompt_tiles=pt),
        grid=(pt + 1,),
        in_specs=[pl.BlockSpec((ms, d), lambda i: (jnp.minimum(i, pt - 1), 0)),
                  pl.BlockSpec((ms, d), lambda i: (0, 0)),
                  pl.BlockSpec((None, 1, d), lambda i: (layer, 0, 0))],
        out_specs=[pl.BlockSpec((ms, d), lambda i: (i, 0)),
                   pl.BlockSpec((ms, d), lambda i: (i, 0))],
        out_shape=[jax.ShapeDtypeStruct((mp + ms, d), F32),
                   jax.ShapeDtypeStruct((mp + ms, d), BF16)],
        compiler_params=_params(("arbitrary",), 0),
        name=name,
    )(xp, xs, g)


def _split_norm_body(x_ref, g_ref, yp_ref, ys_ref, *, prompt_tiles):
    i = pl.program_id(0)
    x = x_ref[...]
    ms = jnp.mean(x * x, axis=-1, keepdims=True)
    y = x * lax.rsqrt(ms + EPS) * g_ref[...]

    @pl.when(i < prompt_tiles)
    def _():
        yp_ref[...] = y

    @pl.when(i >= prompt_tiles)
    def _():
        ys_ref[...] = y


def _split_norm(x, g, *, mp, name):
    m, d = x.shape
    ms = m - mp
    assert mp % ms == 0
    pt = mp // ms
    return pl.pallas_call(
        functools.partial(_split_norm_body, prompt_tiles=pt),
        grid=(pt + 1,),
        in_specs=[pl.BlockSpec((ms, d), lambda i: (i, 0)),
                  pl.BlockSpec((1, d), lambda i: (0, 0))],
        out_specs=[pl.BlockSpec((ms, d), lambda i: (jnp.minimum(i, pt - 1), 0)),
                   pl.BlockSpec((ms, d), lambda i: (0, 0))],
        out_shape=[jax.ShapeDtypeStruct((mp, d), F32),
                   jax.ShapeDtypeStruct((ms, d), F32)],
        compiler_params=_params(("arbitrary",), 0),
        name=name,
    )(x, g.reshape(1, d))


def _mm_body(x_ref, w_ref, o_ref, wbf_ref):
    @pl.when(pl.program_id(1) == 0)
    def _():
        wbf_ref[...] = w_ref[...].astype(BF16)

    o_ref[...] = jnp.dot(x_ref[...], wbf_ref[...],
                         preferred_element_type=F32).astype(o_ref.dtype)


def _mm_res_body(x_ref, w_ref, r_ref, o_ref, wbf_ref):
    @pl.when(pl.program_id(1) == 0)
    def _():
        wbf_ref[...] = w_ref[...].astype(BF16)

    o_ref[...] = r_ref[...] + jnp.dot(x_ref[...], wbf_ref[...],
                                      preferred_element_type=F32)


def _matmul(x, w, layer, *, tm, tn, out_dtype, residual=None, name):
    m, k = x.shape
    n = w.shape[2]
    grid = (n // tn, m // tm)
    in_specs = [pl.BlockSpec((tm, k), lambda j, i: (i, 0)),
                pl.BlockSpec((None, k, tn), lambda j, i: (layer, 0, j))]
    args = [x, w]
    body = _mm_body
    if residual is not None:
        in_specs.append(pl.BlockSpec((tm, tn), lambda j, i: (i, j)))
        args.append(residual)
        body = _mm_res_body
    nbytes = (2 * tm * k * 2 + 2 * k * tn * 4 + k * tn * 2
              + 6 * tm * tn * 4 + (4 << 20))
    return pl.pallas_call(
        body,
        grid=grid,
        in_specs=in_specs,
        out_specs=pl.BlockSpec((tm, tn), lambda j, i: (i, j)),
        out_shape=jax.ShapeDtypeStruct((m, n), out_dtype),
        scratch_shapes=[pltpu.VMEM((k, tn), BF16)],
        compiler_params=_params(("arbitrary", "arbitrary"), nbytes),
        name=name,
    )(*args)


def _out_norm_body(m_ref, w_ref, r_ref, g_ref, x_ref, h_ref, wbf_ref):
    @pl.when(pl.program_id(0) == 0)
    def _():
        wbf_ref[...] = w_ref[...].astype(BF16)

    x = r_ref[...] + jnp.dot(m_ref[...], wbf_ref[...], preferred_element_type=F32)
    x_ref[...] = x
    ms = jnp.mean(x * x, axis=-1, keepdims=True)
    h_ref[...] = (x * lax.rsqrt(ms + EPS) * g_ref[...]).astype(h_ref.dtype)


def _matmul_res_norm(a, w, layer, residual, g, *, tm, name):
    m, k = a.shape
    n = w.shape[2]
    nbytes = (k * n * 4 + k * n * 2 + 2 * tm * k * 2 + 4 * tm * n * 4
              + 2 * tm * n * 2 + 2 * tm * n * 4 + (4 << 20))
    return pl.pallas_call(
        _out_norm_body,
        grid=(m // tm,),
        in_specs=[pl.BlockSpec((tm, k), lambda i: (i, 0)),
                  pl.BlockSpec((None, k, n), lambda i: (layer, 0, 0),
                               pipeline_mode=pl.Buffered(1)),
                  pl.BlockSpec((tm, n), lambda i: (i, 0)),
                  pl.BlockSpec((None, 1, n), lambda i: (layer, 0, 0))],
        out_specs=[pl.BlockSpec((tm, n), lambda i: (i, 0)),
                   pl.BlockSpec((tm, n), lambda i: (i, 0))],
        out_shape=[jax.ShapeDtypeStruct((m, n), F32),
                   jax.ShapeDtypeStruct((m, n), BF16)],
        scratch_shapes=[pltpu.VMEM((k, n), BF16)],
        compiler_params=_params(("arbitrary",), nbytes),
        name=name,
    )(a, w, residual, g)


def _merge_body(oa_ref, ob_ref, ga_ref, gb_ref, wa_ref, wb_ref, o_ref,
                wabf_ref, wbbf_ref):
    @pl.when(pl.program_id(1) == 0)
    def _():
        wabf_ref[...] = wa_ref[...].astype(BF16)
        wbbf_ref[...] = wb_ref[...].astype(BF16)

    a = jnp.dot(oa_ref[...], wabf_ref[...], preferred_element_type=F32)
    b = jnp.dot(ob_ref[...], wbbf_ref[...], preferred_element_type=F32)
    ga = jax.nn.sigmoid(ga_ref[...].astype(F32))
    gb = jax.nn.sigmoid(gb_ref[...].astype(F32))
    o_ref[...] = (ga * a + gb * b).astype(o_ref.dtype)


def _merge(oa, ob, z, wa, wb, layer, *, ga_col, gb_col, tm, tn, name):
    m, k = oa.shape
    n = wa.shape[2]
    assert ga_col % tn == 0 and gb_col % tn == 0
    ga_blk = ga_col // tn
    gb_blk = gb_col // tn
    nbytes = (4 * tm * k * 2 + 4 * k * tn * 4 + 2 * k * tn * 2
              + 4 * tm * tn * 2 + 8 * tm * tn * 4 + (4 << 20))
    return pl.pallas_call(
        _merge_body,
        grid=(n // tn, m // tm),
        in_specs=[pl.BlockSpec((tm, k), lambda j, i: (i, 0)),
                  pl.BlockSpec((tm, k), lambda j, i: (i, 0)),
                  pl.BlockSpec((tm, tn), lambda j, i: (i, ga_blk + j)),
                  pl.BlockSpec((tm, tn), lambda j, i: (i, gb_blk + j)),
                  pl.BlockSpec((None, k, tn), lambda j, i: (layer, 0, j)),
                  pl.BlockSpec((None, k, tn), lambda j, i: (layer, 0, j))],
        out_specs=pl.BlockSpec((tm, tn), lambda j, i: (i, j)),
        out_shape=jax.ShapeDtypeStruct((m, n), BF16),
        scratch_shapes=[pltpu.VMEM((k, tn), BF16), pltpu.VMEM((k, tn), BF16)],
        compiler_params=_params(("arbitrary", "arbitrary"), nbytes),
        name=name,
    )(oa, ob, z, z, wa, wb)


def _rope_tables(pos):
    half = A_HEAD_DIM // 2
    inv = ROPE_THETA ** (-2.0 * jnp.arange(half, dtype=F32) / A_HEAD_DIM)
    ang = pos[:, None] * inv[None, :]
    c, s = jnp.cos(ang), jnp.sin(ang)
    cos = jnp.concatenate([c, c, c, c], axis=-1)
    sin = jnp.concatenate([-s, s, -s, s], axis=-1)
    return cos, sin


def _retrot_tables(pos):
    half = R_DIM // 2
    inv = 1.0 / (10000.0 ** jnp.linspace(0.0, 1.0, half, dtype=F32))
    ang = pos[:, None] * inv[None, :]
    c, s = jnp.cos(ang), jnp.sin(ang)
    cos = jnp.stack([c, c], axis=-1).reshape(pos.shape[0], R_DIM)
    sin = jnp.stack([-s, s], axis=-1).reshape(pos.shape[0], R_DIM)
    return cos, sin


def _rope128(x, cos, sin):
    lane = lax.broadcasted_iota(jnp.int32, x.shape, 1)
    first = (lane & (A_HEAD_DIM - 1)) < (A_HEAD_DIM // 2)
    partner = jnp.where(first, pltpu.roll(x, LANES - 32, 1), pltpu.roll(x, 32, 1))
    return x * cos + partner * sin


def _retrot128(x, cos, sin):
    lane = lax.broadcasted_iota(jnp.int32, x.shape, 1)
    even = (lane & 1) == 0
    partner = jnp.where(even, pltpu.roll(x, LANES - 1, 1), pltpu.roll(x, 1, 1))
    return x * cos + partner * sin


def _swa_heads(q_chunks, score_fn, out_fn, sink_ref, layer):
    rows = q_chunks[0].shape[0]
    lane = lax.broadcasted_iota(jnp.int32, (rows, LANES), 1)
    low = lane < A_HEAD_DIM
    outs = []
    for pair in range(A_HEADS // 2):
        halves = []
        for hpos in range(2):
            h = 2 * pair + hpos
            kv = h // A_GROUP
            c, kpos = kv // 2, kv % 2
            qc = q_chunks[pair]
            if hpos != kpos:
                qc = pltpu.roll(qc, A_HEAD_DIM, 1)
            keep = low if kpos == 0 else jnp.logical_not(low)
            qm = jnp.where(keep, qc, 0.0).astype(BF16)
            s = score_fn(qm, c)
            sink = sink_ref[layer, h]
            m = jnp.maximum(jnp.max(s, axis=-1, keepdims=True), sink)
            p = jnp.exp(s - m)
            denom = jnp.sum(p, axis=-1, keepdims=True) + jnp.exp(sink - m)
            o = out_fn(p, c) / denom
            if hpos != kpos:
                o = pltpu.roll(o, A_HEAD_DIM, 1)
            halves.append(o)
        outs.append(jnp.where(low, halves[0], halves[1]))
    return outs


def _swa_prompt_body(sink_ref, q_ref, k_ref, vc_ref, vp_ref, cos_ref, sin_ref,
                     o_ref, kr_ref, kprev_ref, *, layer):
    n = pl.program_id(1)
    cos = cos_ref[...]
    sin = sin_ref[...]

    @pl.when(n == 0)
    def _():
        kprev_ref[...] = jnp.zeros_like(kprev_ref)

    k_cur, k_both, v_both = [], [], []
    for c in range(2):
        sl = slice(c * LANES, (c + 1) * LANES)
        kc = _rope128(k_ref[:, sl].astype(F32), cos, sin)
        kr_ref[:, sl] = kc
        k_cur.append(kc.astype(BF16))
        k_both.append(jnp.concatenate([kprev_ref[:, sl], k_cur[c]], axis=0))
        v_both.append(jnp.concatenate([vp_ref[:, sl], vc_ref[:, sl]], axis=0))

    scale = A_HEAD_DIM ** -0.5
    q_chunks = [
        _rope128(q_ref[:, c * LANES:(c + 1) * LANES].astype(F32), cos, sin) * scale
        for c in range(A_HEADS // 2)]

    qi = lax.broadcasted_iota(jnp.int32, (WINDOW, WINDOW), 0)
    kj = lax.broadcasted_iota(jnp.int32, (WINDOW, WINDOW), 1)
    from_cur = kj <= qi
    prev_bias = jnp.where(n > 0, 0.0, -jnp.inf)

    def scores(qm, c):
        s = lax.dot_general(qm, k_both[c], NT_DIMS, preferred_element_type=F32)
        return jnp.where(from_cur, s[:, WINDOW:], s[:, :WINDOW] + prev_bias)

    def weighted_values(p, c):
        p_both = jnp.concatenate([jnp.where(from_cur, 0.0, p), jnp.where(from_cur, p, 0.0)],
                                 axis=1).astype(BF16)
        return jnp.dot(p_both, v_both[c], preferred_element_type=F32)

    outs = _swa_heads(q_chunks, scores, weighted_values, sink_ref, layer)
    for c, o in enumerate(outs):
        o_ref[:, c * LANES:(c + 1) * LANES] = o.astype(o_ref.dtype)
    for c in range(2):
        kprev_ref[:, c * LANES:(c + 1) * LANES] = k_cur[c]


def _swa_prompt(z, sinks, layer, cos, sin, *, batch, seq, out_rows, name):
    nb = seq // WINDOW
    a_width = A_HEADS * A_HEAD_DIM
    kv_width = A_KV_HEADS * A_HEAD_DIM
    k_blk = a_width // kv_width
    v_blk = k_blk + 1
    rows = out_rows
    return pl.pallas_call(
        functools.partial(_swa_prompt_body, layer=layer),
        grid=(batch, nb),
        in_specs=[
            pl.BlockSpec(memory_space=pltpu.SMEM),
            pl.BlockSpec((WINDOW, a_width), lambda b, n: (b * nb + n, 0)),
            pl.BlockSpec((WINDOW, kv_width), lambda b, n: (b * nb + n, k_blk)),
            pl.BlockSpec((WINDOW, kv_width), lambda b, n: (b * nb + n, v_blk)),
            pl.BlockSpec((WINDOW, kv_width),
                         lambda b, n: (b * nb + jnp.maximum(n - 1, 0), v_blk)),
            pl.BlockSpec((WINDOW, LANES), lambda b, n: (n, 0)),
            pl.BlockSpec((WINDOW, LANES), lambda b, n: (n, 0)),
        ],
        out_specs=[
            pl.BlockSpec((WINDOW, a_width), lambda b, n: (b * nb + n, 0)),
            pl.BlockSpec((WINDOW, kv_width), lambda b, n: (b, 0)),
        ],
        out_shape=[jax.ShapeDtypeStruct((rows, a_width), BF16),
                   jax.ShapeDtypeStruct((batch * WINDOW, kv_width), F32)],
        scratch_shapes=[pltpu.VMEM((WINDOW, kv_width), BF16)],
        compiler_params=_params(("arbitrary", "arbitrary"), 0),
        name=name,
    )(sinks, z, z, z, z, cos, sin)


SWA_SAMPLE_SEQS = 8


def _swa_sample_body(sink_ref, q_ref, k_ref, v_ref, ck_ref, cv_ref, cos_ref,
                     sin_ref, oa_hbm_ref, o_ref, kr_ref, *, dec_seq, layer):
    del oa_hbm_ref
    cos = cos_ref[...]
    sin = sin_ref[...]
    g = SWA_SAMPLE_SEQS
    rows = g * dec_seq
    ncache = g * WINDOW
    k_chunks, v_chunks = [], []
    for c in range(2):
        sl = slice(c * LANES, (c + 1) * LANES)
        kc = _rope128(k_ref[:, sl].astype(F32), cos, sin)
        kr_ref[:, sl] = kc
        k_chunks.append(jnp.concatenate(
            [ck_ref[:, sl].astype(BF16), kc.astype(BF16)], axis=0))
        v_chunks.append(jnp.concatenate(
            [cv_ref[:, sl].astype(BF16), v_ref[:, sl]], axis=0))

    scale = A_HEAD_DIM ** -0.5
    q_chunks = [
        _rope128(q_ref[:, c * LANES:(c + 1) * LANES].astype(F32), cos, sin) * scale
        for c in range(A_HEADS // 2)]

    shape = (rows, ncache + rows)
    r = lax.broadcasted_iota(jnp.int32, shape, 0)
    col = lax.broadcasted_iota(jnp.int32, shape, 1)
    log_l = dec_seq.bit_length() - 1
    log_w = WINDOW.bit_length() - 1
    seq_q = r >> log_l
    i = r & (dec_seq - 1)
    in_cache = col < ncache
    cnew = col - ncache
    seq_k = jnp.where(in_cache, col >> log_w, cnew >> log_l)
    ok = ((in_cache & ((col & (WINDOW - 1)) > i))
          | (jnp.logical_not(in_cache) & ((cnew & (dec_seq - 1)) <= i)))
    valid = (seq_k == seq_q) & ok

    def scores(qm, c):
        s = lax.dot_general(qm, k_chunks[c], NT_DIMS, preferred_element_type=F32)
        return jnp.where(valid, s, -jnp.inf)

    def weighted_values(p, c):
        return jnp.dot(p.astype(BF16), v_chunks[c], preferred_element_type=F32)

    outs = _swa_heads(q_chunks, scores, weighted_values, sink_ref, layer)
    for c, o in enumerate(outs):
        o_ref[:, c * LANES:(c + 1) * LANES] = o.astype(o_ref.dtype)


def _swa_sample(z, cache_k, cache_v, sinks, layer, oa, cos, sin, *, row0, dec_batch, dec_seq,
                name):
    g = SWA_SAMPLE_SEQS
    rows = g * dec_seq
    a_width = A_HEADS * A_HEAD_DIM
    kv_width = A_KV_HEADS * A_HEAD_DIM
    k_blk = a_width // kv_width
    v_blk = k_blk + 1
    r0 = row0 // rows
    depth = cache_k.shape[0]
    ck = cache_k.reshape(depth, dec_batch * WINDOW, kv_width)
    cv = cache_v.reshape(depth, dec_batch * WINDOW, kv_width)
    return pl.pallas_call(
        functools.partial(_swa_sample_body, dec_seq=dec_seq, layer=layer),
        grid=(dec_batch // g,),
        in_specs=[
            pl.BlockSpec(memory_space=pltpu.SMEM),
            pl.BlockSpec((rows, a_width), lambda s: (r0 + s, 0)),
            pl.BlockSpec((rows, kv_width), lambda s: (r0 + s, k_blk)),
            pl.BlockSpec((rows, kv_width), lambda s: (r0 + s, v_blk)),
            pl.BlockSpec((None, g * WINDOW, kv_width), lambda s: (layer, s, 0)),
            pl.BlockSpec((None, g * WINDOW, kv_width), lambda s: (layer, s, 0)),
            pl.BlockSpec((rows, LANES), lambda s: (0, 0)),
            pl.BlockSpec((rows, LANES), lambda s: (0, 0)),
            pl.BlockSpec(memory_space=pl.ANY),
        ],
        out_specs=[
            pl.BlockSpec((rows, a_width), lambda s: (r0 + s, 0)),
            pl.BlockSpec((rows, kv_width), lambda s: (s, 0)),
        ],
        out_shape=[jax.ShapeDtypeStruct(oa.shape, oa.dtype),
                   jax.ShapeDtypeStruct((dec_batch * dec_seq, kv_width), F32)],
        input_output_aliases={8: 0},
        compiler_params=_params(("arbitrary",), 0),
        name=name,
    )(sinks, z, z, z, ck, cv, cos, sin, oa)


RET_HEADS_PER_STEP = 4
RET_BLOCK = RET_HEADS_PER_STEP * R_DIM
RET_CHUNKS_PER_STEP = 4


def _decay_tables(length, nseq):
    log_g = np.log1p(-np.exp2(-5.0 - np.arange(R_HEADS, dtype=np.float64)))
    i = np.arange(length, dtype=np.float64)
    diff = i[:, None] - i[None, :]
    d = np.where(diff >= 0, np.exp(log_g[:, None, None] * np.maximum(diff, 0.0)), 0.0)
    d = np.einsum('st,hij->hsitj', np.eye(nseq), d).reshape(
        R_HEADS, nseq * length, nseq * length)
    q_dec = np.exp(log_g[:, None] * (i[None, :] + 1.0))
    k_dec = np.exp(log_g[:, None] * (length - 1.0 - i)[None, :])
    shape = (R_HEADS, nseq * length, R_DIM)
    qd = np.broadcast_to(np.tile(q_dec, (1, nseq))[:, :, None], shape)
    kd = np.broadcast_to(np.tile(k_dec, (1, nseq))[:, :, None], shape)
    g_len = np.exp(log_g * length)
    return tuple(jnp.asarray(t, F32) for t in (d, qd, kd, g_len))


def _group_norm_gate(o, gate):
    o = o * lax.rsqrt(jnp.mean(o * o, axis=-1, keepdims=True) + EPS)
    return o * (gate * jax.nn.sigmoid(gate))


def _ret_prompt_body(gl_ref, *refs):
    nb = R_HEADS // RET_HEADS_PER_STEP
    q_refs, k_refs, v_refs, g_refs = (refs[t * nb:(t + 1) * nb] for t in range(4))
    cos_ref, sin_ref, d_ref, qd_ref, kd_ref, o_ref, s_ref = refs[4 * nb:]
    c = pl.program_id(1)

    @pl.when(c == 0)
    def _():
        s_ref[...] = jnp.zeros_like(s_ref)

    scale = R_DIM ** -0.5
    for ck in range(RET_CHUNKS_PER_STEP):
        rs = slice(ck * R_CHUNK, (ck + 1) * R_CHUNK)
        cos = cos_ref[rs, :]
        sin = sin_ref[rs, :]
        for h in range(R_HEADS):
            blk = h // RET_HEADS_PER_STEP
            sl = slice((h % RET_HEADS_PER_STEP) * R_DIM, (h % RET_HEADS_PER_STEP + 1) * R_DIM)
            q = _retrot128(q_refs[blk][rs, sl].astype(F32), cos, sin)
            k = _retrot128(k_refs[blk][rs, sl].astype(F32), cos, sin) * scale
            v = v_refs[blk][rs, sl]
            qb = q.astype(BF16)
            state = s_ref[0, h]
            scores = lax.dot_general(qb, k.astype(BF16), NT_DIMS,
                                     preferred_element_type=F32) * d_ref[h]
            o = jnp.dot(scores.astype(BF16), v, preferred_element_type=F32)
            cross = jnp.dot(qb, state.astype(BF16), preferred_element_type=F32)
            o = o + cross * qd_ref[h]
            kd = (k * kd_ref[h]).astype(BF16)
            s_ref[0, h] = gl_ref[h] * state + lax.dot_general(
                kd, v, TN_DIMS, preferred_element_type=F32)
            o_ref[rs, h * R_DIM:(h + 1) * R_DIM] = _group_norm_gate(
                o, g_refs[blk][rs, sl].astype(F32)).astype(o_ref.dtype)


def _ret_prompt(z, cos, sin, tabs, *, col0, batch, seq, out_rows, name):
    d, qd, kd, g_len = tabs
    rows = RET_CHUNKS_PER_STEP * R_CHUNK
    assert seq % rows == 0
    nc = seq // rows
    nb = R_HEADS // RET_HEADS_PER_STEP
    width = R_HEADS * R_DIM
    blks = [(col0 + t * width) // RET_BLOCK + i for t in range(4) for i in range(nb)]

    def zspec(blk):
        return pl.BlockSpec((rows, RET_BLOCK), lambda b, c: (b * nc + c, blk))

    def tspec():
        return pl.BlockSpec((R_HEADS, R_CHUNK, R_DIM), lambda b, c: (0, 0, 0))

    return pl.pallas_call(
        _ret_prompt_body,
        grid=(batch, nc),
        in_specs=[pl.BlockSpec(memory_space=pltpu.SMEM)]
        + [zspec(blk) for blk in blks]
        + [pl.BlockSpec((rows, R_DIM), lambda b, c: (c, 0)),
           pl.BlockSpec((rows, R_DIM), lambda b, c: (c, 0)),
           tspec(), tspec(), tspec()],
        out_specs=[
            pl.BlockSpec((rows, width), lambda b, c: (b * nc + c, 0)),
            pl.BlockSpec((1, R_HEADS, R_DIM, R_DIM), lambda b, c: (b, 0, 0, 0)),
        ],
        out_shape=[jax.ShapeDtypeStruct((out_rows, width), BF16),
                   jax.ShapeDtypeStruct((batch, R_HEADS, R_DIM, R_DIM), F32)],
        compiler_params=_params(("arbitrary", "arbitrary"), 0),
        name=name,
    )(g_len, *([z] * len(blks)), cos, sin, d, qd, kd)


def _ret_sample_body(gl_ref, q_ref, k_ref, v_ref, g_ref, cos_ref, sin_ref,
                     d_ref, qd_ref, kd_ref, s_ref, ob_hbm_ref, o_ref, so_ref, *, dec_seq):
    del ob_hbm_ref
    hh = pl.program_id(1)
    cos = cos_ref[...]
    sin = sin_ref[...]
    nseq = R_CHUNK // dec_seq
    log_l = dec_seq.bit_length() - 1
    row_seq = lax.broadcasted_iota(jnp.int32, (R_CHUNK, R_DIM), 0) >> log_l
    scale = R_DIM ** -0.5
    for hl in range(RET_HEADS_PER_STEP):
        sl = slice(hl * R_DIM, (hl + 1) * R_DIM)
        q = _retrot128(q_ref[:, sl].astype(F32), cos, sin)
        k = _retrot128(k_ref[:, sl].astype(F32), cos, sin) * scale
        v = v_ref[:, sl]
        qb = q.astype(BF16)
        scores = lax.dot_general(qb, k.astype(BF16), NT_DIMS,
                                 preferred_element_type=F32) * d_ref[hl]
        o = jnp.dot(scores.astype(BF16), v, preferred_element_type=F32)
        kd = k * kd_ref[hl]
        gl = gl_ref[hh * RET_HEADS_PER_STEP + hl]
        cross = jnp.zeros((R_CHUNK, R_DIM), F32)
        for s in range(nseq):
            mine = row_seq == s
            state = s_ref[s, hl]
            cs = jnp.dot(qb, state.astype(BF16), preferred_element_type=F32)
            cross = jnp.where(mine, cs, cross)
            ks = jnp.where(mine, kd, 0.0).astype(BF16)
            so_ref[s, hl] = gl * state + lax.dot_general(
                ks, v, TN_DIMS, preferred_element_type=F32)
        o = o + cross * qd_ref[hl]
        o_ref[:, sl] = _group_norm_gate(o, g_ref[:, sl].astype(F32)).astype(o_ref.dtype)


def _ret_sample(z, state, layer, ob, cos, sin, tabs, *, row0, col0, dec_batch, dec_seq, name):
    d, qd, kd, g_len = tabs
    nseq = R_CHUNK // dec_seq
    steps = dec_batch // nseq
    nh = R_HEADS // RET_HEADS_PER_STEP
    width = R_HEADS * R_DIM
    qb, kb, vb, gb = [(col0 + t * width) // RET_BLOCK for t in range(4)]
    r0 = row0 // R_CHUNK
    hp = RET_HEADS_PER_STEP

    def zspec(blk):
        return pl.BlockSpec((R_CHUNK, RET_BLOCK), lambda s, h: (r0 + s, blk + h))

    def tspec():
        return pl.BlockSpec((hp, R_CHUNK, R_DIM), lambda s, h: (h, 0, 0))

    sspec_in = pl.BlockSpec((None, nseq, hp, R_DIM, R_DIM), lambda s, h: (layer, s, h, 0, 0))
    sspec = pl.BlockSpec((nseq, hp, R_DIM, R_DIM), lambda s, h: (s, h, 0, 0))
    return pl.pallas_call(
        functools.partial(_ret_sample_body, dec_seq=dec_seq),
        grid=(steps, nh),
        in_specs=[
            pl.BlockSpec(memory_space=pltpu.SMEM),
            zspec(qb), zspec(kb), zspec(vb), zspec(gb),
            pl.BlockSpec((R_CHUNK, R_DIM), lambda s, h: (0, 0)),
            pl.BlockSpec((R_CHUNK, R_DIM), lambda s, h: (0, 0)),
            tspec(), tspec(), tspec(),
            sspec_in,
            pl.BlockSpec(memory_space=pl.ANY),
        ],
        out_specs=[
            pl.BlockSpec((R_CHUNK, RET_BLOCK), lambda s, h: (r0 + s, h)),
            sspec,
        ],
        out_shape=[jax.ShapeDtypeStruct(ob.shape, ob.dtype),
                   jax.ShapeDtypeStruct(state.shape[1:], F32)],
        input_output_aliases={11: 0},
        compiler_params=_params(("arbitrary", "arbitrary"),
                                4 * nseq * hp * R_DIM * R_DIM * 4 + (16 << 20)),
        name=name,
    )(g_len, z, z, z, z, cos, sin, d, qd, kd, state, ob)


def _gelu_tanh(x):
    c = (2.0 / jnp.pi) ** 0.5
    return 0.5 * x * (1.0 + jnp.tanh(c * (x + 0.044715 * (x * x * x))))


def _ffn_up_body(h_ref, wu_ref, wg_ref, cw_ref, cb_ref, e0_ref, e1_ref,
                 f_ref, tail_ref, us_ref, wubf_ref, wgbf_ref, carry_ref, ug_ref,
                 *, tm, n_tiles, seq, batch, ms, dec_seq):
    i = pl.program_id(1)
    tn = f_ref.shape[1]
    mp = batch * seq

    @pl.when(i == 0)
    def _():
        wubf_ref[...] = wu_ref[...].astype(BF16)
        wgbf_ref[...] = wg_ref[...].astype(BF16)
        carry_ref[...] = jnp.zeros_like(carry_ref)

    last = n_tiles - 1
    s0 = mp - last * tm
    assert 0 <= s0 and s0 + ms == tm and s0 % FIX_ROWS == 0
    starts = [k * seq for k in range(1, batch)]
    for r in starts:
        assert r % tm != 0 and r % FIX_ROWS == 0, "mid-tile, packed-tile aligned starts only"

    def conv_gate(u, gate, p1, p2, w, bias):
        conv = bias + w[0:1, :] * p2
        conv = conv + w[1:2, :] * p1
        conv = conv + w[2:3, :] * u
        return (_gelu_tanh(conv) * gate).astype(f_ref.dtype)

    tr = tm // FFN_ROW_PARTS
    assert tm % FFN_ROW_PARTS == 0 and tr % FIX_ROWS == 0
    for c in range(tn // MXU_COLS):
        cs = slice(c * MXU_COLS, (c + 1) * MXU_COLS)
        w = cw_ref[:, cs]
        bias = cb_ref[:, cs]
        for part in range(FFN_ROW_PARTS):
            rs = slice(part * tr, (part + 1) * tr)
            h = h_ref[rs, :]
            ug_ref[2 * c, rs, :] = jnp.dot(h, wubf_ref[:, cs], preferred_element_type=F32)
            ug_ref[2 * c + 1, rs, :] = jnp.dot(h, wgbf_ref[:, cs], preferred_element_type=F32)
            u = ug_ref[2 * c, rs, :]
            gate = ug_ref[2 * c + 1, rs, :]

            if part == 0:
                before = carry_ref[:, cs]
            else:
                before = ug_ref[2 * c, part * tr - SUBLANES:part * tr, :]
            last1 = before[SUBLANES - 1:SUBLANES, :]
            last2 = before[SUBLANES - 2:SUBLANES - 1, :]
            row = lax.broadcasted_iota(jnp.int32, u.shape, 0)
            prev1 = jnp.where(row == 0, last1, pltpu.roll(u, 1, 0))
            prev2 = jnp.where(row == 0, last2,
                              jnp.where(row == 1, last1, pltpu.roll(u, 2, 0)))
            f_ref[rs, cs] = conv_gate(u, gate, prev1, prev2, w, bias)
        carry_ref[:, cs] = ug_ref[2 * c, tm - SUBLANES:, :]

    def rewrite(rows, p1_fn, p2_fn):
        for c in range(tn // MXU_COLS):
            cs = slice(c * MXU_COLS, (c + 1) * MXU_COLS)
            us = ug_ref[2 * c, rows, :]
            f_ref[rows, cs] = conv_gate(us, ug_ref[2 * c + 1, rows, :], p1_fn(us, cs),
                                        p2_fn(us, cs), cw_ref[:, cs], cb_ref[:, cs])

    for r in starts:
        @pl.when(i == r // tm)
        def _(off=r % tm):
            pos = lax.broadcasted_iota(jnp.int32, (FIX_ROWS, MXU_COLS), 0)
            rewrite(slice(off, off + FIX_ROWS),
                    lambda us, cs: jnp.where(pos == 0, 0.0, pltpu.roll(us, 1, 0)),
                    lambda us, cs: jnp.where(pos <= 1, 0.0, pltpu.roll(us, 2, 0)))

    @pl.when(i == last)
    def _():
        pos = lax.broadcasted_iota(jnp.int32, (ms, MXU_COLS), 0) & (dec_seq - 1)
        rewrite(slice(s0, tm),
                lambda us, cs: jnp.where(pos == 0, e1_ref[:, cs], pltpu.roll(us, 1, 0)),
                lambda us, cs: jnp.where(
                    pos == 0, e0_ref[:, cs],
                    jnp.where(pos == 1, e1_ref[:, cs], pltpu.roll(us, 2, 0))))
        for c in range(tn // MXU_COLS):
            us_ref[:, c * MXU_COLS:(c + 1) * MXU_COLS] = ug_ref[2 * c, s0:, :]

    for b in range(batch):
        r = (b + 1) * seq - SUBLANES

        @pl.when(i == r // tm)
        def _(b=b, off=r % tm):
            for c in range(tn // MXU_COLS):
                tail_ref[b * SUBLANES:(b + 1) * SUBLANES, c * MXU_COLS:(c + 1) * MXU_COLS] = (
                    ug_ref[2 * c, off:off + SUBLANES, :])


def _ffn_up(h, w_up, conv_w, conv_b, layer, e, *, tm, tn, seq, batch, dec_seq, name):
    m, k = h.shape
    d_ff = w_up.shape[2] // 2
    ms = e.shape[2]
    nj = d_ff // tn
    nt = m // tm
    body = functools.partial(_ffn_up_body, tm=tm, n_tiles=nt, seq=seq, batch=batch,
                             ms=ms, dec_seq=dec_seq)
    nbytes = (2 * tm * k * 2 + 4 * k * tn * 4 + 2 * k * tn * 2
              + 16 * tm * tn * 4 + 4 * ms * tn * 4 + (4 << 20))
    return pl.pallas_call(
        body,
        grid=(nj, nt),
        in_specs=[
            pl.BlockSpec((tm, k), lambda j, i: (i, 0)),
            pl.BlockSpec((None, k, tn), lambda j, i: (layer, 0, j)),
            pl.BlockSpec((None, k, tn), lambda j, i: (layer, 0, nj + j)),
            pl.BlockSpec((None, CONV_W, tn), lambda j, i: (layer, 0, j)),
            pl.BlockSpec((None, 1, tn), lambda j, i: (layer, 0, j)),
            pl.BlockSpec((None, None, ms, tn), lambda j, i: (layer, 0, 0, j)),
            pl.BlockSpec((None, None, ms, tn), lambda j, i: (layer, 1, 0, j)),
        ],
        out_specs=[
            pl.BlockSpec((tm, tn), lambda j, i: (i, j)),
            pl.BlockSpec((batch * SUBLANES, tn), lambda j, i: (0, j)),
            pl.BlockSpec((ms, tn), lambda j, i: (0, j)),
        ],
        out_shape=[jax.ShapeDtypeStruct((m, d_ff), BF16),
                   jax.ShapeDtypeStruct((batch * SUBLANES, d_ff), F32),
                   jax.ShapeDtypeStruct((ms, d_ff), F32)],
        scratch_shapes=[pltpu.VMEM((k, tn), BF16), pltpu.VMEM((k, tn), BF16),
                        pltpu.VMEM((SUBLANES, tn), F32),
                        pltpu.VMEM((2 * tn // MXU_COLS, tm, MXU_COLS), F32)],
        compiler_params=_params(("arbitrary", "arbitrary"), nbytes),
        name=name,
    )(h, w_up, w_up, conv_w, conv_b, e, e)


TM = 1408
TM_NORM = 768
TM_DOWN = 704
TM_IN = 2816
TN_IN = 512
TN_UP = 512
TM_MERGE = 1408
TN_MERGE = 512
TM_OUT = 352
FFN_ROW_PARTS = 2
TN_OUT = 512


def kernel(x_prompt, x_sample, cache_win_k, cache_win_v, state_ret, state_conv,
           g_mix, w_in, sinks, w_proj_a, w_proj_b, w_o, g_ffn, w_up, conv_w,
           conv_b, w_down, g_final):
    batch, seq, d_model = x_prompt.shape
    dec_batch, dec_seq, _ = x_sample.shape
    depth = w_in.shape[0]
    d_ff = w_down.shape[1]
    mp = batch * seq
    ms = dec_batch * dec_seq
    a_width = A_HEADS * A_HEAD_DIM
    kv_width = A_KV_HEADS * A_HEAD_DIM
    r_width = R_HEADS * R_DIM
    ret_col0 = a_width + 2 * kv_width
    ga_col = ret_col0 + 4 * r_width
    gb_col = ga_col + d_model
    assert dec_seq & (dec_seq - 1) == 0 and R_CHUNK % dec_seq == 0

    pos_p = jnp.arange(seq, dtype=F32)
    pos_s = PAST_LEN + jnp.arange(dec_seq, dtype=F32)
    rope_p = _rope_tables(pos_p)
    rope_s = tuple(jnp.tile(t, (SWA_SAMPLE_SEQS, 1)) for t in _rope_tables(pos_s))
    rrot_p = _retrot_tables(pos_p)
    nseq = R_CHUNK // dec_seq
    rrot_s = tuple(jnp.tile(t, (nseq, 1)) for t in _retrot_tables(pos_s))
    tabs_p = _decay_tables(R_CHUNK, 1)
    tabs_s = _decay_tables(dec_seq, nseq)
    g_mix3 = g_mix.reshape(depth, 1, d_model)
    g_ffn3 = g_ffn.reshape(depth, 1, d_model)
    conv_b3 = conv_b.reshape(depth, 1, d_ff)

    e_conv = jnp.repeat(jnp.swapaxes(state_conv, 1, 2), dec_seq, axis=2)
    v_col0 = a_width + kv_width

    kp_l, vp_l, sp_l, cp_l = [], [], [], []
    ks_l, vs_l, ss_l, cs_l = [], [], [], []
    for l in range(depth):
        if l == 0:
            x, h = _join_norm(x_prompt.reshape(mp, d_model), x_sample.reshape(ms, d_model),
                              g_mix3, l, name="join_norm_mix0")
        else:
            h = _rmsnorm(x, g_mix3, l, BF16, tm=TM_NORM, name=f"norm_mix{l}")
        z = _matmul(h, w_in, l, tm=TM_IN, tn=TN_IN, out_dtype=BF16, name=f"proj_in{l}")

        oa, kr_p = _swa_prompt(z, sinks, l, *rope_p, batch=batch, seq=seq,
                               out_rows=mp + ms, name=f"swa_prompt{l}")
        oa, kr_s = _swa_sample(z, cache_win_k, cache_win_v, sinks, l, oa, *rope_s,
                               row0=mp, dec_batch=dec_batch, dec_seq=dec_seq,
                               name=f"swa_sample{l}")
        ob, s_p = _ret_prompt(z, *rrot_p, tabs_p, col0=ret_col0, batch=batch,
                              seq=seq, out_rows=mp + ms, name=f"ret_prompt{l}")
        ob, s_s = _ret_sample(z, state_ret, l, ob, *rrot_s, tabs_s, row0=mp,
                              col0=ret_col0, dec_batch=dec_batch, dec_seq=dec_seq,
                              name=f"ret_sample{l}")
        merged = _merge(oa, ob, z, w_proj_a, w_proj_b, l, ga_col=ga_col,
                        gb_col=gb_col, tm=TM_MERGE, tn=TN_MERGE, name=f"merge{l}")
        x, h2 = _matmul_res_norm(merged, w_o, l, x, g_ffn3, tm=TM_OUT,
                                 name=f"proj_out_norm{l}")
        f, u_tail, u_s = _ffn_up(h2, w_up, conv_w, conv_b3, l, e_conv, tm=TM, tn=TN_UP,
                                 seq=seq, batch=batch, dec_seq=dec_seq, name=f"ffn_up{l}")
        x = _matmul(f, w_down, l, tm=TM_DOWN, tn=TN_OUT, out_dtype=F32, residual=x,
                    name=f"proj_down{l}")

        v_p = jnp.stack([lax.slice(z, ((b + 1) * seq - WINDOW, v_col0), ((b + 1) * seq, ret_col0))
                         for b in range(batch)])
        v_s = lax.slice(z, (mp, v_col0), (mp + ms, ret_col0))
        kp_l.append(kr_p.reshape(batch, WINDOW, A_KV_HEADS, A_HEAD_DIM))
        vp_l.append(v_p.astype(F32).reshape(batch, WINDOW, A_KV_HEADS, A_HEAD_DIM))
        sp_l.append(s_p)
        cp_l.append(u_tail.reshape(batch, SUBLANES, d_ff)[:, -(CONV_W - 1):])
        ks_l.append(kr_s.reshape(dec_batch, dec_seq, A_KV_HEADS, A_HEAD_DIM))
        vs_l.append(v_s.astype(F32).reshape(dec_batch, dec_seq, A_KV_HEADS, A_HEAD_DIM))
        ss_l.append(s_s)
        cs_l.append(u_s.reshape(dec_batch, dec_seq, d_ff)[:, -(CONV_W - 1):])

    y_prompt, y_sample = _split_norm(x, g_final, mp=mp, name="norm_final")
    y_prompt = y_prompt.reshape(batch, seq, d_model)
    y_sample = y_sample.reshape(dec_batch, dec_seq, d_model)
    return (y_prompt, y_sample,
            jnp.stack(kp_l), jnp.stack(vp_l), jnp.stack(sp_l), jnp.stack(cp_l),
            jnp.stack(ks_l), jnp.stack(vs_l), jnp.stack(ss_l), jnp.stack(cs_l))
```

```python
import functools

import jax
import jax.numpy as jnp
import numpy as np
from jax import lax
from jax.experimental import pallas as pl
from jax.experimental.pallas import tpu as pltpu

F32 = jnp.float32
BF16 = jnp.bfloat16

LANES = 128
SUBLANES = 8
MXU_COLS = 256
FIX_ROWS = 2 * SUBLANES
VMEM_LIMIT_CAP = 56 * 1024 * 1024

WINDOW = 128
A_HEADS = 16
A_KV_HEADS = 4
A_HEAD_DIM = 64
A_GROUP = A_HEADS // A_KV_HEADS
R_HEADS = 8
R_DIM = 128
R_CHUNK = 128
CONV_W = 3
EPS = 1e-6
ROPE_THETA = 10000.0
PAST_LEN = 16384

NT_DIMS = (((1,), (1,)), ((), ()))
TN_DIMS = (((0,), (0,)), ((), ()))


def _vmem_limit(nbytes):
    return int(min(VMEM_LIMIT_CAP, max(32 * 1024 * 1024, nbytes)))


def _params(semantics, nbytes):
    return pltpu.CompilerParams(dimension_semantics=semantics,
                                vmem_limit_bytes=_vmem_limit(nbytes))


def _rmsnorm_body(x_ref, g_ref, o_ref):
    x = x_ref[...]
    ms = jnp.mean(x * x, axis=-1, keepdims=True)
    o_ref[...] = (x * lax.rsqrt(ms + EPS) * g_ref[...]).astype(o_ref.dtype)


def _rmsnorm(x, g, layer, out_dtype, *, tm, name):
    m, d = x.shape
    return pl.pallas_call(
        _rmsnorm_body,
        grid=(m // tm,),
        in_specs=[pl.BlockSpec((tm, d), lambda i: (i, 0)),
                  pl.BlockSpec((None, 1, d), lambda i: (layer, 0, 0))],
        out_specs=pl.BlockSpec((tm, d), lambda i: (i, 0)),
        out_shape=jax.ShapeDtypeStruct((m, d), out_dtype),
        compiler_params=_params(("arbitrary",), 6 * tm * d * 4),
        name=name,
    )(x, g)


def _join_norm_body(xp_ref, xs_ref, g_ref, x_ref, h_ref, *, prompt_tiles):
    i = pl.program_id(0)

    def emit(x):
        ms = jnp.mean(x * x, axis=-1, keepdims=True)
        x_ref[...] = x
        h_ref[...] = (x * lax.rsqrt(ms + EPS) * g_ref[...]).astype(h_ref.dtype)

    @pl.when(i < prompt_tiles)
    def _():
        emit(xp_ref[...])

    @pl.when(i >= prompt_tiles)
    def _():
        emit(xs_ref[...])


def _join_norm(xp, xs, g, layer, *, name):
    mp, d = xp.shape
    ms = xs.shape[0]
    assert mp % ms == 0
    pt = mp // ms
    return pl.pallas_call(
        functools.partial(_join_norm_body, prompt_tiles=pt),
        grid=(pt + 1,),
        in_specs=[pl.BlockSpec((ms, d), lambda i: (jnp.minimum(i, pt - 1), 0)),
                  pl.BlockSpec((ms, d), lambda i: (0, 0)),
                  pl.BlockSpec((None, 1, d), lambda i: (layer, 0, 0))],
        out_specs=[pl.BlockSpec((ms, d), lambda i: (i, 0)),
                   pl.BlockSpec((ms, d), lambda i: (i, 0))],
        out_shape=[jax.ShapeDtypeStruct((mp + ms, d), F32),
                   jax.ShapeDtypeStruct((mp + ms, d), BF16)],
        compiler_params=_params(("arbitrary",), 0),
        name=name,
    )(xp, xs, g)


def _split_norm_body(x_ref, g_ref, yp_ref, ys_ref, *, prompt_tiles):
    i = pl.program_id(0)
    x = x_ref[...]
    ms = jnp.mean(x * x, axis=-1, keepdims=True)
    y = x * lax.rsqrt(ms + EPS) * g_ref[...]

    @pl.when(i < prompt_tiles)
    def _():
        yp_ref[...] = y

    @pl.when(i >= prompt_tiles)
    def _():
        ys_ref[...] = y


def _split_norm(x, g, *, mp, name):
    m, d = x.shape
    ms = m - mp
    assert mp % ms == 0
    pt = mp // ms
    return pl.pallas_call(
        functools.partial(_split_norm_body, prompt_tiles=pt),
        grid=(pt + 1,),
        in_specs=[pl.BlockSpec((ms, d), lambda i: (i, 0)),
                  pl.BlockSpec((1, d), lambda i: (0, 0))],
        out_specs=[pl.BlockSpec((ms, d), lambda i: (jnp.minimum(i, pt - 1), 0)),
                   pl.BlockSpec((ms, d), lambda i: (0, 0))],
        out_shape=[jax.ShapeDtypeStruct((mp, d), F32),
                   jax.ShapeDtypeStruct((ms, d), F32)],
        compiler_params=_params(("arbitrary",), 0),
        name=name,
    )(x, g.reshape(1, d))


def _mm_body(x_ref, w_ref, o_ref, wbf_ref):
    @pl.when(pl.program_id(1) == 0)
    def _():
        wbf_ref[...] = w_ref[...].astype(BF16)

    o_ref[...] = jnp.dot(x_ref[...], wbf_ref[...],
                         preferred_element_type=F32).astype(o_ref.dtype)


def _mm_res_body(x_ref, w_ref, r_ref, o_ref, wbf_ref):
    @pl.when(pl.program_id(1) == 0)
    def _():
        wbf_ref[...] = w_ref[...].astype(BF16)

    o_ref[...] = r_ref[...] + jnp.dot(x_ref[...], wbf_ref[...],
                                      preferred_element_type=F32)


def _matmul(x, w, layer, *, tm, tn, out_dtype, residual=None, name):
    m, k = x.shape
    n = w.shape[2]
    grid = (n // tn, m // tm)
    in_specs = [pl.BlockSpec((tm, k), lambda j, i: (i, 0)),
                pl.BlockSpec((None, k, tn), lambda j, i: (layer, 0, j))]
    args = [x, w]
    body = _mm_body
    if residual is not None:
        in_specs.append(pl.BlockSpec((tm, tn), lambda j, i: (i, j)))
        args.append(residual)
        body = _mm_res_body
    nbytes = (2 * tm * k * 2 + 2 * k * tn * 4 + k * tn * 2
              + 6 * tm * tn * 4 + (4 << 20))
    return pl.pallas_call(
        body,
        grid=grid,
        in_specs=in_specs,
        out_specs=pl.BlockSpec((tm, tn), lambda j, i: (i, j)),
        out_shape=jax.ShapeDtypeStruct((m, n), out_dtype),
        scratch_shapes=[pltpu.VMEM((k, tn), BF16)],
        compiler_params=_params(("arbitrary", "arbitrary"), nbytes),
        name=name,
    )(*args)


def _out_norm_body(m_ref, w_ref, r_ref, g_ref, x_ref, h_ref, wbf_ref):
    @pl.when(pl.program_id(0) == 0)
    def _():
        wbf_ref[...] = w_ref[...].astype(BF16)

    x = r_ref[...] + jnp.dot(m_ref[...], wbf_ref[...], preferred_element_type=F32)
    x_ref[...] = x
    ms = jnp.mean(x * x, axis=-1, keepdims=True)
    h_ref[...] = (x * lax.rsqrt(ms + EPS) * g_ref[...]).astype(h_ref.dtype)


def _matmul_res_norm(a, w, layer, residual, g, *, tm, name):
    m, k = a.shape
    n = w.shape[2]
    nbytes = (k * n * 4 + k * n * 2 + 2 * tm * k * 2 + 4 * tm * n * 4
              + 2 * tm * n * 2 + 2 * tm * n * 4 + (4 << 20))
    return pl.pallas_call(
        _out_norm_body,
        grid=(m // tm,),
        in_specs=[pl.BlockSpec((tm, k), lambda i: (i, 0)),
                  pl.BlockSpec((None, k, n), lambda i: (layer, 0, 0),
                               pipeline_mode=pl.Buffered(1)),
                  pl.BlockSpec((tm, n), lambda i: (i, 0)),
                  pl.BlockSpec((None, 1, n), lambda i: (layer, 0, 0))],
        out_specs=[pl.BlockSpec((tm, n), lambda i: (i, 0)),
                   pl.BlockSpec((tm, n), lambda i: (i, 0))],
        out_shape=[jax.ShapeDtypeStruct((m, n), F32),
                   jax.ShapeDtypeStruct((m, n), BF16)],
        scratch_shapes=[pltpu.VMEM((k, n), BF16)],
        compiler_params=_params(("arbitrary",), nbytes),
        name=name,
    )(a, w, residual, g)


def _merge_body(*refs, n_gate, tg):
    oa_ref, ob_ref = refs[:2]
    ga_refs = refs[2:2 + n_gate]
    gb_refs = refs[2 + n_gate:2 + 2 * n_gate]
    wa_ref, wb_ref, o_ref, wabf_ref, wbbf_ref = refs[2 + 2 * n_gate:]

    @pl.when(pl.program_id(0) == 0)
    def _():
        wabf_ref[...] = wa_ref[...].astype(BF16)
        wbbf_ref[...] = wb_ref[...].astype(BF16)

    oa = oa_ref[...]
    ob = ob_ref[...]
    for c in range(n_gate):
        cs = slice(c * tg, (c + 1) * tg)
        a = jnp.dot(oa, wabf_ref[:, cs], preferred_element_type=F32)
        b = jnp.dot(ob, wbbf_ref[:, cs], preferred_element_type=F32)
        ga = jax.nn.sigmoid(ga_refs[c][...].astype(F32))
        gb = jax.nn.sigmoid(gb_refs[c][...].astype(F32))
        o_ref[:, cs] = (ga * a + gb * b).astype(o_ref.dtype)


def _merge(oa, ob, z, wa, wb, layer, *, ga_col, gb_col, tm, tg, name):
    m, k = oa.shape
    n = wa.shape[2]
    assert ga_col % tg == 0 and gb_col % tg == 0 and n % tg == 0
    n_gate = n // tg
    nbytes = (2 * k * n * 4 + 2 * k * n * 2 + 4 * tm * k * 2 + 4 * tm * n * 2
              + 2 * tm * n * 2 + 6 * tm * tg * 4 + (4 << 20))

    def gate_spec(blk):
        return pl.BlockSpec((tm, tg), lambda i: (i, blk))

    def gate_specs(col):
        return [gate_spec(col // tg + c) for c in range(n_gate)]

    def wspec():
        return pl.BlockSpec((None, k, n), lambda i: (layer, 0, 0), pipeline_mode=pl.Buffered(1))

    return pl.pallas_call(
        functools.partial(_merge_body, n_gate=n_gate, tg=tg),
        grid=(m // tm,),
        in_specs=[pl.BlockSpec((tm, k), lambda i: (i, 0)),
                  pl.BlockSpec((tm, k), lambda i: (i, 0))]
        + gate_specs(ga_col) + gate_specs(gb_col) + [wspec(), wspec()],
        out_specs=pl.BlockSpec((tm, n), lambda i: (i, 0)),
        out_shape=jax.ShapeDtypeStruct((m, n), BF16),
        scratch_shapes=[pltpu.VMEM((k, n), BF16), pltpu.VMEM((k, n), BF16)],
        compiler_params=_params(("arbitrary",), nbytes),
        name=name,
    )(oa, ob, *([z] * (2 * n_gate)), wa, wb)


def _rope_tables(pos):
    half = A_HEAD_DIM // 2
    inv = ROPE_THETA ** (-2.0 * jnp.arange(half, dtype=F32) / A_HEAD_DIM)
    ang = pos[:, None] * inv[None, :]
    c, s = jnp.cos(ang), jnp.sin(ang)
    cos = jnp.concatenate([c, c, c, c], axis=-1)
    sin = jnp.concatenate([-s, s, -s, s], axis=-1)
    return cos, sin


def _retrot_tables(pos):
    half = R_DIM // 2
    inv = 1.0 / (10000.0 ** jnp.linspace(0.0, 1.0, half, dtype=F32))
    ang = pos[:, None] * inv[None, :]
    c, s = jnp.cos(ang), jnp.sin(ang)
    cos = jnp.stack([c, c], axis=-1).reshape(pos.shape[0], R_DIM)
    sin = jnp.stack([-s, s], axis=-1).reshape(pos.shape[0], R_DIM)
    return cos, sin


def _rope128(x, cos, sin):
    lane = lax.broadcasted_iota(jnp.int32, x.shape, 1)
    first = (lane & (A_HEAD_DIM - 1)) < (A_HEAD_DIM // 2)
    partner = jnp.where(first, pltpu.roll(x, LANES - 32, 1), pltpu.roll(x, 32, 1))
    return x * cos + partner * sin


def _retrot128(x, cos, sin):
    lane = lax.broadcasted_iota(jnp.int32, x.shape, 1)
    even = (lane & 1) == 0
    partner = jnp.where(even, pltpu.roll(x, LANES - 1, 1), pltpu.roll(x, 1, 1))
    return x * cos + partner * sin


def _swa_heads(q_chunks, score_fn, out_fn, sink_ref, layer):
    rows = q_chunks[0].shape[0]
    lane = lax.broadcasted_iota(jnp.int32, (rows, LANES), 1)
    low = lane < A_HEAD_DIM
    outs = []
    for pair in range(A_HEADS // 2):
        halves = []
        for hpos in range(2):
            h = 2 * pair + hpos
            kv = h // A_GROUP
            c, kpos = kv // 2, kv % 2
            qc = q_chunks[pair]
            if hpos != kpos:
                qc = pltpu.roll(qc, A_HEAD_DIM, 1)
            keep = low if kpos == 0 else jnp.logical_not(low)
            qm = jnp.where(keep, qc, 0.0).astype(BF16)
            s = score_fn(qm, c)
            sink = sink_ref[layer, h]
            m = jnp.maximum(jnp.max(s, axis=-1, keepdims=True), sink)
            p = jnp.exp(s - m)
            pv, psum = out_fn(p, c)
            o = pv / (psum + jnp.exp(sink - m))
            if hpos != kpos:
                o = pltpu.roll(o, A_HEAD_DIM, 1)
            halves.append(o)
        outs.append(jnp.where(low, halves[0], halves[1]))
    return outs


def _swa_prompt_body(sink_ref, q_ref, k_ref, vc_ref, vp_ref, cos_ref, sin_ref,
                     o_ref, kr_ref, kprev_ref, *, layer):
    n = pl.program_id(1)
    cos = cos_ref[...]
    sin = sin_ref[...]

    @pl.when(n == 0)
    def _():
        kprev_ref[...] = jnp.zeros_like(kprev_ref)

    k_cur, k_both, v_both = [], [], []
    for c in range(2):
        sl = slice(c * LANES, (c + 1) * LANES)
        kc = _rope128(k_ref[:, sl].astype(F32), cos, sin)
        kr_ref[:, sl] = kc
        k_cur.append(kc.astype(BF16))
        k_both.append(jnp.concatenate([kprev_ref[:, sl], k_cur[c]], axis=0))
        v_both.append(jnp.concatenate([vp_ref[:, sl], vc_ref[:, sl]], axis=0))

    scale = A_HEAD_DIM ** -0.5
    q_chunks = [
        _rope128(q_ref[:, c * LANES:(c + 1) * LANES].astype(F32), cos, sin) * scale
        for c in range(A_HEADS // 2)]

    qi = lax.broadcasted_iota(jnp.int32, (WINDOW, WINDOW), 0)
    kj = lax.broadcasted_iota(jnp.int32, (WINDOW, WINDOW), 1)
    from_cur = kj <= qi
    prev_bias = jnp.where(n > 0, 0.0, -jnp.inf)

    def scores(qm, c):
        s = lax.dot_general(qm, k_both[c], NT_DIMS, preferred_element_type=F32)
        return jnp.where(from_cur, s[:, WINDOW:], s[:, :WINDOW] + prev_bias)

    def weighted_values(p, c):
        p_both = jnp.concatenate([jnp.where(from_cur, 0.0, p), jnp.where(from_cur, p, 0.0)],
                                 axis=1).astype(BF16)
        return (jnp.dot(p_both, v_both[c], preferred_element_type=F32),
                jnp.sum(p, axis=-1, keepdims=True))

    outs = _swa_heads(q_chunks, scores, weighted_values, sink_ref, layer)
    for c, o in enumerate(outs):
        o_ref[:, c * LANES:(c + 1) * LANES] = o.astype(o_ref.dtype)
    for c in range(2):
        kprev_ref[:, c * LANES:(c + 1) * LANES] = k_cur[c]


def _swa_prompt(z, sinks, layer, cos, sin, *, batch, seq, out_rows, name):
    nb = seq // WINDOW
    a_width = A_HEADS * A_HEAD_DIM
    kv_width = A_KV_HEADS * A_HEAD_DIM
    k_blk = a_width // kv_width
    v_blk = k_blk + 1
    rows = out_rows
    return pl.pallas_call(
        functools.partial(_swa_prompt_body, layer=layer),
        grid=(batch, nb),
        in_specs=[
            pl.BlockSpec(memory_space=pltpu.SMEM),
            pl.BlockSpec((WINDOW, a_width), lambda b, n: (b * nb + n, 0)),
            pl.BlockSpec((WINDOW, kv_width), lambda b, n: (b * nb + n, k_blk)),
            pl.BlockSpec((WINDOW, kv_width), lambda b, n: (b * nb + n, v_blk)),
            pl.BlockSpec((WINDOW, kv_width),
                         lambda b, n: (b * nb + jnp.maximum(n - 1, 0), v_blk)),
            pl.BlockSpec((WINDOW, LANES), lambda b, n: (n, 0)),
            pl.BlockSpec((WINDOW, LANES), lambda b, n: (n, 0)),
        ],
        out_specs=[
            pl.BlockSpec((WINDOW, a_width), lambda b, n: (b * nb + n, 0)),
            pl.BlockSpec((WINDOW, kv_width), lambda b, n: (b, 0)),
        ],
        out_shape=[jax.ShapeDtypeStruct((rows, a_width), BF16),
                   jax.ShapeDtypeStruct((batch * WINDOW, kv_width), F32)],
        scratch_shapes=[pltpu.VMEM((WINDOW, kv_width), BF16)],
        compiler_params=_params(("arbitrary", "arbitrary"), 0),
        name=name,
    )(sinks, z, z, z, z, cos, sin)


SWA_SAMPLE_SEQS = 8


def _swa_sample_body(sink_ref, q_ref, k_ref, v_ref, ck_ref, cv_ref, cos_ref,
                     sin_ref, oa_hbm_ref, o_ref, kr_ref, *, dec_seq, layer):
    del oa_hbm_ref
    cos = cos_ref[...]
    sin = sin_ref[...]
    g = SWA_SAMPLE_SEQS
    rows = g * dec_seq
    ncache = g * WINDOW
    k_chunks, v_chunks = [], []
    for c in range(2):
        sl = slice(c * LANES, (c + 1) * LANES)
        kc = _rope128(k_ref[:, sl].astype(F32), cos, sin)
        kr_ref[:, sl] = kc
        k_chunks.append(jnp.concatenate(
            [ck_ref[:, sl].astype(BF16), kc.astype(BF16)], axis=0))
        v_chunks.append(jnp.concatenate(
            [cv_ref[:, sl].astype(BF16), v_ref[:, sl]], axis=0))

    scale = A_HEAD_DIM ** -0.5
    q_chunks = [
        _rope128(q_ref[:, c * LANES:(c + 1) * LANES].astype(F32), cos, sin) * scale
        for c in range(A_HEADS // 2)]

    shape = (rows, ncache + rows)
    r = lax.broadcasted_iota(jnp.int32, shape, 0)
    col = lax.broadcasted_iota(jnp.int32, shape, 1)
    log_l = dec_seq.bit_length() - 1
    log_w = WINDOW.bit_length() - 1
    seq_q = r >> log_l
    i = r & (dec_seq - 1)
    in_cache = col < ncache
    cnew = col - ncache
    seq_k = jnp.where(in_cache, col >> log_w, cnew >> log_l)
    ok = ((in_cache & ((col & (WINDOW - 1)) > i))
          | (jnp.logical_not(in_cache) & ((cnew & (dec_seq - 1)) <= i)))
    valid = (seq_k == seq_q) & ok

    def scores(qm, c):
        s = lax.dot_general(qm, k_chunks[c], NT_DIMS, preferred_element_type=F32)
        return jnp.where(valid, s, -jnp.inf)

    def weighted_values(p, c):
        return (jnp.dot(p.astype(BF16), v_chunks[c], preferred_element_type=F32),
                jnp.sum(p, axis=-1, keepdims=True))

    outs = _swa_heads(q_chunks, scores, weighted_values, sink_ref, layer)
    for c, o in enumerate(outs):
        o_ref[:, c * LANES:(c + 1) * LANES] = o.astype(o_ref.dtype)


def _swa_sample(z, cache_k, cache_v, sinks, layer, oa, cos, sin, *, row0, dec_batch, dec_seq,
                name):
    g = SWA_SAMPLE_SEQS
    rows = g * dec_seq
    a_width = A_HEADS * A_HEAD_DIM
    kv_width = A_KV_HEADS * A_HEAD_DIM
    k_blk = a_width // kv_width
    v_blk = k_blk + 1
    r0 = row0 // rows
    depth = cache_k.shape[0]
    ck = cache_k.reshape(depth, dec_batch * WINDOW, kv_width)
    cv = cache_v.reshape(depth, dec_batch * WINDOW, kv_width)
    return pl.pallas_call(
        functools.partial(_swa_sample_body, dec_seq=dec_seq, layer=layer),
        grid=(dec_batch // g,),
        in_specs=[
            pl.BlockSpec(memory_space=pltpu.SMEM),
            pl.BlockSpec((rows, a_width), lambda s: (r0 + s, 0)),
            pl.BlockSpec((rows, kv_width), lambda s: (r0 + s, k_blk)),
            pl.BlockSpec((rows, kv_width), lambda s: (r0 + s, v_blk)),
            pl.BlockSpec((None, g * WINDOW, kv_width), lambda s: (layer, s, 0)),
            pl.BlockSpec((None, g * WINDOW, kv_width), lambda s: (layer, s, 0)),
            pl.BlockSpec((rows, LANES), lambda s: (0, 0)),
            pl.BlockSpec((rows, LANES), lambda s: (0, 0)),
            pl.BlockSpec(memory_space=pl.ANY),
        ],
        out_specs=[
            pl.BlockSpec((rows, a_width), lambda s: (r0 + s, 0)),
            pl.BlockSpec((rows, kv_width), lambda s: (s, 0)),
        ],
        out_shape=[jax.ShapeDtypeStruct(oa.shape, oa.dtype),
                   jax.ShapeDtypeStruct((dec_batch * dec_seq, kv_width), F32)],
        input_output_aliases={8: 0},
        compiler_params=_params(("arbitrary",), 0),
        name=name,
    )(sinks, z, z, z, ck, cv, cos, sin, oa)


RET_HEADS_PER_STEP = 4
RET_BLOCK = RET_HEADS_PER_STEP * R_DIM
RET_CHUNKS_PER_STEP = 4


def _decay_tables(length, nseq):
    log_g = np.log1p(-np.exp2(-5.0 - np.arange(R_HEADS, dtype=np.float64)))
    i = np.arange(length, dtype=np.float64)
    diff = i[:, None] - i[None, :]
    d = np.where(diff >= 0, np.exp(log_g[:, None, None] * np.maximum(diff, 0.0)), 0.0)
    d = np.einsum('st,hij->hsitj', np.eye(nseq), d).reshape(
        R_HEADS, nseq * length, nseq * length)
    q_dec = np.exp(log_g[:, None] * (i[None, :] + 1.0))
    k_dec = np.exp(log_g[:, None] * (length - 1.0 - i)[None, :])
    shape = (R_HEADS, nseq * length, R_DIM)
    qd = np.broadcast_to(np.tile(q_dec, (1, nseq))[:, :, None], shape)
    kd = np.broadcast_to(np.tile(k_dec, (1, nseq))[:, :, None], shape)
    g_len = np.exp(log_g * length)
    return tuple(jnp.asarray(t, F32) for t in (d, qd, kd, g_len))


def _group_norm_gate(o, gate):
    o = o * lax.rsqrt(jnp.mean(o * o, axis=-1, keepdims=True) + EPS)
    return o * (gate * jax.nn.sigmoid(gate))


def _ret_prompt_body(gl_ref, *refs):
    nb = R_HEADS // RET_HEADS_PER_STEP
    q_refs, k_refs, v_refs, g_refs = (refs[t * nb:(t + 1) * nb] for t in range(4))
    cos_ref, sin_ref, d_ref, qd_ref, kd_ref, o_ref, s_ref = refs[4 * nb:]
    c = pl.program_id(1)

    @pl.when(c == 0)
    def _():
        s_ref[...] = jnp.zeros_like(s_ref)

    scale = R_DIM ** -0.5
    for ck in range(RET_CHUNKS_PER_STEP):
        rs = slice(ck * R_CHUNK, (ck + 1) * R_CHUNK)
        cos = cos_ref[rs, :]
        sin = sin_ref[rs, :]
        for h in range(R_HEADS):
            blk = h // RET_HEADS_PER_STEP
            sl = slice((h % RET_HEADS_PER_STEP) * R_DIM, (h % RET_HEADS_PER_STEP + 1) * R_DIM)
            q = _retrot128(q_refs[blk][rs, sl].astype(F32), cos, sin)
            k = _retrot128(k_refs[blk][rs, sl].astype(F32), cos, sin) * scale
            v = v_refs[blk][rs, sl]
            qb = q.astype(BF16)
            state = s_ref[0, h]
            scores = lax.dot_general(qb, k.astype(BF16), NT_DIMS,
                                     preferred_element_type=F32) * d_ref[h]
            o = jnp.dot(scores.astype(BF16), v, preferred_element_type=F32)
            cross = jnp.dot(qb, state.astype(BF16), preferred_element_type=F32)
            o = o + cross * qd_ref[h]
            kd = (k * kd_ref[h]).astype(BF16)
            s_ref[0, h] = gl_ref[h] * state + lax.dot_general(
                kd, v, TN_DIMS, preferred_element_type=F32)
            o_ref[rs, h * R_DIM:(h + 1) * R_DIM] = _group_norm_gate(
                o, g_refs[blk][rs, sl].astype(F32)).astype(o_ref.dtype)


def _ret_prompt(z, cos, sin, tabs, *, col0, batch, seq, out_rows, name):
    d, qd, kd, g_len = tabs
    rows = RET_CHUNKS_PER_STEP * R_CHUNK
    assert seq % rows == 0
    nc = seq // rows
    nb = R_HEADS // RET_HEADS_PER_STEP
    width = R_HEADS * R_DIM
    blks = [(col0 + t * width) // RET_BLOCK + i for t in range(4) for i in range(nb)]

    def zspec(blk):
        return pl.BlockSpec((rows, RET_BLOCK), lambda b, c: (b * nc + c, blk))

    def tspec():
        return pl.BlockSpec((R_HEADS, R_CHUNK, R_DIM), lambda b, c: (0, 0, 0))

    return pl.pallas_call(
        _ret_prompt_body,
        grid=(batch, nc),
        in_specs=[pl.BlockSpec(memory_space=pltpu.SMEM)]
        + [zspec(blk) for blk in blks]
        + [pl.BlockSpec((rows, R_DIM), lambda b, c: (c, 0)),
           pl.BlockSpec((rows, R_DIM), lambda b, c: (c, 0)),
           tspec(), tspec(), tspec()],
        out_specs=[
            pl.BlockSpec((rows, width), lambda b, c: (b * nc + c, 0)),
            pl.BlockSpec((1, R_HEADS, R_DIM, R_DIM), lambda b, c: (b, 0, 0, 0)),
        ],
        out_shape=[jax.ShapeDtypeStruct((out_rows, width), BF16),
                   jax.ShapeDtypeStruct((batch, R_HEADS, R_DIM, R_DIM), F32)],
        compiler_params=_params(("arbitrary", "arbitrary"), 0),
        name=name,
    )(g_len, *([z] * len(blks)), cos, sin, d, qd, kd)


def _ret_sample_body(gl_ref, q_ref, k_ref, v_ref, g_ref, cos_ref, sin_ref,
                     d_ref, qd_ref, kd_ref, s_ref, ob_hbm_ref, o_ref, so_ref, *, dec_seq):
    del ob_hbm_ref
    hh = pl.program_id(1)
    cos = cos_ref[...]
    sin = sin_ref[...]
    nseq = R_CHUNK // dec_seq
    log_l = dec_seq.bit_length() - 1
    row_seq = lax.broadcasted_iota(jnp.int32, (R_CHUNK, R_DIM), 0) >> log_l
    scale = R_DIM ** -0.5
    for hl in range(RET_HEADS_PER_STEP):
        sl = slice(hl * R_DIM, (hl + 1) * R_DIM)
        q = _retrot128(q_ref[:, sl].astype(F32), cos, sin)
        k = _retrot128(k_ref[:, sl].astype(F32), cos, sin) * scale
        v = v_ref[:, sl]
        qb = q.astype(BF16)
        scores = lax.dot_general(qb, k.astype(BF16), NT_DIMS,
                                 preferred_element_type=F32) * d_ref[hl]
        o = jnp.dot(scores.astype(BF16), v, preferred_element_type=F32)
        kd = k * kd_ref[hl]
        gl = gl_ref[hh * RET_HEADS_PER_STEP + hl]
        cross = jnp.zeros((R_CHUNK, R_DIM), F32)
        for s in range(nseq):
            mine = row_seq == s
            state = s_ref[s, hl]
            cs = jnp.dot(qb, state.astype(BF16), preferred_element_type=F32)
            cross = jnp.where(mine, cs, cross)
            ks = jnp.where(mine, kd, 0.0).astype(BF16)
            so_ref[s, hl] = gl * state + lax.dot_general(
                ks, v, TN_DIMS, preferred_element_type=F32)
        o = o + cross * qd_ref[hl]
        o_ref[:, sl] = _group_norm_gate(o, g_ref[:, sl].astype(F32)).astype(o_ref.dtype)


def _ret_sample(z, state, layer, ob, cos, sin, tabs, *, row0, col0, dec_batch, dec_seq, name):
    d, qd, kd, g_len = tabs
    nseq = R_CHUNK // dec_seq
    steps = dec_batch // nseq
    nh = R_HEADS // RET_HEADS_PER_STEP
    width = R_HEADS * R_DIM
    qb, kb, vb, gb = [(col0 + t * width) // RET_BLOCK for t in range(4)]
    r0 = row0 // R_CHUNK
    hp = RET_HEADS_PER_STEP

    def zspec(blk):
        return pl.BlockSpec((R_CHUNK, RET_BLOCK), lambda s, h: (r0 + s, blk + h))

    def tspec():
        return pl.BlockSpec((hp, R_CHUNK, R_DIM), lambda s, h: (h, 0, 0))

    sspec_in = pl.BlockSpec((None, nseq, hp, R_DIM, R_DIM), lambda s, h: (layer, s, h, 0, 0))
    sspec = pl.BlockSpec((nseq, hp, R_DIM, R_DIM), lambda s, h: (s, h, 0, 0))
    return pl.pallas_call(
        functools.partial(_ret_sample_body, dec_seq=dec_seq),
        grid=(steps, nh),
        in_specs=[
            pl.BlockSpec(memory_space=pltpu.SMEM),
            zspec(qb), zspec(kb), zspec(vb), zspec(gb),
            pl.BlockSpec((R_CHUNK, R_DIM), lambda s, h: (0, 0)),
            pl.BlockSpec((R_CHUNK, R_DIM), lambda s, h: (0, 0)),
            tspec(), tspec(), tspec(),
            sspec_in,
            pl.BlockSpec(memory_space=pl.ANY),
        ],
        out_specs=[
            pl.BlockSpec((R_CHUNK, RET_BLOCK), lambda s, h: (r0 + s, h)),
            sspec,
        ],
        out_shape=[jax.ShapeDtypeStruct(ob.shape, ob.dtype),
                   jax.ShapeDtypeStruct(state.shape[1:], F32)],
        input_output_aliases={11: 0},
        compiler_params=_params(("arbitrary", "arbitrary"),
                                4 * nseq * hp * R_DIM * R_DIM * 4 + (16 << 20)),
        name=name,
    )(g_len, z, z, z, z, cos, sin, d, qd, kd, state, ob)


def _gelu_tanh_times(x, gate):
    c = (2.0 / jnp.pi) ** 0.5
    t = jnp.tanh(x * (c + (c * 0.044715) * (x * x)))
    y = (0.5 * x) * gate
    return y + y * t


def _ffn_up_body(h_ref, wu_ref, wg_ref, cw_ref, cb_ref, e0_ref, e1_ref,
                 f_ref, tail_ref, us_ref, wubf_ref, wgbf_ref, carry_ref, ug_ref,
                 *, tm, n_tiles, seq, batch, ms, dec_seq):
    i = pl.program_id(1)
    tn = f_ref.shape[1]
    mp = batch * seq

    @pl.when(i == 0)
    def _():
        wubf_ref[...] = wu_ref[...].astype(BF16)
        wgbf_ref[...] = wg_ref[...].astype(BF16)
        carry_ref[...] = jnp.zeros_like(carry_ref)

    last = n_tiles - 1
    s0 = mp - last * tm
    assert 0 <= s0 and s0 + ms == tm and s0 % FIX_ROWS == 0
    starts = [k * seq for k in range(1, batch)]
    for r in starts:
        assert r % tm != 0 and r % FIX_ROWS == 0, "mid-tile, packed-tile aligned starts only"

    def conv_gate(u, gate, p1, p2, w, bias):
        conv = bias + w[0:1, :] * p2
        conv = conv + w[1:2, :] * p1
        conv = conv + w[2:3, :] * u
        return _gelu_tanh_times(conv, gate).astype(f_ref.dtype)

    tr = tm // FFN_ROW_PARTS
    assert tm % FFN_ROW_PARTS == 0 and tr % FIX_ROWS == 0
    for c in range(tn // MXU_COLS):
        cs = slice(c * MXU_COLS, (c + 1) * MXU_COLS)
        w = cw_ref[:, cs]
        bias = cb_ref[:, cs]
        for part in range(FFN_ROW_PARTS):
            rs = slice(part * tr, (part + 1) * tr)
            h = h_ref[rs, :]
            ug_ref[2 * c, rs, :] = jnp.dot(h, wubf_ref[:, cs], preferred_element_type=F32)
            ug_ref[2 * c + 1, rs, :] = jnp.dot(h, wgbf_ref[:, cs], preferred_element_type=F32)
            u = ug_ref[2 * c, rs, :]
            gate = ug_ref[2 * c + 1, rs, :]

            if part == 0:
                before = carry_ref[:, cs]
            else:
                before = ug_ref[2 * c, part * tr - SUBLANES:part * tr, :]
            last1 = before[SUBLANES - 1:SUBLANES, :]
            last2 = before[SUBLANES - 2:SUBLANES - 1, :]
            f_ref[rs, cs] = conv_gate(u, gate, pltpu.roll(u, 1, 0), pltpu.roll(u, 2, 0),
                                      w, bias)
            uh = u[:FIX_ROWS, :]
            pos = lax.broadcasted_iota(jnp.int32, uh.shape, 0)
            prev1 = jnp.where(pos == 0, last1, pltpu.roll(uh, 1, 0))
            prev2 = jnp.where(pos == 0, last2,
                              jnp.where(pos == 1, last1, pltpu.roll(uh, 2, 0)))
            f_ref[part * tr:part * tr + FIX_ROWS, cs] = conv_gate(
                uh, gate[:FIX_ROWS, :], prev1, prev2, w, bias)
        carry_ref[:, cs] = ug_ref[2 * c, tm - SUBLANES:, :]

    def rewrite(rows, p1_fn, p2_fn):
        for c in range(tn // MXU_COLS):
            cs = slice(c * MXU_COLS, (c + 1) * MXU_COLS)
            us = ug_ref[2 * c, rows, :]
            f_ref[rows, cs] = conv_gate(us, ug_ref[2 * c + 1, rows, :], p1_fn(us, cs),
                                        p2_fn(us, cs), cw_ref[:, cs], cb_ref[:, cs])

    for r in starts:
        @pl.when(i == r // tm)
        def _(off=r % tm):
            pos = lax.broadcasted_iota(jnp.int32, (FIX_ROWS, MXU_COLS), 0)
            rewrite(slice(off, off + FIX_ROWS),
                    lambda us, cs: jnp.where(pos == 0, 0.0, pltpu.roll(us, 1, 0)),
                    lambda us, cs: jnp.where(pos <= 1, 0.0, pltpu.roll(us, 2, 0)))

    @pl.when(i == last)
    def _():
        pos = lax.broadcasted_iota(jnp.int32, (ms, MXU_COLS), 0) & (dec_seq - 1)
        rewrite(slice(s0, tm),
                lambda us, cs: jnp.where(pos == 0, e1_ref[:, cs], pltpu.roll(us, 1, 0)),
                lambda us, cs: jnp.where(
                    pos == 0, e0_ref[:, cs],
                    jnp.where(pos == 1, e1_ref[:, cs], pltpu.roll(us, 2, 0))))
        for c in range(tn // MXU_COLS):
            us_ref[:, c * MXU_COLS:(c + 1) * MXU_COLS] = ug_ref[2 * c, s0:, :]

    for b in range(batch):
        r = (b + 1) * seq - SUBLANES

        @pl.when(i == r // tm)
        def _(b=b, off=r % tm):
            for c in range(tn // MXU_COLS):
                tail_ref[b * SUBLANES:(b + 1) * SUBLANES, c * MXU_COLS:(c + 1) * MXU_COLS] = (
                    ug_ref[2 * c, off:off + SUBLANES, :])


def _ffn_up(h, w_up, conv_w, conv_b, layer, e, *, tm, tn, seq, batch, dec_seq, name):
    m, k = h.shape
    d_ff = w_up.shape[2] // 2
    ms = e.shape[2]
    nj = d_ff // tn
    nt = m // tm
    body = functools.partial(_ffn_up_body, tm=tm, n_tiles=nt, seq=seq, batch=batch,
                             ms=ms, dec_seq=dec_seq)
    nbytes = (2 * tm * k * 2 + 4 * k * tn * 4 + 2 * k * tn * 2
              + 16 * tm * tn * 4 + 4 * ms * tn * 4 + (4 << 20))
    return pl.pallas_call(
        body,
        grid=(nj, nt),
        in_specs=[
            pl.BlockSpec((tm, k), lambda j, i: (i, 0)),
            pl.BlockSpec((None, k, tn), lambda j, i: (layer, 0, j)),
            pl.BlockSpec((None, k, tn), lambda j, i: (layer, 0, nj + j)),
            pl.BlockSpec((None, CONV_W, tn), lambda j, i: (layer, 0, j)),
            pl.BlockSpec((None, 1, tn), lambda j, i: (layer, 0, j)),
            pl.BlockSpec((None, None, ms, tn), lambda j, i: (layer, 0, 0, j)),
            pl.BlockSpec((None, None, ms, tn), lambda j, i: (layer, 1, 0, j)),
        ],
        out_specs=[
            pl.BlockSpec((tm, tn), lambda j, i: (i, j)),
            pl.BlockSpec((batch * SUBLANES, tn), lambda j, i: (0, j)),
            pl.BlockSpec((ms, tn), lambda j, i: (0, j)),
        ],
        out_shape=[jax.ShapeDtypeStruct((m, d_ff), BF16),
                   jax.ShapeDtypeStruct((batch * SUBLANES, d_ff), F32),
                   jax.ShapeDtypeStruct((ms, d_ff), F32)],
        scratch_shapes=[pltpu.VMEM((k, tn), BF16), pltpu.VMEM((k, tn), BF16),
                        pltpu.VMEM((SUBLANES, tn), F32),
                        pltpu.VMEM((2 * tn // MXU_COLS, tm, MXU_COLS), F32)],
        compiler_params=_params(("arbitrary", "arbitrary"), nbytes),
        name=name,
    )(h, w_up, w_up, conv_w, conv_b, e, e)


TM = 1408
TM_NORM = 768
TM_DOWN = 704
TM_IN = 2816
TN_IN = 512
TN_UP = 512
TM_MERGE = 352
GATE_BLOCK = 512
TM_OUT = 352
FFN_ROW_PARTS = 2
TN_OUT = 512


def kernel(x_prompt, x_sample, cache_win_k, cache_win_v, state_ret, state_conv,
           g_mix, w_in, sinks, w_proj_a, w_proj_b, w_o, g_ffn, w_up, conv_w,
           conv_b, w_down, g_final):
    batch, seq, d_model = x_prompt.shape
    dec_batch, dec_seq, _ = x_sample.shape
    depth = w_in.shape[0]
    d_ff = w_down.shape[1]
    mp = batch * seq
    ms = dec_batch * dec_seq
    a_width = A_HEADS * A_HEAD_DIM
    kv_width = A_KV_HEADS * A_HEAD_DIM
    r_width = R_HEADS * R_DIM
    ret_col0 = a_width + 2 * kv_width
    ga_col = ret_col0 + 4 * r_width
    gb_col = ga_col + d_model
    assert dec_seq & (dec_seq - 1) == 0 and R_CHUNK % dec_seq == 0

    pos_p = jnp.arange(seq, dtype=F32)
    pos_s = PAST_LEN + jnp.arange(dec_seq, dtype=F32)
    rope_p = _rope_tables(pos_p)
    rope_s = tuple(jnp.tile(t, (SWA_SAMPLE_SEQS, 1)) for t in _rope_tables(pos_s))
    rrot_p = _retrot_tables(pos_p)
    nseq = R_CHUNK // dec_seq
    rrot_s = tuple(jnp.tile(t, (nseq, 1)) for t in _retrot_tables(pos_s))
    tabs_p = _decay_tables(R_CHUNK, 1)
    tabs_s = _decay_tables(dec_seq, nseq)
    g_mix3 = g_mix.reshape(depth, 1, d_model)
    g_ffn3 = g_ffn.reshape(depth, 1, d_model)
    conv_b3 = conv_b.reshape(depth, 1, d_ff)

    e_conv = jnp.repeat(jnp.swapaxes(state_conv, 1, 2), dec_seq, axis=2)
    v_col0 = a_width + kv_width

    kp_l, vp_l, sp_l, cp_l = [], [], [], []
    ks_l, vs_l, ss_l, cs_l = [], [], [], []
    for l in range(depth):
        if l == 0:
            x, h = _join_norm(x_prompt.reshape(mp, d_model), x_sample.reshape(ms, d_model),
                              g_mix3, l, name="join_norm_mix0")
        else:
            h = _rmsnorm(x, g_mix3, l, BF16, tm=TM_NORM, name=f"norm_mix{l}")
        z = _matmul(h, w_in, l, tm=TM_IN, tn=TN_IN, out_dtype=BF16, name=f"proj_in{l}")

        oa, kr_p = _swa_prompt(z, sinks, l, *rope_p, batch=batch, seq=seq,
                               out_rows=mp + ms, name=f"swa_prompt{l}")
        oa, kr_s = _swa_sample(z, cache_win_k, cache_win_v, sinks, l, oa, *rope_s,
                               row0=mp, dec_batch=dec_batch, dec_seq=dec_seq,
                               name=f"swa_sample{l}")
        ob, s_p = _ret_prompt(z, *rrot_p, tabs_p, col0=ret_col0, batch=batch,
                              seq=seq, out_rows=mp + ms, name=f"ret_prompt{l}")
        ob, s_s = _ret_sample(z, state_ret, l, ob, *rrot_s, tabs_s, row0=mp,
                              col0=ret_col0, dec_batch=dec_batch, dec_seq=dec_seq,
                              name=f"ret_sample{l}")
        merged = _merge(oa, ob, z, w_proj_a, w_proj_b, l, ga_col=ga_col,
                        gb_col=gb_col, tm=TM_MERGE, tg=GATE_BLOCK, name=f"merge{l}")
        x, h2 = _matmul_res_norm(merged, w_o, l, x, g_ffn3, tm=TM_OUT,
                                 name=f"proj_out_norm{l}")
        f, u_tail, u_s = _ffn_up(h2, w_up, conv_w, conv_b3, l, e_conv, tm=TM, tn=TN_UP,
                                 seq=seq, batch=batch, dec_seq=dec_seq, name=f"ffn_up{l}")
        x = _matmul(f, w_down, l, tm=TM_DOWN, tn=TN_OUT, out_dtype=F32, residual=x,
                    name=f"proj_down{l}")

        v_p = jnp.stack([lax.slice(z, ((b + 1) * seq - WINDOW, v_col0), ((b + 1) * seq, ret_col0))
                         for b in range(batch)])
        v_s = lax.slice(z, (mp, v_col0), (mp + ms, ret_col0))
        kp_l.append(kr_p.reshape(batch, WINDOW, A_KV_HEADS, A_HEAD_DIM))
        vp_l.append(v_p.astype(F32).reshape(batch, WINDOW, A_KV_HEADS, A_HEAD_DIM))
        sp_l.append(s_p)
        cp_l.append(u_tail.reshape(batch, SUBLANES, d_ff)[:, -(CONV_W - 1):])
        ks_l.append(kr_s.reshape(dec_batch, dec_seq, A_KV_HEADS, A_HEAD_DIM))
        vs_l.append(v_s.astype(F32).reshape(dec_batch, dec_seq, A_KV_HEADS, A_HEAD_DIM))
        ss_l.append(s_s)
        cs_l.append(u_s.reshape(dec_batch, dec_seq, d_ff)[:, -(CONV_W - 1):])

    y_prompt, y_sample = _split_norm(x, g_final, mp=mp, name="norm_final")
    y_prompt = y_prompt.reshape(batch, seq, d_model)
    y_sample = y_sample.reshape(dec_batch, dec_seq, d_model)
    return (y_prompt, y_sample,
            jnp.stack(kp_l), jnp.stack(vp_l), jnp.stack(sp_l), jnp.stack(cp_l),
            jnp.stack(ks_l), jnp.stack(vs_l), jnp.stack(ss_l), jnp.stack(cs_l))
```

```python
import functools

import jax
import jax.numpy as jnp
import numpy as np
from jax import lax
from jax.experimental import pallas as pl
from jax.experimental.pallas import tpu as pltpu

F32 = jnp.float32
BF16 = jnp.bfloat16

LANES = 128
SUBLANES = 8
MXU_COLS = 256
FIX_ROWS = 2 * SUBLANES
VMEM_LIMIT_CAP = 56 * 1024 * 1024

WINDOW = 128
A_HEADS = 16
A_KV_HEADS = 4
A_HEAD_DIM = 64
A_GROUP = A_HEADS // A_KV_HEADS
R_HEADS = 8
R_DIM = 128
R_CHUNK = 128
CONV_W = 3
EPS = 1e-6
ROPE_THETA = 10000.0
PAST_LEN = 16384

NT_DIMS = (((1,), (1,)), ((), ()))
TN_DIMS = (((0,), (0,)), ((), ()))


def _vmem_limit(nbytes):
    return int(min(VMEM_LIMIT_CAP, max(32 * 1024 * 1024, nbytes)))


def _params(semantics, nbytes):
    return pltpu.CompilerParams(dimension_semantics=semantics,
                                vmem_limit_bytes=_vmem_limit(nbytes))


def _rmsnorm_body(x_ref, g_ref, o_ref):
    x = x_ref[...]
    ms = jnp.mean(x * x, axis=-1, keepdims=True)
    o_ref[...] = (x * lax.rsqrt(ms + EPS) * g_ref[...]).astype(o_ref.dtype)


def _rmsnorm(x, g, layer, out_dtype, *, tm, name):
    m, d = x.shape
    return pl.pallas_call(
        _rmsnorm_body,
        grid=(m // tm,),
        in_specs=[pl.BlockSpec((tm, d), lambda i: (i, 0)),
                  pl.BlockSpec((None, 1, d), lambda i: (layer, 0, 0))],
        out_specs=pl.BlockSpec((tm, d), lambda i: (i, 0)),
        out_shape=jax.ShapeDtypeStruct((m, d), out_dtype),
        compiler_params=_params(("arbitrary",), 6 * tm * d * 4),
        name=name,
    )(x, g)


def _join_norm_body(xp_ref, xs_ref, g_ref, x_ref, h_ref, *, prompt_tiles):
    i = pl.program_id(0)

    def emit(x):
        ms = jnp.mean(x * x, axis=-1, keepdims=True)
        x_ref[...] = x
        h_ref[...] = (x * lax.rsqrt(ms + EPS) * g_ref[...]).astype(h_ref.dtype)

    @pl.when(i < prompt_tiles)
    def _():
        emit(xp_ref[...])

    @pl.when(i >= prompt_tiles)
    def _():
        emit(xs_ref[...])


def _join_norm(xp, xs, g, layer, *, name):
    mp, d = xp.shape
    ms = xs.shape[0]
    assert mp % ms == 0
    pt = mp // ms
    return pl.pallas_call(
        functools.partial(_join_norm_body, prompt_tiles=pt),
        grid=(pt + 1,),
        in_specs=[pl.BlockSpec((ms, d), lambda i: (jnp.minimum(i, pt - 1), 0)),
                  pl.BlockSpec((ms, d), lambda i: (0, 0)),
                  pl.BlockSpec((None, 1, d), lambda i: (layer, 0, 0))],
        out_specs=[pl.BlockSpec((ms, d), lambda i: (i, 0)),
                   pl.BlockSpec((ms, d), lambda i: (i, 0))],
        out_shape=[jax.ShapeDtypeStruct((mp + ms, d), F32),
                   jax.ShapeDtypeStruct((mp + ms, d), BF16)],
        compiler_params=_params(("arbitrary",), 0),
        name=name,
    )(xp, xs, g)


def _split_norm_body(x_ref, g_ref, yp_ref, ys_ref, *, prompt_tiles):
    i = pl.program_id(0)
    x = x_ref[...]
    ms = jnp.mean(x * x, axis=-1, keepdims=True)
    y = x * lax.rsqrt(ms + EPS) * g_ref[...]

    @pl.when(i < prompt_tiles)
    def _():
        yp_ref[...] = y

    @pl.when(i >= prompt_tiles)
    def _():
        ys_ref[...] = y


def _split_norm(x, g, *, mp, name):
    m, d = x.shape
    ms = m - mp
    assert mp % ms == 0
    pt = mp // ms
    return pl.pallas_call(
        functools.partial(_split_norm_body, prompt_tiles=pt),
        grid=(pt + 1,),
        in_specs=[pl.BlockSpec((ms, d), lambda i: (i, 0)),
                  pl.BlockSpec((1, d), lambda i: (0, 0))],
        out_specs=[pl.BlockSpec((ms, d), lambda i: (jnp.minimum(i, pt - 1), 0)),
                   pl.BlockSpec((ms, d), lambda i: (0, 0))],
        out_shape=[jax.ShapeDtypeStruct((mp, d), F32),
                   jax.ShapeDtypeStruct((ms, d), F32)],
        compiler_params=_params(("arbitrary",), 0),
        name=name,
    )(x, g.reshape(1, d))


def _mm_body(x_ref, w_ref, o_ref, wbf_ref):
    @pl.when(pl.program_id(1) == 0)
    def _():
        wbf_ref[...] = w_ref[...].astype(BF16)

    o_ref[...] = jnp.dot(x_ref[...], wbf_ref[...],
                         preferred_element_type=F32).astype(o_ref.dtype)


def _mm_res_body(x_ref, w_ref, r_ref, o_ref, wbf_ref):
    @pl.when(pl.program_id(1) == 0)
    def _():
        wbf_ref[...] = w_ref[...].astype(BF16)

    o_ref[...] = r_ref[...] + jnp.dot(x_ref[...], wbf_ref[...],
                                      preferred_element_type=F32)


def _matmul(x, w, layer, *, tm, tn, out_dtype, residual=None, name):
    m, k = x.shape
    n = w.shape[2]
    grid = (n // tn, m // tm)
    in_specs = [pl.BlockSpec((tm, k), lambda j, i: (i, 0)),
                pl.BlockSpec((None, k, tn), lambda j, i: (layer, 0, j))]
    args = [x, w]
    body = _mm_body
    if residual is not None:
        in_specs.append(pl.BlockSpec((tm, tn), lambda j, i: (i, j)))
        args.append(residual)
        body = _mm_res_body
    nbytes = (2 * tm * k * 2 + 2 * k * tn * 4 + k * tn * 2
              + 6 * tm * tn * 4 + (4 << 20))
    return pl.pallas_call(
        body,
        grid=grid,
        in_specs=in_specs,
        out_specs=pl.BlockSpec((tm, tn), lambda j, i: (i, j)),
        out_shape=jax.ShapeDtypeStruct((m, n), out_dtype),
        scratch_shapes=[pltpu.VMEM((k, tn), BF16)],
        compiler_params=_params(("arbitrary", "arbitrary"), nbytes),
        name=name,
    )(*args)


def _out_norm_body(m_ref, w_ref, r_ref, g_ref, x_ref, h_ref, wbf_ref):
    @pl.when(pl.program_id(0) == 0)
    def _():
        wbf_ref[...] = w_ref[...].astype(BF16)

    x = r_ref[...] + jnp.dot(m_ref[...], wbf_ref[...], preferred_element_type=F32)
    x_ref[...] = x
    ms = jnp.mean(x * x, axis=-1, keepdims=True)
    h_ref[...] = (x * lax.rsqrt(ms + EPS) * g_ref[...]).astype(h_ref.dtype)


def _matmul_res_norm(a, w, layer, residual, g, *, tm, name):
    m, k = a.shape
    n = w.shape[2]
    nbytes = (k * n * 4 + k * n * 2 + 2 * tm * k * 2 + 4 * tm * n * 4
              + 2 * tm * n * 2 + 2 * tm * n * 4 + (4 << 20))
    return pl.pallas_call(
        _out_norm_body,
        grid=(m // tm,),
        in_specs=[pl.BlockSpec((tm, k), lambda i: (i, 0)),
                  pl.BlockSpec((None, k, n), lambda i: (layer, 0, 0),
                               pipeline_mode=pl.Buffered(1)),
                  pl.BlockSpec((tm, n), lambda i: (i, 0)),
                  pl.BlockSpec((None, 1, n), lambda i: (layer, 0, 0))],
        out_specs=[pl.BlockSpec((tm, n), lambda i: (i, 0)),
                   pl.BlockSpec((tm, n), lambda i: (i, 0))],
        out_shape=[jax.ShapeDtypeStruct((m, n), F32),
                   jax.ShapeDtypeStruct((m, n), BF16)],
        scratch_shapes=[pltpu.VMEM((k, n), BF16)],
        compiler_params=_params(("arbitrary",), nbytes),
        name=name,
    )(a, w, residual, g)


def _merge_body(*refs, n_gate, tg):
    oa_ref, ob_ref = refs[:2]
    ga_refs = refs[2:2 + n_gate]
    gb_refs = refs[2 + n_gate:2 + 2 * n_gate]
    wa_ref, wb_ref, o_ref, wabf_ref, wbbf_ref = refs[2 + 2 * n_gate:]

    @pl.when(pl.program_id(0) == 0)
    def _():
        wabf_ref[...] = wa_ref[...].astype(BF16)
        wbbf_ref[...] = wb_ref[...].astype(BF16)

    oa = oa_ref[...]
    ob = ob_ref[...]
    for c in range(n_gate):
        cs = slice(c * tg, (c + 1) * tg)
        a = jnp.dot(oa, wabf_ref[:, cs], preferred_element_type=F32)
        b = jnp.dot(ob, wbbf_ref[:, cs], preferred_element_type=F32)
        ga = jax.nn.sigmoid(ga_refs[c][...].astype(F32))
        gb = jax.nn.sigmoid(gb_refs[c][...].astype(F32))
        o_ref[:, cs] = (ga * a + gb * b).astype(o_ref.dtype)


def _merge(oa, ob, z, wa, wb, layer, *, ga_col, gb_col, tm, tg, name):
    m, k = oa.shape
    n = wa.shape[2]
    assert ga_col % tg == 0 and gb_col % tg == 0 and n % tg == 0
    n_gate = n // tg
    nbytes = (2 * k * n * 4 + 2 * k * n * 2 + 4 * tm * k * 2 + 4 * tm * n * 2
              + 2 * tm * n * 2 + 6 * tm * tg * 4 + (4 << 20))

    def gate_spec(blk):
        return pl.BlockSpec((tm, tg), lambda i: (i, blk))

    def gate_specs(col):
        return [gate_spec(col // tg + c) for c in range(n_gate)]

    def wspec():
        return pl.BlockSpec((None, k, n), lambda i: (layer, 0, 0), pipeline_mode=pl.Buffered(1))

    return pl.pallas_call(
        functools.partial(_merge_body, n_gate=n_gate, tg=tg),
        grid=(m // tm,),
        in_specs=[pl.BlockSpec((tm, k), lambda i: (i, 0)),
                  pl.BlockSpec((tm, k), lambda i: (i, 0))]
        + gate_specs(ga_col) + gate_specs(gb_col) + [wspec(), wspec()],
        out_specs=pl.BlockSpec((tm, n), lambda i: (i, 0)),
        out_shape=jax.ShapeDtypeStruct((m, n), BF16),
        scratch_shapes=[pltpu.VMEM((k, n), BF16), pltpu.VMEM((k, n), BF16)],
        compiler_params=_params(("arbitrary",), nbytes),
        name=name,
    )(oa, ob, *([z] * (2 * n_gate)), wa, wb)


def _rope_tables(pos):
    half = A_HEAD_DIM // 2
    inv = ROPE_THETA ** (-2.0 * jnp.arange(half, dtype=F32) / A_HEAD_DIM)
    ang = pos[:, None] * inv[None, :]
    c, s = jnp.cos(ang), jnp.sin(ang)
    cos = jnp.concatenate([c, c, c, c], axis=-1)
    sin = jnp.concatenate([-s, s, -s, s], axis=-1)
    return cos, sin


def _retrot_tables(pos):
    half = R_DIM // 2
    inv = 1.0 / (10000.0 ** jnp.linspace(0.0, 1.0, half, dtype=F32))
    ang = pos[:, None] * inv[None, :]
    c, s = jnp.cos(ang), jnp.sin(ang)
    cos = jnp.stack([c, c], axis=-1).reshape(pos.shape[0], R_DIM)
    sin = jnp.stack([-s, s], axis=-1).reshape(pos.shape[0], R_DIM)
    return cos, sin


def _rope128(x, cos, sin):
    lane = lax.broadcasted_iota(jnp.int32, x.shape, 1)
    first = (lane & (A_HEAD_DIM - 1)) < (A_HEAD_DIM // 2)
    partner = jnp.where(first, pltpu.roll(x, LANES - 32, 1), pltpu.roll(x, 32, 1))
    return x * cos + partner * sin


def _retrot128(x, cos, sin):
    lane = lax.broadcasted_iota(jnp.int32, x.shape, 1)
    even = (lane & 1) == 0
    partner = jnp.where(even, pltpu.roll(x, LANES - 1, 1), pltpu.roll(x, 1, 1))
    return x * cos + partner * sin


def _swa_heads(q_chunks, score_fn, out_fn, sink_ref, layer):
    rows = q_chunks[0].shape[0]
    lane = lax.broadcasted_iota(jnp.int32, (rows, LANES), 1)
    low = lane < A_HEAD_DIM
    outs = []
    for pair in range(A_HEADS // 2):
        halves = []
        for hpos in range(2):
            h = 2 * pair + hpos
            kv = h // A_GROUP
            c, kpos = kv // 2, kv % 2
            qc = q_chunks[pair]
            if hpos != kpos:
                qc = pltpu.roll(qc, A_HEAD_DIM, 1)
            keep = low if kpos == 0 else jnp.logical_not(low)
            qm = jnp.where(keep, qc, 0.0).astype(BF16)
            s = score_fn(qm, c)
            sink = sink_ref[layer, h]
            m = jnp.maximum(jnp.max(s, axis=-1, keepdims=True), sink)
            p = jnp.exp(s - m)
            pv, psum = out_fn(p, c)
            o = pv / (psum + jnp.exp(sink - m))
            if hpos != kpos:
                o = pltpu.roll(o, A_HEAD_DIM, 1)
            halves.append(o)
        outs.append(jnp.where(low, halves[0], halves[1]))
    return outs


def _swa_prompt_body(sink_ref, q_ref, k_ref, vc_ref, vp_ref, cos_ref, sin_ref,
                     o_ref, kr_ref, kprev_ref, *, layer):
    n = pl.program_id(1)
    cos = cos_ref[...]
    sin = sin_ref[...]

    @pl.when(n == 0)
    def _():
        kprev_ref[...] = jnp.zeros_like(kprev_ref)

    k_cur, k_both, v_both = [], [], []
    for c in range(2):
        sl = slice(c * LANES, (c + 1) * LANES)
        kc = _rope128(k_ref[:, sl].astype(F32), cos, sin)
        kr_ref[:, sl] = kc
        k_cur.append(kc.astype(BF16))
        k_both.append(jnp.concatenate([kprev_ref[:, sl], k_cur[c]], axis=0))
        v_both.append(jnp.concatenate([vp_ref[:, sl], vc_ref[:, sl]], axis=0))

    scale = A_HEAD_DIM ** -0.5
    q_chunks = [
        _rope128(q_ref[:, c * LANES:(c + 1) * LANES].astype(F32), cos, sin) * scale
        for c in range(A_HEADS // 2)]

    qi = lax.broadcasted_iota(jnp.int32, (WINDOW, WINDOW), 0)
    kj = lax.broadcasted_iota(jnp.int32, (WINDOW, WINDOW), 1)
    from_cur = kj <= qi
    prev_bias = jnp.where(n > 0, 0.0, -jnp.inf)

    def scores(qm, c):
        s = lax.dot_general(qm, k_both[c], NT_DIMS, preferred_element_type=F32)
        return jnp.where(from_cur, s[:, WINDOW:], s[:, :WINDOW] + prev_bias)

    def weighted_values(p, c):
        p_both = jnp.concatenate([jnp.where(from_cur, 0.0, p), jnp.where(from_cur, p, 0.0)],
                                 axis=1).astype(BF16)
        return (jnp.dot(p_both, v_both[c], preferred_element_type=F32),
                jnp.sum(p, axis=-1, keepdims=True))

    outs = _swa_heads(q_chunks, scores, weighted_values, sink_ref, layer)
    for c, o in enumerate(outs):
        o_ref[:, c * LANES:(c + 1) * LANES] = o.astype(o_ref.dtype)
    for c in range(2):
        kprev_ref[:, c * LANES:(c + 1) * LANES] = k_cur[c]


def _swa_prompt(z, sinks, layer, cos, sin, *, batch, seq, out_rows, name):
    nb = seq // WINDOW
    a_width = A_HEADS * A_HEAD_DIM
    kv_width = A_KV_HEADS * A_HEAD_DIM
    k_blk = a_width // kv_width
    v_blk = k_blk + 1
    rows = out_rows
    return pl.pallas_call(
        functools.partial(_swa_prompt_body, layer=layer),
        grid=(batch, nb),
        in_specs=[
            pl.BlockSpec(memory_space=pltpu.SMEM),
            pl.BlockSpec((WINDOW, a_width), lambda b, n: (b * nb + n, 0)),
            pl.BlockSpec((WINDOW, kv_width), lambda b, n: (b * nb + n, k_blk)),
            pl.BlockSpec((WINDOW, kv_width), lambda b, n: (b * nb + n, v_blk)),
            pl.BlockSpec((WINDOW, kv_width),
                         lambda b, n: (b * nb + jnp.maximum(n - 1, 0), v_blk)),
            pl.BlockSpec((WINDOW, LANES), lambda b, n: (n, 0)),
            pl.BlockSpec((WINDOW, LANES), lambda b, n: (n, 0)),
        ],
        out_specs=[
            pl.BlockSpec((WINDOW, a_width), lambda b, n: (b * nb + n, 0)),
            pl.BlockSpec((WINDOW, kv_width), lambda b, n: (b, 0)),
        ],
        out_shape=[jax.ShapeDtypeStruct((rows, a_width), BF16),
                   jax.ShapeDtypeStruct((batch * WINDOW, kv_width), F32)],
        scratch_shapes=[pltpu.VMEM((WINDOW, kv_width), BF16)],
        compiler_params=_params(("arbitrary", "arbitrary"), 0),
        name=name,
    )(sinks, z, z, z, z, cos, sin)


SWA_SAMPLE_SEQS = 8


def _swa_sample_body(sink_ref, q_ref, k_ref, v_ref, ck_ref, cv_ref, cos_ref,
                     sin_ref, oa_hbm_ref, o_ref, kr_ref, *, dec_seq, layer):
    del oa_hbm_ref
    cos = cos_ref[...]
    sin = sin_ref[...]
    g = SWA_SAMPLE_SEQS
    rows = g * dec_seq
    ncache = g * WINDOW
    k_chunks, v_chunks = [], []
    for c in range(2):
        sl = slice(c * LANES, (c + 1) * LANES)
        kc = _rope128(k_ref[:, sl].astype(F32), cos, sin)
        kr_ref[:, sl] = kc
        k_chunks.append(jnp.concatenate(
            [ck_ref[:, sl].astype(BF16), kc.astype(BF16)], axis=0))
        v_chunks.append(jnp.concatenate(
            [cv_ref[:, sl].astype(BF16), v_ref[:, sl]], axis=0))

    scale = A_HEAD_DIM ** -0.5
    q_chunks = [
        _rope128(q_ref[:, c * LANES:(c + 1) * LANES].astype(F32), cos, sin) * scale
        for c in range(A_HEADS // 2)]

    shape = (rows, ncache + rows)
    r = lax.broadcasted_iota(jnp.int32, shape, 0)
    col = lax.broadcasted_iota(jnp.int32, shape, 1)
    log_l = dec_seq.bit_length() - 1
    log_w = WINDOW.bit_length() - 1
    seq_q = r >> log_l
    i = r & (dec_seq - 1)
    in_cache = col < ncache
    cnew = col - ncache
    seq_k = jnp.where(in_cache, col >> log_w, cnew >> log_l)
    ok = ((in_cache & ((col & (WINDOW - 1)) > i))
          | (jnp.logical_not(in_cache) & ((cnew & (dec_seq - 1)) <= i)))
    valid = (seq_k == seq_q) & ok

    def scores(qm, c):
        s = lax.dot_general(qm, k_chunks[c], NT_DIMS, preferred_element_type=F32)
        return jnp.where(valid, s, -jnp.inf)

    def weighted_values(p, c):
        return (jnp.dot(p.astype(BF16), v_chunks[c], preferred_element_type=F32),
                jnp.sum(p, axis=-1, keepdims=True))

    outs = _swa_heads(q_chunks, scores, weighted_values, sink_ref, layer)
    for c, o in enumerate(outs):
        o_ref[:, c * LANES:(c + 1) * LANES] = o.astype(o_ref.dtype)


def _swa_sample(z, cache_k, cache_v, sinks, layer, oa, cos, sin, *, row0, dec_batch, dec_seq,
                name):
    g = SWA_SAMPLE_SEQS
    rows = g * dec_seq
    a_width = A_HEADS * A_HEAD_DIM
    kv_width = A_KV_HEADS * A_HEAD_DIM
    k_blk = a_width // kv_width
    v_blk = k_blk + 1
    r0 = row0 // rows
    depth = cache_k.shape[0]
    ck = cache_k.reshape(depth, dec_batch * WINDOW, kv_width)
    cv = cache_v.reshape(depth, dec_batch * WINDOW, kv_width)
    return pl.pallas_call(
        functools.partial(_swa_sample_body, dec_seq=dec_seq, layer=layer),
        grid=(dec_batch // g,),
        in_specs=[
            pl.BlockSpec(memory_space=pltpu.SMEM),
            pl.BlockSpec((rows, a_width), lambda s: (r0 + s, 0)),
            pl.BlockSpec((rows, kv_width), lambda s: (r0 + s, k_blk)),
            pl.BlockSpec((rows, kv_width), lambda s: (r0 + s, v_blk)),
            pl.BlockSpec((None, g * WINDOW, kv_width), lambda s: (layer, s, 0)),
            pl.BlockSpec((None, g * WINDOW, kv_width), lambda s: (layer, s, 0)),
            pl.BlockSpec((rows, LANES), lambda s: (0, 0)),
            pl.BlockSpec((rows, LANES), lambda s: (0, 0)),
            pl.BlockSpec(memory_space=pl.ANY),
        ],
        out_specs=[
            pl.BlockSpec((rows, a_width), lambda s: (r0 + s, 0)),
            pl.BlockSpec((rows, kv_width), lambda s: (s, 0)),
        ],
        out_shape=[jax.ShapeDtypeStruct(oa.shape, oa.dtype),
                   jax.ShapeDtypeStruct((dec_batch * dec_seq, kv_width), F32)],
        input_output_aliases={8: 0},
        compiler_params=_params(("arbitrary",), 0),
        name=name,
    )(sinks, z, z, z, ck, cv, cos, sin, oa)


RET_HEADS_PER_STEP = 4
RET_BLOCK = RET_HEADS_PER_STEP * R_DIM
RET_CHUNKS_PER_STEP = 4


def _decay_tables(length, nseq):
    log_g = np.log1p(-np.exp2(-5.0 - np.arange(R_HEADS, dtype=np.float64)))
    i = np.arange(length, dtype=np.float64)
    diff = i[:, None] - i[None, :]
    d = np.where(diff >= 0, np.exp(log_g[:, None, None] * np.maximum(diff, 0.0)), 0.0)
    d = np.einsum('st,hij->hsitj', np.eye(nseq), d).reshape(
        R_HEADS, nseq * length, nseq * length)
    q_dec = np.exp(log_g[:, None] * (i[None, :] + 1.0))
    k_dec = np.exp(log_g[:, None] * (length - 1.0 - i)[None, :])
    shape = (R_HEADS, nseq * length, R_DIM)
    qd = np.broadcast_to(np.tile(q_dec, (1, nseq))[:, :, None], shape)
    kd = np.broadcast_to(np.tile(k_dec, (1, nseq))[:, :, None], shape)
    g_len = np.exp(log_g * length)
    return tuple(jnp.asarray(t, F32) for t in (d, qd, kd, g_len))


def _group_norm_gate(o, gate):
    o = o * lax.rsqrt(jnp.mean(o * o, axis=-1, keepdims=True) + EPS)
    return o * (gate * jax.nn.sigmoid(gate))


def _ret_prompt_body(gl_ref, *refs):
    nb = R_HEADS // RET_HEADS_PER_STEP
    q_refs, k_refs, v_refs, g_refs = (refs[t * nb:(t + 1) * nb] for t in range(4))
    cos_ref, sin_ref, d_ref, qd_ref, kd_ref, o_ref, s_ref = refs[4 * nb:]
    c = pl.program_id(1)

    @pl.when(c == 0)
    def _():
        s_ref[...] = jnp.zeros_like(s_ref)

    scale = R_DIM ** -0.5
    for ck in range(RET_CHUNKS_PER_STEP):
        rs = slice(ck * R_CHUNK, (ck + 1) * R_CHUNK)
        cos = cos_ref[rs, :]
        sin = sin_ref[rs, :]
        for h in range(R_HEADS):
            blk = h // RET_HEADS_PER_STEP
            sl = slice((h % RET_HEADS_PER_STEP) * R_DIM, (h % RET_HEADS_PER_STEP + 1) * R_DIM)
            q = _retrot128(q_refs[blk][rs, sl].astype(F32), cos, sin)
            k = _retrot128(k_refs[blk][rs, sl].astype(F32), cos, sin) * scale
            v = v_refs[blk][rs, sl]
            qb = q.astype(BF16)
            state = s_ref[0, h]
            scores = lax.dot_general(qb, k.astype(BF16), NT_DIMS,
                                     preferred_element_type=F32) * d_ref[h]
            o = jnp.dot(scores.astype(BF16), v, preferred_element_type=F32)
            cross = jnp.dot(qb, state.astype(BF16), preferred_element_type=F32)
            o = o + cross * qd_ref[h]
            kd = (k * kd_ref[h]).astype(BF16)
            s_ref[0, h] = gl_ref[h] * state + lax.dot_general(
                kd, v, TN_DIMS, preferred_element_type=F32)
            o_ref[rs, h * R_DIM:(h + 1) * R_DIM] = _group_norm_gate(
                o, g_refs[blk][rs, sl].astype(F32)).astype(o_ref.dtype)


def _ret_prompt(z, cos, sin, tabs, *, col0, batch, seq, out_rows, name):
    d, qd, kd, g_len = tabs
    rows = RET_CHUNKS_PER_STEP * R_CHUNK
    assert seq % rows == 0
    nc = seq // rows
    nb = R_HEADS // RET_HEADS_PER_STEP
    width = R_HEADS * R_DIM
    blks = [(col0 + t * width) // RET_BLOCK + i for t in range(4) for i in range(nb)]

    def zspec(blk):
        return pl.BlockSpec((rows, RET_BLOCK), lambda b, c: (b * nc + c, blk))

    def tspec():
        return pl.BlockSpec((R_HEADS, R_CHUNK, R_DIM), lambda b, c: (0, 0, 0))

    return pl.pallas_call(
        _ret_prompt_body,
        grid=(batch, nc),
        in_specs=[pl.BlockSpec(memory_space=pltpu.SMEM)]
        + [zspec(blk) for blk in blks]
        + [pl.BlockSpec((rows, R_DIM), lambda b, c: (c, 0)),
           pl.BlockSpec((rows, R_DIM), lambda b, c: (c, 0)),
           tspec(), tspec(), tspec()],
        out_specs=[
            pl.BlockSpec((rows, width), lambda b, c: (b * nc + c, 0)),
            pl.BlockSpec((1, R_HEADS, R_DIM, R_DIM), lambda b, c: (b, 0, 0, 0)),
        ],
        out_shape=[jax.ShapeDtypeStruct((out_rows, width), BF16),
                   jax.ShapeDtypeStruct((batch, R_HEADS, R_DIM, R_DIM), F32)],
        compiler_params=_params(("arbitrary", "arbitrary"), 0),
        name=name,
    )(g_len, *([z] * len(blks)), cos, sin, d, qd, kd)


def _ret_sample_body(gl_ref, q_ref, k_ref, v_ref, g_ref, cos_ref, sin_ref,
                     d_ref, qd_ref, kd_ref, s_ref, *rest, dec_seq):
    o_ref, so_ref = rest[-2:]
    hh = pl.program_id(1)
    cos = cos_ref[...]
    sin = sin_ref[...]
    nseq = R_CHUNK // dec_seq
    log_l = dec_seq.bit_length() - 1
    row_seq = lax.broadcasted_iota(jnp.int32, (R_CHUNK, R_DIM), 0) >> log_l
    scale = R_DIM ** -0.5
    for hl in range(RET_HEADS_PER_STEP):
        sl = slice(hl * R_DIM, (hl + 1) * R_DIM)
        q = _retrot128(q_ref[:, sl].astype(F32), cos, sin)
        k = _retrot128(k_ref[:, sl].astype(F32), cos, sin) * scale
        v = v_ref[:, sl]
        qb = q.astype(BF16)
        scores = lax.dot_general(qb, k.astype(BF16), NT_DIMS,
                                 preferred_element_type=F32) * d_ref[hl]
        o = jnp.dot(scores.astype(BF16), v, preferred_element_type=F32)
        kd = k * kd_ref[hl]
        gl = gl_ref[hh * RET_HEADS_PER_STEP + hl]
        cross = jnp.zeros((R_CHUNK, R_DIM), F32)
        for s in range(nseq):
            mine = row_seq == s
            state = s_ref[s, hl]
            cs = jnp.dot(qb, state.astype(BF16), preferred_element_type=F32)
            cross = jnp.where(mine, cs, cross)
            ks = jnp.where(mine, kd, 0.0).astype(BF16)
            so_ref[s, hl] = gl * state + lax.dot_general(
                ks, v, TN_DIMS, preferred_element_type=F32)
        o = o + cross * qd_ref[hl]
        o_ref[:, sl] = _group_norm_gate(o, g_ref[:, sl].astype(F32)).astype(o_ref.dtype)


def _ret_sample(z, state, layer, ob, s_all, cos, sin, tabs, *, row0, col0, dec_batch, dec_seq,
                name):
    d, qd, kd, g_len = tabs
    nseq = R_CHUNK // dec_seq
    steps = dec_batch // nseq
    nh = R_HEADS // RET_HEADS_PER_STEP
    width = R_HEADS * R_DIM
    qb, kb, vb, gb = [(col0 + t * width) // RET_BLOCK for t in range(4)]
    r0 = row0 // R_CHUNK
    hp = RET_HEADS_PER_STEP

    def zspec(blk):
        return pl.BlockSpec((R_CHUNK, RET_BLOCK), lambda s, h: (r0 + s, blk + h))

    def tspec():
        return pl.BlockSpec((hp, R_CHUNK, R_DIM), lambda s, h: (h, 0, 0))

    sspec = pl.BlockSpec((None, nseq, hp, R_DIM, R_DIM), lambda s, h: (layer, s, h, 0, 0))
    args = [g_len, z, z, z, z, cos, sin, d, qd, kd, state, ob]
    aliases = {11: 0}
    if s_all is not None:
        aliases[len(args)] = 1
        args.append(s_all)
    return pl.pallas_call(
        functools.partial(_ret_sample_body, dec_seq=dec_seq),
        grid=(steps, nh),
        in_specs=[
            pl.BlockSpec(memory_space=pltpu.SMEM),
            zspec(qb), zspec(kb), zspec(vb), zspec(gb),
            pl.BlockSpec((R_CHUNK, R_DIM), lambda s, h: (0, 0)),
            pl.BlockSpec((R_CHUNK, R_DIM), lambda s, h: (0, 0)),
            tspec(), tspec(), tspec(),
            sspec,
        ] + [pl.BlockSpec(memory_space=pl.ANY)] * (len(args) - 11),
        out_specs=[
            pl.BlockSpec((R_CHUNK, RET_BLOCK), lambda s, h: (r0 + s, h)),
            sspec,
        ],
        out_shape=[jax.ShapeDtypeStruct(ob.shape, ob.dtype),
                   jax.ShapeDtypeStruct(state.shape, F32)],
        input_output_aliases=aliases,
        compiler_params=_params(("arbitrary", "arbitrary"),
                                4 * nseq * hp * R_DIM * R_DIM * 4 + (16 << 20)),
        name=name,
    )(*args)


def _gelu_tanh_times(x, gate):
    c = (2.0 / jnp.pi) ** 0.5
    t = jnp.tanh(x * (c + (c * 0.044715) * (x * x)))
    y = (0.5 * x) * gate
    return y + y * t


def _ffn_up_body(h_ref, wu_ref, wg_ref, cw_ref, cb_ref, e0_ref, e1_ref,
                 f_ref, tail_ref, us_ref, wubf_ref, wgbf_ref, carry_ref, ug_ref,
                 *, tm, n_tiles, seq, batch, ms, dec_seq):
    i = pl.program_id(1)
    tn = f_ref.shape[1]
    mp = batch * seq

    @pl.when(i == 0)
    def _():
        wubf_ref[...] = wu_ref[...].astype(BF16)
        wgbf_ref[...] = wg_ref[...].astype(BF16)
        carry_ref[...] = jnp.zeros_like(carry_ref)

    last = n_tiles - 1
    s0 = mp - last * tm
    assert 0 <= s0 and s0 + ms == tm and s0 % FIX_ROWS == 0
    starts = [k * seq for k in range(1, batch)]
    for r in starts:
        assert r % tm != 0 and r % FIX_ROWS == 0, "mid-tile, packed-tile aligned starts only"

    def conv_gate(u, gate, p1, p2, w, bias):
        conv = bias + w[0:1, :] * p2
        conv = conv + w[1:2, :] * p1
        conv = conv + w[2:3, :] * u
        return _gelu_tanh_times(conv, gate).astype(f_ref.dtype)

    cuts = [0] + [tm * a // FFN_ROW_SPLIT[-1] for a in FFN_ROW_SPLIT]
    assert cuts[-1] == tm and all(r % FIX_ROWS == 0 for r in cuts)
    for c in range(tn // MXU_COLS):
        cs = slice(c * MXU_COLS, (c + 1) * MXU_COLS)
        w = cw_ref[:, cs]
        bias = cb_ref[:, cs]
        for part in range(len(cuts) - 1):
            r0 = cuts[part]
            rs = slice(r0, cuts[part + 1])
            h = h_ref[rs, :]
            ug_ref[2 * c, rs, :] = jnp.dot(h, wubf_ref[:, cs], preferred_element_type=F32)
            ug_ref[2 * c + 1, rs, :] = jnp.dot(h, wgbf_ref[:, cs], preferred_element_type=F32)
            u = ug_ref[2 * c, rs, :]
            gate = ug_ref[2 * c + 1, rs, :]

            if part == 0:
                before = carry_ref[:, cs]
            else:
                before = ug_ref[2 * c, r0 - SUBLANES:r0, :]
            last1 = before[SUBLANES - 1:SUBLANES, :]
            last2 = before[SUBLANES - 2:SUBLANES - 1, :]
            f_ref[rs, cs] = conv_gate(u, gate, pltpu.roll(u, 1, 0), pltpu.roll(u, 2, 0),
                                      w, bias)
            uh = u[:FIX_ROWS, :]
            pos = lax.broadcasted_iota(jnp.int32, uh.shape, 0)
            prev1 = jnp.where(pos == 0, last1, pltpu.roll(uh, 1, 0))
            prev2 = jnp.where(pos == 0, last2,
                              jnp.where(pos == 1, last1, pltpu.roll(uh, 2, 0)))
            f_ref[r0:r0 + FIX_ROWS, cs] = conv_gate(
                uh, gate[:FIX_ROWS, :], prev1, prev2, w, bias)
        carry_ref[:, cs] = ug_ref[2 * c, tm - SUBLANES:, :]

    def rewrite(rows, p1_fn, p2_fn):
        for c in range(tn // MXU_COLS):
            cs = slice(c * MXU_COLS, (c + 1) * MXU_COLS)
            us = ug_ref[2 * c, rows, :]
            f_ref[rows, cs] = conv_gate(us, ug_ref[2 * c + 1, rows, :], p1_fn(us, cs),
                                        p2_fn(us, cs), cw_ref[:, cs], cb_ref[:, cs])

    for r in starts:
        @pl.when(i == r // tm)
        def _(off=r % tm):
            pos = lax.broadcasted_iota(jnp.int32, (FIX_ROWS, MXU_COLS), 0)
            rewrite(slice(off, off + FIX_ROWS),
                    lambda us, cs: jnp.where(pos == 0, 0.0, pltpu.roll(us, 1, 0)),
                    lambda us, cs: jnp.where(pos <= 1, 0.0, pltpu.roll(us, 2, 0)))

    @pl.when(i == last)
    def _():
        pos = lax.broadcasted_iota(jnp.int32, (ms, MXU_COLS), 0) & (dec_seq - 1)
        rewrite(slice(s0, tm),
                lambda us, cs: jnp.where(pos == 0, e1_ref[:, cs], pltpu.roll(us, 1, 0)),
                lambda us, cs: jnp.where(
                    pos == 0, e0_ref[:, cs],
                    jnp.where(pos == 1, e1_ref[:, cs], pltpu.roll(us, 2, 0))))
        for c in range(tn // MXU_COLS):
            us_ref[:, c * MXU_COLS:(c + 1) * MXU_COLS] = ug_ref[2 * c, s0:, :]

    for b in range(batch):
        r = (b + 1) * seq - SUBLANES

        @pl.when(i == r // tm)
        def _(b=b, off=r % tm):
            for c in range(tn // MXU_COLS):
                tail_ref[b * SUBLANES:(b + 1) * SUBLANES, c * MXU_COLS:(c + 1) * MXU_COLS] = (
                    ug_ref[2 * c, off:off + SUBLANES, :])


def _ffn_up(h, w_up, conv_w, conv_b, layer, e, *, tm, tn, seq, batch, dec_seq, name):
    m, k = h.shape
    d_ff = w_up.shape[2] // 2
    ms = e.shape[2]
    nj = d_ff // tn
    nt = m // tm
    body = functools.partial(_ffn_up_body, tm=tm, n_tiles=nt, seq=seq, batch=batch,
                             ms=ms, dec_seq=dec_seq)
    nbytes = (2 * tm * k * 2 + 4 * k * tn * 4 + 2 * k * tn * 2
              + 16 * tm * tn * 4 + 4 * ms * tn * 4 + (4 << 20))
    return pl.pallas_call(
        body,
        grid=(nj, nt),
        in_specs=[
            pl.BlockSpec((tm, k), lambda j, i: (i, 0)),
            pl.BlockSpec((None, k, tn), lambda j, i: (layer, 0, j)),
            pl.BlockSpec((None, k, tn), lambda j, i: (layer, 0, nj + j)),
            pl.BlockSpec((None, CONV_W, tn), lambda j, i: (layer, 0, j)),
            pl.BlockSpec((None, 1, tn), lambda j, i: (layer, 0, j)),
            pl.BlockSpec((None, None, ms, tn), lambda j, i: (layer, 0, 0, j)),
            pl.BlockSpec((None, None, ms, tn), lambda j, i: (layer, 1, 0, j)),
        ],
        out_specs=[
            pl.BlockSpec((tm, tn), lambda j, i: (i, j)),
            pl.BlockSpec((batch * SUBLANES, tn), lambda j, i: (0, j)),
            pl.BlockSpec((ms, tn), lambda j, i: (0, j)),
        ],
        out_shape=[jax.ShapeDtypeStruct((m, d_ff), BF16),
                   jax.ShapeDtypeStruct((batch * SUBLANES, d_ff), F32),
                   jax.ShapeDtypeStruct((ms, d_ff), F32)],
        scratch_shapes=[pltpu.VMEM((k, tn), BF16), pltpu.VMEM((k, tn), BF16),
                        pltpu.VMEM((SUBLANES, tn), F32),
                        pltpu.VMEM((2 * tn // MXU_COLS, tm, MXU_COLS), F32)],
        compiler_params=_params(("arbitrary", "arbitrary"), nbytes),
        name=name,
    )(h, w_up, w_up, conv_w, conv_b, e, e)


TM = 1408
TM_NORM = 768
TM_DOWN = 704
TM_IN = 2816
TN_IN = 512
TN_UP = 512
TM_MERGE = 352
GATE_BLOCK = 512
TM_OUT = 352
FFN_ROW_SPLIT = (2, 4)
TN_OUT = 512


def kernel(x_prompt, x_sample, cache_win_k, cache_win_v, state_ret, state_conv,
           g_mix, w_in, sinks, w_proj_a, w_proj_b, w_o, g_ffn, w_up, conv_w,
           conv_b, w_down, g_final):
    batch, seq, d_model = x_prompt.shape
    dec_batch, dec_seq, _ = x_sample.shape
    depth = w_in.shape[0]
    d_ff = w_down.shape[1]
    mp = batch * seq
    ms = dec_batch * dec_seq
    a_width = A_HEADS * A_HEAD_DIM
    kv_width = A_KV_HEADS * A_HEAD_DIM
    r_width = R_HEADS * R_DIM
    ret_col0 = a_width + 2 * kv_width
    ga_col = ret_col0 + 4 * r_width
    gb_col = ga_col + d_model
    assert dec_seq & (dec_seq - 1) == 0 and R_CHUNK % dec_seq == 0

    pos_p = jnp.arange(seq, dtype=F32)
    pos_s = PAST_LEN + jnp.arange(dec_seq, dtype=F32)
    rope_p = _rope_tables(pos_p)
    rope_s = tuple(jnp.tile(t, (SWA_SAMPLE_SEQS, 1)) for t in _rope_tables(pos_s))
    rrot_p = _retrot_tables(pos_p)
    nseq = R_CHUNK // dec_seq
    rrot_s = tuple(jnp.tile(t, (nseq, 1)) for t in _retrot_tables(pos_s))
    tabs_p = _decay_tables(R_CHUNK, 1)
    tabs_s = _decay_tables(dec_seq, nseq)
    g_mix3 = g_mix.reshape(depth, 1, d_model)
    g_ffn3 = g_ffn.reshape(depth, 1, d_model)
    conv_b3 = conv_b.reshape(depth, 1, d_ff)

    e_conv = jnp.repeat(jnp.swapaxes(state_conv, 1, 2), dec_seq, axis=2)
    v_col0 = a_width + kv_width

    kp_l, vp_l, sp_l, cp_l = [], [], [], []
    ks_l, vs_l, cs_l = [], [], []
    s_all = None
    for l in range(depth):
        if l == 0:
            x, h = _join_norm(x_prompt.reshape(mp, d_model), x_sample.reshape(ms, d_model),
                              g_mix3, l, name="join_norm_mix0")
        else:
            h = _rmsnorm(x, g_mix3, l, BF16, tm=TM_NORM, name=f"norm_mix{l}")
        z = _matmul(h, w_in, l, tm=TM_IN, tn=TN_IN, out_dtype=BF16, name=f"proj_in{l}")

        oa, kr_p = _swa_prompt(z, sinks, l, *rope_p, batch=batch, seq=seq,
                               out_rows=mp + ms, name=f"swa_prompt{l}")
        oa, kr_s = _swa_sample(z, cache_win_k, cache_win_v, sinks, l, oa, *rope_s,
                               row0=mp, dec_batch=dec_batch, dec_seq=dec_seq,
                               name=f"swa_sample{l}")
        ob, s_p = _ret_prompt(z, *rrot_p, tabs_p, col0=ret_col0, batch=batch,
                              seq=seq, out_rows=mp + ms, name=f"ret_prompt{l}")
        ob, s_all = _ret_sample(z, state_ret, l, ob, s_all, *rrot_s, tabs_s, row0=mp,
                                col0=ret_col0, dec_batch=dec_batch, dec_seq=dec_seq,
                                name=f"ret_sample{l}")
        merged = _merge(oa, ob, z, w_proj_a, w_proj_b, l, ga_col=ga_col,
                        gb_col=gb_col, tm=TM_MERGE, tg=GATE_BLOCK, name=f"merge{l}")
        x, h2 = _matmul_res_norm(merged, w_o, l, x, g_ffn3, tm=TM_OUT,
                                 name=f"proj_out_norm{l}")
        f, u_tail, u_s = _ffn_up(h2, w_up, conv_w, conv_b3, l, e_conv, tm=TM, tn=TN_UP,
                                 seq=seq, batch=batch, dec_seq=dec_seq, name=f"ffn_up{l}")
        x = _matmul(f, w_down, l, tm=TM_DOWN, tn=TN_OUT, out_dtype=F32, residual=x,
                    name=f"proj_down{l}")

        v_p = jnp.stack([lax.slice(z, ((b + 1) * seq - WINDOW, v_col0), ((b + 1) * seq, ret_col0))
                         for b in range(batch)])
        v_s = lax.slice(z, (mp, v_col0), (mp + ms, ret_col0))
        kp_l.append(kr_p.reshape(batch, WINDOW, A_KV_HEADS, A_HEAD_DIM))
        vp_l.append(v_p.astype(F32).reshape(batch, WINDOW, A_KV_HEADS, A_HEAD_DIM))
        sp_l.append(s_p)
        cp_l.append(u_tail.reshape(batch, SUBLANES, d_ff)[:, -(CONV_W - 1):])
        ks_l.append(kr_s.reshape(dec_batch, dec_seq, A_KV_HEADS, A_HEAD_DIM))
        vs_l.append(v_s.astype(F32).reshape(dec_batch, dec_seq, A_KV_HEADS, A_HEAD_DIM))
        cs_l.append(u_s.reshape(dec_batch, dec_seq, d_ff)[:, -(CONV_W - 1):])

    y_prompt, y_sample = _split_norm(x, g_final, mp=mp, name="norm_final")
    y_prompt = y_prompt.reshape(batch, seq, d_model)
    y_sample = y_sample.reshape(dec_batch, dec_seq, d_model)
    return (y_prompt, y_sample,
            jnp.stack(kp_l), jnp.stack(vp_l), jnp.stack(sp_l), jnp.stack(cp_l),
            jnp.stack(ks_l), jnp.stack(vs_l), s_all, jnp.stack(cs_l))
```

```python
import functools

import jax
import jax.numpy as jnp
import numpy as np
from jax import lax
from jax.experimental import pallas as pl
from jax.experimental.pallas import tpu as pltpu

F32 = jnp.float32
BF16 = jnp.bfloat16

LANES = 128
SUBLANES = 8
MXU_COLS = 256
FIX_ROWS = 2 * SUBLANES
VMEM_LIMIT_CAP = 56 * 1024 * 1024

WINDOW = 128
A_HEADS = 16
A_KV_HEADS = 4
A_HEAD_DIM = 64
A_GROUP = A_HEADS // A_KV_HEADS
R_HEADS = 8
R_DIM = 128
R_CHUNK = 128
CONV_W = 3
EPS = 1e-6
ROPE_THETA = 10000.0
PAST_LEN = 16384

NT_DIMS = (((1,), (1,)), ((), ()))
TN_DIMS = (((0,), (0,)), ((), ()))


def _vmem_limit(nbytes):
    return int(min(VMEM_LIMIT_CAP, max(32 * 1024 * 1024, nbytes)))


def _params(semantics, nbytes):
    return pltpu.CompilerParams(dimension_semantics=semantics,
                                vmem_limit_bytes=_vmem_limit(nbytes))


def _rmsnorm_body(x_ref, g_ref, o_ref):
    x = x_ref[...]
    ms = jnp.mean(x * x, axis=-1, keepdims=True)
    o_ref[...] = (x * lax.rsqrt(ms + EPS) * g_ref[...]).astype(o_ref.dtype)


def _rmsnorm(x, g, layer, out_dtype, *, tm, name):
    m, d = x.shape
    return pl.pallas_call(
        _rmsnorm_body,
        grid=(m // tm,),
        in_specs=[pl.BlockSpec((tm, d), lambda i: (i, 0)),
                  pl.BlockSpec((None, 1, d), lambda i: (layer, 0, 0))],
        out_specs=pl.BlockSpec((tm, d), lambda i: (i, 0)),
        out_shape=jax.ShapeDtypeStruct((m, d), out_dtype),
        compiler_params=_params(("arbitrary",), 6 * tm * d * 4),
        name=name,
    )(x, g)


def _join_norm_body(xp_ref, xs_ref, g_ref, x_ref, h_ref, *, prompt_tiles):
    i = pl.program_id(0)

    def emit(x):
        ms = jnp.mean(x * x, axis=-1, keepdims=True)
        x_ref[...] = x
        h_ref[...] = (x * lax.rsqrt(ms + EPS) * g_ref[...]).astype(h_ref.dtype)

    @pl.when(i < prompt_tiles)
    def _():
        emit(xp_ref[...])

    @pl.when(i >= prompt_tiles)
    def _():
        emit(xs_ref[...])


def _join_norm(xp, xs, g, layer, *, name):
    mp, d = xp.shape
    ms = xs.shape[0]
    assert mp % ms == 0
    pt = mp // ms
    return pl.pallas_call(
        functools.partial(_join_norm_body, prompt_tiles=pt),
        grid=(pt + 1,),
        in_specs=[pl.BlockSpec((ms, d), lambda i: (jnp.minimum(i, pt - 1), 0)),
                  pl.BlockSpec((ms, d), lambda i: (0, 0)),
                  pl.BlockSpec((None, 1, d), lambda i: (layer, 0, 0))],
        out_specs=[pl.BlockSpec((ms, d), lambda i: (i, 0)),
                   pl.BlockSpec((ms, d), lambda i: (i, 0))],
        out_shape=[jax.ShapeDtypeStruct((mp + ms, d), F32),
                   jax.ShapeDtypeStruct((mp + ms, d), BF16)],
        compiler_params=_params(("arbitrary",), 0),
        name=name,
    )(xp, xs, g)


def _split_norm_body(x_ref, g_ref, yp_ref, ys_ref, *, prompt_tiles):
    i = pl.program_id(0)
    x = x_ref[...]
    ms = jnp.mean(x * x, axis=-1, keepdims=True)
    y = x * lax.rsqrt(ms + EPS) * g_ref[...]

    @pl.when(i < prompt_tiles)
    def _():
        yp_ref[...] = y

    @pl.when(i >= prompt_tiles)
    def _():
        ys_ref[...] = y


def _split_norm(x, g, *, mp, name):
    m, d = x.shape
    ms = m - mp
    assert mp % ms == 0
    pt = mp // ms
    return pl.pallas_call(
        functools.partial(_split_norm_body, prompt_tiles=pt),
        grid=(pt + 1,),
        in_specs=[pl.BlockSpec((ms, d), lambda i: (i, 0)),
                  pl.BlockSpec((1, d), lambda i: (0, 0))],
        out_specs=[pl.BlockSpec((ms, d), lambda i: (jnp.minimum(i, pt - 1), 0)),
                   pl.BlockSpec((ms, d), lambda i: (0, 0))],
        out_shape=[jax.ShapeDtypeStruct((mp, d), F32),
                   jax.ShapeDtypeStruct((ms, d), F32)],
        compiler_params=_params(("arbitrary",), 0),
        name=name,
    )(x, g.reshape(1, d))


def _mm_body(x_ref, w_ref, o_ref, wbf_ref):
    @pl.when(pl.program_id(1) == 0)
    def _():
        wbf_ref[...] = w_ref[...].astype(BF16)

    o_ref[...] = jnp.dot(x_ref[...], wbf_ref[...],
                         preferred_element_type=F32).astype(o_ref.dtype)


def _mm_res_body(x_ref, w_ref, r_ref, o_ref, wbf_ref):
    @pl.when(pl.program_id(1) == 0)
    def _():
        wbf_ref[...] = w_ref[...].astype(BF16)

    o_ref[...] = r_ref[...] + jnp.dot(x_ref[...], wbf_ref[...],
                                      preferred_element_type=F32)


def _matmul(x, w, layer, *, tm, tn, out_dtype, residual=None, name):
    m, k = x.shape
    n = w.shape[2]
    grid = (n // tn, m // tm)
    in_specs = [pl.BlockSpec((tm, k), lambda j, i: (i, 0)),
                pl.BlockSpec((None, k, tn), lambda j, i: (layer, 0, j))]
    args = [x, w]
    body = _mm_body
    if residual is not None:
        in_specs.append(pl.BlockSpec((tm, tn), lambda j, i: (i, j)))
        args.append(residual)
        body = _mm_res_body
    nbytes = (2 * tm * k * 2 + 2 * k * tn * 4 + k * tn * 2
              + 6 * tm * tn * 4 + (4 << 20))
    return pl.pallas_call(
        body,
        grid=grid,
        in_specs=in_specs,
        out_specs=pl.BlockSpec((tm, tn), lambda j, i: (i, j)),
        out_shape=jax.ShapeDtypeStruct((m, n), out_dtype),
        scratch_shapes=[pltpu.VMEM((k, tn), BF16)],
        compiler_params=_params(("arbitrary", "arbitrary"), nbytes),
        name=name,
    )(*args)


def _out_norm_body(m_ref, w_ref, r_ref, g_ref, x_ref, h_ref, wbf_ref):
    @pl.when(pl.program_id(0) == 0)
    def _():
        wbf_ref[...] = w_ref[...].astype(BF16)

    x = r_ref[...] + jnp.dot(m_ref[...], wbf_ref[...], preferred_element_type=F32)
    x_ref[...] = x
    ms = jnp.mean(x * x, axis=-1, keepdims=True)
    h_ref[...] = (x * lax.rsqrt(ms + EPS) * g_ref[...]).astype(h_ref.dtype)


def _matmul_res_norm(a, w, layer, residual, g, *, tm, name):
    m, k = a.shape
    n = w.shape[2]
    nbytes = (k * n * 4 + k * n * 2 + 2 * tm * k * 2 + 4 * tm * n * 4
              + 2 * tm * n * 2 + 2 * tm * n * 4 + (4 << 20))
    return pl.pallas_call(
        _out_norm_body,
        grid=(m // tm,),
        in_specs=[pl.BlockSpec((tm, k), lambda i: (i, 0)),
                  pl.BlockSpec((None, k, n), lambda i: (layer, 0, 0),
                               pipeline_mode=pl.Buffered(1)),
                  pl.BlockSpec((tm, n), lambda i: (i, 0)),
                  pl.BlockSpec((None, 1, n), lambda i: (layer, 0, 0))],
        out_specs=[pl.BlockSpec((tm, n), lambda i: (i, 0)),
                   pl.BlockSpec((tm, n), lambda i: (i, 0))],
        out_shape=[jax.ShapeDtypeStruct((m, n), F32),
                   jax.ShapeDtypeStruct((m, n), BF16)],
        scratch_shapes=[pltpu.VMEM((k, n), BF16)],
        compiler_params=_params(("arbitrary",), nbytes),
        name=name,
    )(a, w, residual, g)


def _merge_body(*refs, n_gate, tg, prompt_tiles):
    oap_ref, oas_ref, obp_ref, obs_ref = refs[:4]
    ga_refs = refs[4:4 + n_gate]
    gb_refs = refs[4 + n_gate:4 + 2 * n_gate]
    wa_ref, wb_ref, o_ref, wabf_ref, wbbf_ref = refs[4 + 2 * n_gate:]
    i = pl.program_id(0)

    @pl.when(i == 0)
    def _():
        wabf_ref[...] = wa_ref[...].astype(BF16)
        wbbf_ref[...] = wb_ref[...].astype(BF16)

    is_prompt = i < prompt_tiles
    oa = jnp.where(is_prompt, oap_ref[...], oas_ref[...])
    ob = jnp.where(is_prompt, obp_ref[...], obs_ref[...])
    for c in range(n_gate):
        cs = slice(c * tg, (c + 1) * tg)
        a = jnp.dot(oa, wabf_ref[:, cs], preferred_element_type=F32)
        b = jnp.dot(ob, wbbf_ref[:, cs], preferred_element_type=F32)
        ga = jax.nn.sigmoid(ga_refs[c][...].astype(F32))
        gb = jax.nn.sigmoid(gb_refs[c][...].astype(F32))
        o_ref[:, cs] = (ga * a + gb * b).astype(o_ref.dtype)


def _merge(oa_p, oa_s, ob_p, ob_s, z, wa, wb, layer, *, ga_col, gb_col, tg, name):
    mp, k = oa_p.shape
    tm = oa_s.shape[0]
    assert mp % tm == 0
    pt = mp // tm
    m = mp + tm
    n = wa.shape[2]
    assert ga_col % tg == 0 and gb_col % tg == 0 and n % tg == 0
    n_gate = n // tg
    nbytes = (2 * k * n * 4 + 2 * k * n * 2 + 4 * tm * k * 2 + 4 * tm * n * 2
              + 2 * tm * n * 2 + 6 * tm * tg * 4 + (4 << 20))

    def gate_spec(blk):
        return pl.BlockSpec((tm, tg), lambda i: (i, blk))

    def gate_specs(col):
        return [gate_spec(col // tg + c) for c in range(n_gate)]

    def wspec():
        return pl.BlockSpec((None, k, n), lambda i: (layer, 0, 0), pipeline_mode=pl.Buffered(1))

    def prompt_spec():
        return pl.BlockSpec((tm, k), lambda i: (jnp.minimum(i, pt - 1), 0))

    def sample_spec():
        return pl.BlockSpec((tm, k), lambda i: (0, 0))

    return pl.pallas_call(
        functools.partial(_merge_body, n_gate=n_gate, tg=tg, prompt_tiles=pt),
        grid=(m // tm,),
        in_specs=[prompt_spec(), sample_spec(), prompt_spec(), sample_spec()]
        + gate_specs(ga_col) + gate_specs(gb_col) + [wspec(), wspec()],
        out_specs=pl.BlockSpec((tm, n), lambda i: (i, 0)),
        out_shape=jax.ShapeDtypeStruct((m, n), BF16),
        scratch_shapes=[pltpu.VMEM((k, n), BF16), pltpu.VMEM((k, n), BF16)],
        compiler_params=_params(("arbitrary",), nbytes),
        name=name,
    )(oa_p, oa_s, ob_p, ob_s, *([z] * (2 * n_gate)), wa, wb)


def _rope_tables(pos):
    half = A_HEAD_DIM // 2
    inv = ROPE_THETA ** (-2.0 * jnp.arange(half, dtype=F32) / A_HEAD_DIM)
    ang = pos[:, None] * inv[None, :]
    c, s = jnp.cos(ang), jnp.sin(ang)
    cos = jnp.concatenate([c, c, c, c], axis=-1)
    sin = jnp.concatenate([-s, s, -s, s], axis=-1)
    return cos, sin


def _retrot_tables(pos):
    half = R_DIM // 2
    inv = 1.0 / (10000.0 ** jnp.linspace(0.0, 1.0, half, dtype=F32))
    ang = pos[:, None] * inv[None, :]
    c, s = jnp.cos(ang), jnp.sin(ang)
    cos = jnp.stack([c, c], axis=-1).reshape(pos.shape[0], R_DIM)
    sin = jnp.stack([-s, s], axis=-1).reshape(pos.shape[0], R_DIM)
    return cos, sin


def _rope128(x, cos, sin):
    lane = lax.broadcasted_iota(jnp.int32, x.shape, 1)
    first = (lane & (A_HEAD_DIM - 1)) < (A_HEAD_DIM // 2)
    partner = jnp.where(first, pltpu.roll(x, LANES - 32, 1), pltpu.roll(x, 32, 1))
    return x * cos + partner * sin


def _retrot128(x, cos, sin):
    lane = lax.broadcasted_iota(jnp.int32, x.shape, 1)
    even = (lane & 1) == 0
    partner = jnp.where(even, pltpu.roll(x, LANES - 1, 1), pltpu.roll(x, 1, 1))
    return x * cos + partner * sin


def _swa_heads(q_chunks, score_fn, out_fn, sink_ref, layer):
    rows = q_chunks[0].shape[0]
    lane = lax.broadcasted_iota(jnp.int32, (rows, LANES), 1)
    low = lane < A_HEAD_DIM
    outs = []
    for pair in range(A_HEADS // 2):
        halves = []
        for hpos in range(2):
            h = 2 * pair + hpos
            kv = h // A_GROUP
            c, kpos = kv // 2, kv % 2
            qc = q_chunks[pair]
            if hpos != kpos:
                qc = pltpu.roll(qc, A_HEAD_DIM, 1)
            keep = low if kpos == 0 else jnp.logical_not(low)
            qm = jnp.where(keep, qc, 0.0).astype(BF16)
            s = score_fn(qm, c)
            sink = sink_ref[layer, h]
            m = jnp.maximum(jnp.max(s, axis=-1, keepdims=True), sink)
            p = jnp.exp(s - m)
            pv, psum = out_fn(p, c)
            o = pv / (psum + jnp.exp(sink - m))
            if hpos != kpos:
                o = pltpu.roll(o, A_HEAD_DIM, 1)
            halves.append(o)
        outs.append(jnp.where(low, halves[0], halves[1]))
    return outs


def _swa_prompt_body(sink_ref, q_ref, k_ref, vc_ref, vp_ref, cos_ref, sin_ref,
                     o_ref, kr_ref, kprev_ref, *, layer):
    n = pl.program_id(1)
    cos = cos_ref[...]
    sin = sin_ref[...]

    @pl.when(n == 0)
    def _():
        kprev_ref[...] = jnp.zeros_like(kprev_ref)

    k_cur, k_both, v_both = [], [], []
    for c in range(2):
        sl = slice(c * LANES, (c + 1) * LANES)
        kc = _rope128(k_ref[:, sl].astype(F32), cos, sin)
        kr_ref[:, sl] = kc
        k_cur.append(kc.astype(BF16))
        k_both.append(jnp.concatenate([kprev_ref[:, sl], k_cur[c]], axis=0))
        v_both.append(jnp.concatenate([vp_ref[:, sl], vc_ref[:, sl]], axis=0))

    scale = A_HEAD_DIM ** -0.5
    q_chunks = [
        _rope128(q_ref[:, c * LANES:(c + 1) * LANES].astype(F32), cos, sin) * scale
        for c in range(A_HEADS // 2)]

    qi = lax.broadcasted_iota(jnp.int32, (WINDOW, WINDOW), 0)
    kj = lax.broadcasted_iota(jnp.int32, (WINDOW, WINDOW), 1)
    from_cur = kj <= qi
    prev_bias = jnp.where(n > 0, 0.0, -jnp.inf)

    def scores(qm, c):
        s = lax.dot_general(qm, k_both[c], NT_DIMS, preferred_element_type=F32)
        return jnp.where(from_cur, s[:, WINDOW:], s[:, :WINDOW] + prev_bias)

    def weighted_values(p, c):
        p_both = jnp.concatenate([jnp.where(from_cur, 0.0, p), jnp.where(from_cur, p, 0.0)],
                                 axis=1).astype(BF16)
        return (jnp.dot(p_both, v_both[c], preferred_element_type=F32),
                jnp.sum(p, axis=-1, keepdims=True))

    outs = _swa_heads(q_chunks, scores, weighted_values, sink_ref, layer)
    for c, o in enumerate(outs):
        o_ref[:, c * LANES:(c + 1) * LANES] = o.astype(o_ref.dtype)
    for c in range(2):
        kprev_ref[:, c * LANES:(c + 1) * LANES] = k_cur[c]


def _swa_prompt(z, sinks, layer, cos, sin, *, batch, seq, name):
    nb = seq // WINDOW
    a_width = A_HEADS * A_HEAD_DIM
    kv_width = A_KV_HEADS * A_HEAD_DIM
    k_blk = a_width // kv_width
    v_blk = k_blk + 1
    rows = batch * seq
    return pl.pallas_call(
        functools.partial(_swa_prompt_body, layer=layer),
        grid=(batch, nb),
        in_specs=[
            pl.BlockSpec(memory_space=pltpu.SMEM),
            pl.BlockSpec((WINDOW, a_width), lambda b, n: (b * nb + n, 0)),
            pl.BlockSpec((WINDOW, kv_width), lambda b, n: (b * nb + n, k_blk)),
            pl.BlockSpec((WINDOW, kv_width), lambda b, n: (b * nb + n, v_blk)),
            pl.BlockSpec((WINDOW, kv_width),
                         lambda b, n: (b * nb + jnp.maximum(n - 1, 0), v_blk)),
            pl.BlockSpec((WINDOW, LANES), lambda b, n: (n, 0)),
            pl.BlockSpec((WINDOW, LANES), lambda b, n: (n, 0)),
        ],
        out_specs=[
            pl.BlockSpec((WINDOW, a_width), lambda b, n: (b * nb + n, 0)),
            pl.BlockSpec((WINDOW, kv_width), lambda b, n: (b, 0)),
        ],
        out_shape=[jax.ShapeDtypeStruct((rows, a_width), BF16),
                   jax.ShapeDtypeStruct((batch * WINDOW, kv_width), F32)],
        scratch_shapes=[pltpu.VMEM((WINDOW, kv_width), BF16)],
        compiler_params=_params(("arbitrary", "arbitrary"), 0),
        name=name,
    )(sinks, z, z, z, z, cos, sin)


SWA_SAMPLE_SEQS = 8


def _swa_sample_body(sink_ref, q_ref, k_ref, v_ref, ck_ref, cv_ref, cos_ref,
                     sin_ref, o_ref, kr_ref, *, dec_seq, layer):
    cos = cos_ref[...]
    sin = sin_ref[...]
    g = SWA_SAMPLE_SEQS
    rows = g * dec_seq
    ncache = g * WINDOW
    k_chunks, v_chunks = [], []
    for c in range(2):
        sl = slice(c * LANES, (c + 1) * LANES)
        kc = _rope128(k_ref[:, sl].astype(F32), cos, sin)
        kr_ref[:, sl] = kc
        k_chunks.append(jnp.concatenate(
            [ck_ref[:, sl].astype(BF16), kc.astype(BF16)], axis=0))
        v_chunks.append(jnp.concatenate(
            [cv_ref[:, sl].astype(BF16), v_ref[:, sl]], axis=0))

    scale = A_HEAD_DIM ** -0.5
    q_chunks = [
        _rope128(q_ref[:, c * LANES:(c + 1) * LANES].astype(F32), cos, sin) * scale
        for c in range(A_HEADS // 2)]

    shape = (rows, ncache + rows)
    r = lax.broadcasted_iota(jnp.int32, shape, 0)
    col = lax.broadcasted_iota(jnp.int32, shape, 1)
    log_l = dec_seq.bit_length() - 1
    log_w = WINDOW.bit_length() - 1
    seq_q = r >> log_l
    i = r & (dec_seq - 1)
    in_cache = col < ncache
    cnew = col - ncache
    seq_k = jnp.where(in_cache, col >> log_w, cnew >> log_l)
    ok = ((in_cache & ((col & (WINDOW - 1)) > i))
          | (jnp.logical_not(in_cache) & ((cnew & (dec_seq - 1)) <= i)))
    valid = (seq_k == seq_q) & ok

    def scores(qm, c):
        s = lax.dot_general(qm, k_chunks[c], NT_DIMS, preferred_element_type=F32)
        return jnp.where(valid, s, -jnp.inf)

    def weighted_values(p, c):
        return (jnp.dot(p.astype(BF16), v_chunks[c], preferred_element_type=F32),
                jnp.sum(p, axis=-1, keepdims=True))

    outs = _swa_heads(q_chunks, scores, weighted_values, sink_ref, layer)
    for c, o in enumerate(outs):
        o_ref[:, c * LANES:(c + 1) * LANES] = o.astype(o_ref.dtype)


def _swa_sample(z, cache_k, cache_v, sinks, layer, cos, sin, *, row0, dec_batch, dec_seq, name):
    g = SWA_SAMPLE_SEQS
    rows = g * dec_seq
    a_width = A_HEADS * A_HEAD_DIM
    kv_width = A_KV_HEADS * A_HEAD_DIM
    k_blk = a_width // kv_width
    v_blk = k_blk + 1
    r0 = row0 // rows
    depth = cache_k.shape[0]
    ck = cache_k.reshape(depth, dec_batch * WINDOW, kv_width)
    cv = cache_v.reshape(depth, dec_batch * WINDOW, kv_width)
    return pl.pallas_call(
        functools.partial(_swa_sample_body, dec_seq=dec_seq, layer=layer),
        grid=(dec_batch // g,),
        in_specs=[
            pl.BlockSpec(memory_space=pltpu.SMEM),
            pl.BlockSpec((rows, a_width), lambda s: (r0 + s, 0)),
            pl.BlockSpec((rows, kv_width), lambda s: (r0 + s, k_blk)),
            pl.BlockSpec((rows, kv_width), lambda s: (r0 + s, v_blk)),
            pl.BlockSpec((None, g * WINDOW, kv_width), lambda s: (layer, s, 0)),
            pl.BlockSpec((None, g * WINDOW, kv_width), lambda s: (layer, s, 0)),
            pl.BlockSpec((rows, LANES), lambda s: (0, 0)),
            pl.BlockSpec((rows, LANES), lambda s: (0, 0)),
        ],
        out_specs=[
            pl.BlockSpec((rows, a_width), lambda s: (s, 0)),
            pl.BlockSpec((rows, kv_width), lambda s: (s, 0)),
        ],
        out_shape=[jax.ShapeDtypeStruct((dec_batch * dec_seq, a_width), BF16),
                   jax.ShapeDtypeStruct((dec_batch * dec_seq, kv_width), F32)],
        compiler_params=_params(("arbitrary",), 0),
        name=name,
    )(sinks, z, z, z, ck, cv, cos, sin)


RET_HEADS_PER_STEP = 4
RET_BLOCK = RET_HEADS_PER_STEP * R_DIM
RET_CHUNKS_PER_STEP = 4


def _decay_tables(length, nseq):
    log_g = np.log1p(-np.exp2(-5.0 - np.arange(R_HEADS, dtype=np.float64)))
    i = np.arange(length, dtype=np.float64)
    diff = i[:, None] - i[None, :]
    d = np.where(diff >= 0, np.exp(log_g[:, None, None] * np.maximum(diff, 0.0)), 0.0)
    d = np.einsum('st,hij->hsitj', np.eye(nseq), d).reshape(
        R_HEADS, nseq * length, nseq * length)
    q_dec = np.exp(log_g[:, None] * (i[None, :] + 1.0))
    k_dec = np.exp(log_g[:, None] * (length - 1.0 - i)[None, :])
    shape = (R_HEADS, nseq * length, R_DIM)
    qd = np.broadcast_to(np.tile(q_dec, (1, nseq))[:, :, None], shape)
    kd = np.broadcast_to(np.tile(k_dec, (1, nseq))[:, :, None], shape)
    g_len = np.exp(log_g * length)
    return tuple(jnp.asarray(t, F32) for t in (d, qd, kd, g_len))


def _group_norm_gate(o, gate):
    o = o * lax.rsqrt(jnp.mean(o * o, axis=-1, keepdims=True) + EPS)
    return o * (gate * jax.nn.sigmoid(gate))


def _ret_prompt_body(gl_ref, *refs):
    nb = R_HEADS // RET_HEADS_PER_STEP
    q_refs, k_refs, v_refs, g_refs = (refs[t * nb:(t + 1) * nb] for t in range(4))
    cos_ref, sin_ref, d_ref, qd_ref, kd_ref, o_ref, s_ref = refs[4 * nb:]
    c = pl.program_id(1)

    @pl.when(c == 0)
    def _():
        s_ref[...] = jnp.zeros_like(s_ref)

    scale = R_DIM ** -0.5
    for ck in range(RET_CHUNKS_PER_STEP):
        rs = slice(ck * R_CHUNK, (ck + 1) * R_CHUNK)
        cos = cos_ref[rs, :]
        sin = sin_ref[rs, :]
        for h in range(R_HEADS):
            blk = h // RET_HEADS_PER_STEP
            sl = slice((h % RET_HEADS_PER_STEP) * R_DIM, (h % RET_HEADS_PER_STEP + 1) * R_DIM)
            q = _retrot128(q_refs[blk][rs, sl].astype(F32), cos, sin)
            k = _retrot128(k_refs[blk][rs, sl].astype(F32), cos, sin) * scale
            v = v_refs[blk][rs, sl]
            qb = q.astype(BF16)
            state = s_ref[0, h]
            scores = lax.dot_general(qb, k.astype(BF16), NT_DIMS,
                                     preferred_element_type=F32) * d_ref[h]
            o = jnp.dot(scores.astype(BF16), v, preferred_element_type=F32)
            cross = jnp.dot(qb, state.astype(BF16), preferred_element_type=F32)
            o = o + cross * qd_ref[h]
            kd = (k * kd_ref[h]).astype(BF16)
            s_ref[0, h] = gl_ref[h] * state + lax.dot_general(
                kd, v, TN_DIMS, preferred_element_type=F32)
            o_ref[rs, h * R_DIM:(h + 1) * R_DIM] = _group_norm_gate(
                o, g_refs[blk][rs, sl].astype(F32)).astype(o_ref.dtype)


def _ret_prompt(z, cos, sin, tabs, *, col0, batch, seq, name):
    d, qd, kd, g_len = tabs
    out_rows = batch * seq
    rows = RET_CHUNKS_PER_STEP * R_CHUNK
    assert seq % rows == 0
    nc = seq // rows
    nb = R_HEADS // RET_HEADS_PER_STEP
    width = R_HEADS * R_DIM
    blks = [(col0 + t * width) // RET_BLOCK + i for t in range(4) for i in range(nb)]

    def zspec(blk):
        return pl.BlockSpec((rows, RET_BLOCK), lambda b, c: (b * nc + c, blk))

    def tspec():
        return pl.BlockSpec((R_HEADS, R_CHUNK, R_DIM), lambda b, c: (0, 0, 0))

    return pl.pallas_call(
        _ret_prompt_body,
        grid=(batch, nc),
        in_specs=[pl.BlockSpec(memory_space=pltpu.SMEM)]
        + [zspec(blk) for blk in blks]
        + [pl.BlockSpec((rows, R_DIM), lambda b, c: (c, 0)),
           pl.BlockSpec((rows, R_DIM), lambda b, c: (c, 0)),
           tspec(), tspec(), tspec()],
        out_specs=[
            pl.BlockSpec((rows, width), lambda b, c: (b * nc + c, 0)),
            pl.BlockSpec((1, R_HEADS, R_DIM, R_DIM), lambda b, c: (b, 0, 0, 0)),
        ],
        out_shape=[jax.ShapeDtypeStruct((out_rows, width), BF16),
                   jax.ShapeDtypeStruct((batch, R_HEADS, R_DIM, R_DIM), F32)],
        compiler_params=_params(("arbitrary", "arbitrary"), 0),
        name=name,
    )(g_len, *([z] * len(blks)), cos, sin, d, qd, kd)


def _ret_sample_body(gl_ref, q_ref, k_ref, v_ref, g_ref, cos_ref, sin_ref,
                     d_ref, qd_ref, kd_ref, s_ref, *rest, dec_seq):
    o_ref, so_ref = rest[-2:]
    hh = pl.program_id(1)
    cos = cos_ref[...]
    sin = sin_ref[...]
    nseq = R_CHUNK // dec_seq
    log_l = dec_seq.bit_length() - 1
    row_seq = lax.broadcasted_iota(jnp.int32, (R_CHUNK, R_DIM), 0) >> log_l
    scale = R_DIM ** -0.5
    for hl in range(RET_HEADS_PER_STEP):
        sl = slice(hl * R_DIM, (hl + 1) * R_DIM)
        q = _retrot128(q_ref[:, sl].astype(F32), cos, sin)
        k = _retrot128(k_ref[:, sl].astype(F32), cos, sin) * scale
        v = v_ref[:, sl]
        qb = q.astype(BF16)
        scores = lax.dot_general(qb, k.astype(BF16), NT_DIMS,
                                 preferred_element_type=F32) * d_ref[hl]
        o = jnp.dot(scores.astype(BF16), v, preferred_element_type=F32)
        kd = k * kd_ref[hl]
        gl = gl_ref[hh * RET_HEADS_PER_STEP + hl]
        cross = jnp.zeros((R_CHUNK, R_DIM), F32)
        for s in range(nseq):
            mine = row_seq == s
            state = s_ref[s, hl]
            cs = jnp.dot(qb, state.astype(BF16), preferred_element_type=F32)
            cross = jnp.where(mine, cs, cross)
            ks = jnp.where(mine, kd, 0.0).astype(BF16)
            so_ref[s, hl] = gl * state + lax.dot_general(
                ks, v, TN_DIMS, preferred_element_type=F32)
        o = o + cross * qd_ref[hl]
        o_ref[:, sl] = _group_norm_gate(o, g_ref[:, sl].astype(F32)).astype(o_ref.dtype)


def _ret_sample(z, state, layer, s_all, cos, sin, tabs, *, row0, col0, dec_batch, dec_seq,
                name):
    d, qd, kd, g_len = tabs
    nseq = R_CHUNK // dec_seq
    steps = dec_batch // nseq
    nh = R_HEADS // RET_HEADS_PER_STEP
    width = R_HEADS * R_DIM
    qb, kb, vb, gb = [(col0 + t * width) // RET_BLOCK for t in range(4)]
    r0 = row0 // R_CHUNK
    hp = RET_HEADS_PER_STEP

    def zspec(blk):
        return pl.BlockSpec((R_CHUNK, RET_BLOCK), lambda s, h: (r0 + s, blk + h))

    def tspec():
        return pl.BlockSpec((hp, R_CHUNK, R_DIM), lambda s, h: (h, 0, 0))

    sspec = pl.BlockSpec((None, nseq, hp, R_DIM, R_DIM), lambda s, h: (layer, s, h, 0, 0))
    args = [g_len, z, z, z, z, cos, sin, d, qd, kd, state]
    aliases = {}
    if s_all is not None:
        aliases[len(args)] = 1
        args.append(s_all)
    return pl.pallas_call(
        functools.partial(_ret_sample_body, dec_seq=dec_seq),
        grid=(steps, nh),
        in_specs=[
            pl.BlockSpec(memory_space=pltpu.SMEM),
            zspec(qb), zspec(kb), zspec(vb), zspec(gb),
            pl.BlockSpec((R_CHUNK, R_DIM), lambda s, h: (0, 0)),
            pl.BlockSpec((R_CHUNK, R_DIM), lambda s, h: (0, 0)),
            tspec(), tspec(), tspec(),
            sspec,
        ] + [pl.BlockSpec(memory_space=pl.ANY)] * (len(args) - 11),
        out_specs=[
            pl.BlockSpec((R_CHUNK, RET_BLOCK), lambda s, h: (s, h)),
            sspec,
        ],
        out_shape=[jax.ShapeDtypeStruct((dec_batch * dec_seq, width), BF16),
                   jax.ShapeDtypeStruct(state.shape, F32)],
        input_output_aliases=aliases,
        compiler_params=_params(("arbitrary", "arbitrary"),
                                4 * nseq * hp * R_DIM * R_DIM * 4 + (16 << 20)),
        name=name,
    )(*args)


def _gelu_tanh_times(x, gate):
    c = (2.0 / jnp.pi) ** 0.5
    t = jnp.tanh(x * (c + (c * 0.044715) * (x * x)))
    y = (0.5 * x) * gate
    return y + y * t


def _ffn_up_body(h_ref, wu_ref, wg_ref, cw_ref, cb_ref, e0_ref, e1_ref,
                 f_ref, tail_ref, us_ref, wubf_ref, wgbf_ref, carry_ref, ug_ref,
                 *, tm, n_tiles, seq, batch, ms, dec_seq):
    i = pl.program_id(1)
    tn = f_ref.shape[1]
    mp = batch * seq

    @pl.when(i == 0)
    def _():
        wubf_ref[...] = wu_ref[...].astype(BF16)
        wgbf_ref[...] = wg_ref[...].astype(BF16)
        carry_ref[...] = jnp.zeros_like(carry_ref)

    last = n_tiles - 1
    s0 = mp - last * tm
    assert 0 <= s0 and s0 + ms == tm and s0 % FIX_ROWS == 0
    starts = [k * seq for k in range(1, batch)]
    for r in starts:
        assert r % tm != 0 and r % FIX_ROWS == 0, "mid-tile, packed-tile aligned starts only"

    def conv_gate(u, gate, p1, p2, w, bias):
        conv = bias + w[0:1, :] * p2
        conv = conv + w[1:2, :] * p1
        conv = conv + w[2:3, :] * u
        return _gelu_tanh_times(conv, gate).astype(f_ref.dtype)

    cuts = [0] + [tm * a // FFN_ROW_SPLIT[-1] for a in FFN_ROW_SPLIT]
    assert cuts[-1] == tm and all(r % FIX_ROWS == 0 for r in cuts)
    for c in range(tn // MXU_COLS):
        cs = slice(c * MXU_COLS, (c + 1) * MXU_COLS)
        w = cw_ref[:, cs]
        bias = cb_ref[:, cs]
        for part in range(len(cuts) - 1):
            r0 = cuts[part]
            rs = slice(r0, cuts[part + 1])
            h = h_ref[rs, :]
            ug_ref[2 * c, rs, :] = jnp.dot(h, wubf_ref[:, cs], preferred_element_type=F32)
            ug_ref[2 * c + 1, rs, :] = jnp.dot(h, wgbf_ref[:, cs], preferred_element_type=F32)
            u = ug_ref[2 * c, rs, :]
            gate = ug_ref[2 * c + 1, rs, :]

            if part == 0:
                before = carry_ref[:, cs]
            else:
                before = ug_ref[2 * c, r0 - SUBLANES:r0, :]
            last1 = before[SUBLANES - 1:SUBLANES, :]
            last2 = before[SUBLANES - 2:SUBLANES - 1, :]
            f_ref[rs, cs] = conv_gate(u, gate, pltpu.roll(u, 1, 0), pltpu.roll(u, 2, 0),
                                      w, bias)
            uh = u[:FIX_ROWS, :]
            pos = lax.broadcasted_iota(jnp.int32, uh.shape, 0)
            prev1 = jnp.where(pos == 0, last1, pltpu.roll(uh, 1, 0))
            prev2 = jnp.where(pos == 0, last2,
                              jnp.where(pos == 1, last1, pltpu.roll(uh, 2, 0)))
            f_ref[r0:r0 + FIX_ROWS, cs] = conv_gate(
                uh, gate[:FIX_ROWS, :], prev1, prev2, w, bias)
        carry_ref[:, cs] = ug_ref[2 * c, tm - SUBLANES:, :]

    def rewrite(rows, p1_fn, p2_fn):
        for c in range(tn // MXU_COLS):
            cs = slice(c * MXU_COLS, (c + 1) * MXU_COLS)
            us = ug_ref[2 * c, rows, :]
            f_ref[rows, cs] = conv_gate(us, ug_ref[2 * c + 1, rows, :], p1_fn(us, cs),
                                        p2_fn(us, cs), cw_ref[:, cs], cb_ref[:, cs])

    for r in starts:
        @pl.when(i == r // tm)
        def _(off=r % tm):
            pos = lax.broadcasted_iota(jnp.int32, (FIX_ROWS, MXU_COLS), 0)
            rewrite(slice(off, off + FIX_ROWS),
                    lambda us, cs: jnp.where(pos == 0, 0.0, pltpu.roll(us, 1, 0)),
                    lambda us, cs: jnp.where(pos <= 1, 0.0, pltpu.roll(us, 2, 0)))

    @pl.when(i == last)
    def _():
        pos = lax.broadcasted_iota(jnp.int32, (ms, MXU_COLS), 0) & (dec_seq - 1)
        rewrite(slice(s0, tm),
                lambda us, cs: jnp.where(pos == 0, e1_ref[:, cs], pltpu.roll(us, 1, 0)),
                lambda us, cs: jnp.where(
                    pos == 0, e0_ref[:, cs],
                    jnp.where(pos == 1, e1_ref[:, cs], pltpu.roll(us, 2, 0))))
        for c in range(tn // MXU_COLS):
            us_ref[:, c * MXU_COLS:(c + 1) * MXU_COLS] = ug_ref[2 * c, s0:, :]

    for b in range(batch):
        r = (b + 1) * seq - SUBLANES

        @pl.when(i == r // tm)
        def _(b=b, off=r % tm):
            for c in range(tn // MXU_COLS):
                tail_ref[b * SUBLANES:(b + 1) * SUBLANES, c * MXU_COLS:(c + 1) * MXU_COLS] = (
                    ug_ref[2 * c, off:off + SUBLANES, :])


def _ffn_up(h, w_up, conv_w, conv_b, layer, e, *, tm, tn, seq, batch, dec_seq, name):
    m, k = h.shape
    d_ff = w_up.shape[2] // 2
    ms = e.shape[2]
    nj = d_ff // tn
    nt = m // tm
    body = functools.partial(_ffn_up_body, tm=tm, n_tiles=nt, seq=seq, batch=batch,
                             ms=ms, dec_seq=dec_seq)
    nbytes = (2 * tm * k * 2 + 4 * k * tn * 4 + 2 * k * tn * 2
              + 16 * tm * tn * 4 + 4 * ms * tn * 4 + (4 << 20))
    return pl.pallas_call(
        body,
        grid=(nj, nt),
        in_specs=[
            pl.BlockSpec((tm, k), lambda j, i: (i, 0)),
            pl.BlockSpec((None, k, tn), lambda j, i: (layer, 0, j)),
            pl.BlockSpec((None, k, tn), lambda j, i: (layer, 0, nj + j)),
            pl.BlockSpec((None, CONV_W, tn), lambda j, i: (layer, 0, j)),
            pl.BlockSpec((None, 1, tn), lambda j, i: (layer, 0, j)),
            pl.BlockSpec((None, None, ms, tn), lambda j, i: (layer, 0, 0, j)),
            pl.BlockSpec((None, None, ms, tn), lambda j, i: (layer, 1, 0, j)),
        ],
        out_specs=[
            pl.BlockSpec((tm, tn), lambda j, i: (i, j)),
            pl.BlockSpec((batch * SUBLANES, tn), lambda j, i: (0, j)),
            pl.BlockSpec((ms, tn), lambda j, i: (0, j)),
        ],
        out_shape=[jax.ShapeDtypeStruct((m, d_ff), BF16),
                   jax.ShapeDtypeStruct((batch * SUBLANES, d_ff), F32),
                   jax.ShapeDtypeStruct((ms, d_ff), F32)],
        scratch_shapes=[pltpu.VMEM((k, tn), BF16), pltpu.VMEM((k, tn), BF16),
                        pltpu.VMEM((SUBLANES, tn), F32),
                        pltpu.VMEM((2 * tn // MXU_COLS, tm, MXU_COLS), F32)],
        compiler_params=_params(("arbitrary", "arbitrary"), nbytes),
        name=name,
    )(h, w_up, w_up, conv_w, conv_b, e, e)


TM = 1408
TM_NORM = 768
TM_DOWN = 704
TM_IN = 2816
TN_IN = 512
TN_UP = 512
GATE_BLOCK = 512
TM_OUT = 352
FFN_ROW_SPLIT = (2, 4)
TN_OUT = 512


def kernel(x_prompt, x_sample, cache_win_k, cache_win_v, state_ret, state_conv,
           g_mix, w_in, sinks, w_proj_a, w_proj_b, w_o, g_ffn, w_up, conv_w,
           conv_b, w_down, g_final):
    batch, seq, d_model = x_prompt.shape
    dec_batch, dec_seq, _ = x_sample.shape
    depth = w_in.shape[0]
    d_ff = w_down.shape[1]
    mp = batch * seq
    ms = dec_batch * dec_seq
    a_width = A_HEADS * A_HEAD_DIM
    kv_width = A_KV_HEADS * A_HEAD_DIM
    r_width = R_HEADS * R_DIM
    ret_col0 = a_width + 2 * kv_width
    ga_col = ret_col0 + 4 * r_width
    gb_col = ga_col + d_model
    assert dec_seq & (dec_seq - 1) == 0 and R_CHUNK % dec_seq == 0

    pos_p = jnp.arange(seq, dtype=F32)
    pos_s = PAST_LEN + jnp.arange(dec_seq, dtype=F32)
    rope_p = _rope_tables(pos_p)
    rope_s = tuple(jnp.tile(t, (SWA_SAMPLE_SEQS, 1)) for t in _rope_tables(pos_s))
    rrot_p = _retrot_tables(pos_p)
    nseq = R_CHUNK // dec_seq
    rrot_s = tuple(jnp.tile(t, (nseq, 1)) for t in _retrot_tables(pos_s))
    tabs_p = _decay_tables(R_CHUNK, 1)
    tabs_s = _decay_tables(dec_seq, nseq)
    g_mix3 = g_mix.reshape(depth, 1, d_model)
    g_ffn3 = g_ffn.reshape(depth, 1, d_model)
    conv_b3 = conv_b.reshape(depth, 1, d_ff)

    e_conv = jnp.repeat(jnp.swapaxes(state_conv, 1, 2), dec_seq, axis=2)
    v_col0 = a_width + kv_width

    kp_l, vp_l, sp_l, cp_l = [], [], [], []
    ks_l, vs_l, cs_l = [], [], []
    s_all = None
    for l in range(depth):
        if l == 0:
            x, h = _join_norm(x_prompt.reshape(mp, d_model), x_sample.reshape(ms, d_model),
                              g_mix3, l, name="join_norm_mix0")
        else:
            h = _rmsnorm(x, g_mix3, l, BF16, tm=TM_NORM, name=f"norm_mix{l}")
        z = _matmul(h, w_in, l, tm=TM_IN, tn=TN_IN, out_dtype=BF16, name=f"proj_in{l}")

        oa_p, kr_p = _swa_prompt(z, sinks, l, *rope_p, batch=batch, seq=seq,
                                 name=f"swa_prompt{l}")
        oa_s, kr_s = _swa_sample(z, cache_win_k, cache_win_v, sinks, l, *rope_s,
                                 row0=mp, dec_batch=dec_batch, dec_seq=dec_seq,
                                 name=f"swa_sample{l}")
        ob_p, s_p = _ret_prompt(z, *rrot_p, tabs_p, col0=ret_col0, batch=batch,
                                seq=seq, name=f"ret_prompt{l}")
        ob_s, s_all = _ret_sample(z, state_ret, l, s_all, *rrot_s, tabs_s, row0=mp,
                                  col0=ret_col0, dec_batch=dec_batch, dec_seq=dec_seq,
                                  name=f"ret_sample{l}")
        merged = _merge(oa_p, oa_s, ob_p, ob_s, z, w_proj_a, w_proj_b, l, ga_col=ga_col,
                        gb_col=gb_col, tg=GATE_BLOCK, name=f"merge{l}")
        x, h2 = _matmul_res_norm(merged, w_o, l, x, g_ffn3, tm=TM_OUT,
                                 name=f"proj_out_norm{l}")
        f, u_tail, u_s = _ffn_up(h2, w_up, conv_w, conv_b3, l, e_conv, tm=TM, tn=TN_UP,
                                 seq=seq, batch=batch, dec_seq=dec_seq, name=f"ffn_up{l}")
        x = _matmul(f, w_down, l, tm=TM_DOWN, tn=TN_OUT, out_dtype=F32, residual=x,
                    name=f"proj_down{l}")

        v_p = jnp.stack([lax.slice(z, ((b + 1) * seq - WINDOW, v_col0), ((b + 1) * seq, ret_col0))
                         for b in range(batch)])
        v_s = lax.slice(z, (mp, v_col0), (mp + ms, ret_col0))
        kp_l.append(kr_p.reshape(batch, WINDOW, A_KV_HEADS, A_HEAD_DIM))
        vp_l.append(v_p.astype(F32).reshape(batch, WINDOW, A_KV_HEADS, A_HEAD_DIM))
        sp_l.append(s_p)
        cp_l.append(u_tail.reshape(batch, SUBLANES, d_ff)[:, -(CONV_W - 1):])
        ks_l.append(kr_s.reshape(dec_batch, dec_seq, A_KV_HEADS, A_HEAD_DIM))
        vs_l.append(v_s.astype(F32).reshape(dec_batch, dec_seq, A_KV_HEADS, A_HEAD_DIM))
        cs_l.append(u_s.reshape(dec_batch, dec_seq, d_ff)[:, -(CONV_W - 1):])

    y_prompt, y_sample = _split_norm(x, g_final, mp=mp, name="norm_final")
    y_prompt = y_prompt.reshape(batch, seq, d_model)
    y_sample = y_sample.reshape(dec_batch, dec_seq, d_model)
    return (y_prompt, y_sample,
            jnp.stack(kp_l), jnp.stack(vp_l), jnp.stack(sp_l), jnp.stack(cp_l),
            jnp.stack(ks_l), jnp.stack(vs_l), s_all, jnp.stack(cs_l))
```

```python
import functools

import jax
import jax.numpy as jnp
import numpy as np
from jax import lax
from jax.experimental import pallas as pl
from jax.experimental.pallas import tpu as pltpu

F32 = jnp.float32
BF16 = jnp.bfloat16

LANES = 128
SUBLANES = 8
MXU_COLS = 256
FIX_ROWS = 2 * SUBLANES
VMEM_LIMIT_CAP = 56 * 1024 * 1024

WINDOW = 128
A_HEADS = 16
A_KV_HEADS = 4
A_HEAD_DIM = 64
A_GROUP = A_HEADS // A_KV_HEADS
R_HEADS = 8
R_DIM = 128
R_CHUNK = 128
CONV_W = 3
EPS = 1e-6
ROPE_THETA = 10000.0
PAST_LEN = 16384

NT_DIMS = (((1,), (1,)), ((), ()))
TN_DIMS = (((0,), (0,)), ((), ()))


def _vmem_limit(nbytes):
    return int(min(VMEM_LIMIT_CAP, max(32 * 1024 * 1024, nbytes)))


def _params(semantics, nbytes):
    return pltpu.CompilerParams(dimension_semantics=semantics,
                                vmem_limit_bytes=_vmem_limit(nbytes))


def _rmsnorm_body(x_ref, g_ref, o_ref):
    x = x_ref[...]
    ms = jnp.mean(x * x, axis=-1, keepdims=True)
    o_ref[...] = (x * lax.rsqrt(ms + EPS) * g_ref[...]).astype(o_ref.dtype)


def _rmsnorm(x, g, layer, out_dtype, *, tm, name):
    m, d = x.shape
    return pl.pallas_call(
        _rmsnorm_body,
        grid=(m // tm,),
        in_specs=[pl.BlockSpec((tm, d), lambda i: (i, 0)),
                  pl.BlockSpec((None, 1, d), lambda i: (layer, 0, 0))],
        out_specs=pl.BlockSpec((tm, d), lambda i: (i, 0)),
        out_shape=jax.ShapeDtypeStruct((m, d), out_dtype),
        compiler_params=_params(("arbitrary",), 6 * tm * d * 4),
        name=name,
    )(x, g)


def _join_norm_body(xp_ref, xs_ref, g_ref, x_ref, h_ref, *, prompt_tiles):
    i = pl.program_id(0)

    def emit(x):
        ms = jnp.mean(x * x, axis=-1, keepdims=True)
        x_ref[...] = x
        h_ref[...] = (x * lax.rsqrt(ms + EPS) * g_ref[...]).astype(h_ref.dtype)

    @pl.when(i < prompt_tiles)
    def _():
        emit(xp_ref[...])

    @pl.when(i >= prompt_tiles)
    def _():
        emit(xs_ref[...])


def _join_norm(xp, xs, g, layer, *, name):
    mp, d = xp.shape
    ms = xs.shape[0]
    assert mp % ms == 0
    pt = mp // ms
    return pl.pallas_call(
        functools.partial(_join_norm_body, prompt_tiles=pt),
        grid=(pt + 1,),
        in_specs=[pl.BlockSpec((ms, d), lambda i: (jnp.minimum(i, pt - 1), 0)),
                  pl.BlockSpec((ms, d), lambda i: (0, 0)),
                  pl.BlockSpec((None, 1, d), lambda i: (layer, 0, 0))],
        out_specs=[pl.BlockSpec((ms, d), lambda i: (i, 0)),
                   pl.BlockSpec((ms, d), lambda i: (i, 0))],
        out_shape=[jax.ShapeDtypeStruct((mp + ms, d), F32),
                   jax.ShapeDtypeStruct((mp + ms, d), BF16)],
        compiler_params=_params(("arbitrary",), 0),
        name=name,
    )(xp, xs, g)


def _split_norm_body(x_ref, g_ref, yp_ref, ys_ref, *, prompt_tiles):
    i = pl.program_id(0)
    x = x_ref[...]
    ms = jnp.mean(x * x, axis=-1, keepdims=True)
    y = x * lax.rsqrt(ms + EPS) * g_ref[...]

    @pl.when(i < prompt_tiles)
    def _():
        yp_ref[...] = y

    @pl.when(i >= prompt_tiles)
    def _():
        ys_ref[...] = y


def _split_norm(x, g, *, mp, name):
    m, d = x.shape
    ms = m - mp
    assert mp % ms == 0
    pt = mp // ms
    return pl.pallas_call(
        functools.partial(_split_norm_body, prompt_tiles=pt),
        grid=(pt + 1,),
        in_specs=[pl.BlockSpec((ms, d), lambda i: (i, 0)),
                  pl.BlockSpec((1, d), lambda i: (0, 0))],
        out_specs=[pl.BlockSpec((ms, d), lambda i: (jnp.minimum(i, pt - 1), 0)),
                   pl.BlockSpec((ms, d), lambda i: (0, 0))],
        out_shape=[jax.ShapeDtypeStruct((mp, d), F32),
                   jax.ShapeDtypeStruct((ms, d), F32)],
        compiler_params=_params(("arbitrary",), 0),
        name=name,
    )(x, g.reshape(1, d))


def _mm_body(x_ref, w_ref, o_ref, wbf_ref):
    @pl.when(pl.program_id(1) == 0)
    def _():
        wbf_ref[...] = w_ref[...].astype(BF16)

    o_ref[...] = jnp.dot(x_ref[...], wbf_ref[...],
                         preferred_element_type=F32).astype(o_ref.dtype)


def _mm_res_body(x_ref, w_ref, r_ref, o_ref, wbf_ref):
    @pl.when(pl.program_id(1) == 0)
    def _():
        wbf_ref[...] = w_ref[...].astype(BF16)

    o_ref[...] = r_ref[...] + jnp.dot(x_ref[...], wbf_ref[...],
                                      preferred_element_type=F32)


def _matmul(x, w, layer, *, tm, tn, out_dtype, residual=None, name):
    m, k = x.shape
    n = w.shape[2]
    grid = (n // tn, m // tm)
    in_specs = [pl.BlockSpec((tm, k), lambda j, i: (i, 0)),
                pl.BlockSpec((None, k, tn), lambda j, i: (layer, 0, j))]
    args = [x, w]
    body = _mm_body
    if residual is not None:
        in_specs.append(pl.BlockSpec((tm, tn), lambda j, i: (i, j)))
        args.append(residual)
        body = _mm_res_body
    nbytes = (2 * tm * k * 2 + 2 * k * tn * 4 + k * tn * 2
              + 6 * tm * tn * 4 + (4 << 20))
    return pl.pallas_call(
        body,
        grid=grid,
        in_specs=in_specs,
        out_specs=pl.BlockSpec((tm, tn), lambda j, i: (i, j)),
        out_shape=jax.ShapeDtypeStruct((m, n), out_dtype),
        scratch_shapes=[pltpu.VMEM((k, tn), BF16)],
        compiler_params=_params(("arbitrary", "arbitrary"), nbytes),
        name=name,
    )(*args)


def _out_norm_body(m_ref, w_ref, r_ref, g_ref, x_ref, h_ref, wbf_ref):
    @pl.when(pl.program_id(0) == 0)
    def _():
        wbf_ref[...] = w_ref[...].astype(BF16)

    x = r_ref[...] + jnp.dot(m_ref[...], wbf_ref[...], preferred_element_type=F32)
    x_ref[...] = x
    ms = jnp.mean(x * x, axis=-1, keepdims=True)
    h_ref[...] = (x * lax.rsqrt(ms + EPS) * g_ref[...]).astype(h_ref.dtype)


def _matmul_res_norm(a, w, layer, residual, g, *, tm, name):
    m, k = a.shape
    n = w.shape[2]
    nbytes = (k * n * 4 + k * n * 2 + 2 * tm * k * 2 + 4 * tm * n * 4
              + 2 * tm * n * 2 + 2 * tm * n * 4 + (4 << 20))
    return pl.pallas_call(
        _out_norm_body,
        grid=(m // tm,),
        in_specs=[pl.BlockSpec((tm, k), lambda i: (i, 0)),
                  pl.BlockSpec((None, k, n), lambda i: (layer, 0, 0),
                               pipeline_mode=pl.Buffered(1)),
                  pl.BlockSpec((tm, n), lambda i: (i, 0)),
                  pl.BlockSpec((None, 1, n), lambda i: (layer, 0, 0))],
        out_specs=[pl.BlockSpec((tm, n), lambda i: (i, 0)),
                   pl.BlockSpec((tm, n), lambda i: (i, 0))],
        out_shape=[jax.ShapeDtypeStruct((m, n), F32),
                   jax.ShapeDtypeStruct((m, n), BF16)],
        scratch_shapes=[pltpu.VMEM((k, n), BF16)],
        compiler_params=_params(("arbitrary",), nbytes),
        name=name,
    )(a, w, residual, g)


def _merge_body(*refs, n_gate, tg, prompt_tiles):
    oap_ref, oas_ref, obp_ref, obs_ref = refs[:4]
    ga_refs = refs[4:4 + n_gate]
    gb_refs = refs[4 + n_gate:4 + 2 * n_gate]
    wa_ref, wb_ref, o_ref, wabf_ref, wbbf_ref = refs[4 + 2 * n_gate:]
    i = pl.program_id(0)

    @pl.when(i == 0)
    def _():
        wabf_ref[...] = wa_ref[...].astype(BF16)
        wbbf_ref[...] = wb_ref[...].astype(BF16)

    is_prompt = i < prompt_tiles
    oa = jnp.where(is_prompt, oap_ref[...], oas_ref[...])
    ob = jnp.where(is_prompt, obp_ref[...], obs_ref[...])
    for c in range(n_gate):
        cs = slice(c * tg, (c + 1) * tg)
        a = jnp.dot(oa, wabf_ref[:, cs], preferred_element_type=F32)
        b = jnp.dot(ob, wbbf_ref[:, cs], preferred_element_type=F32)
        ga = jax.nn.sigmoid(ga_refs[c][...].astype(F32))
        gb = jax.nn.sigmoid(gb_refs[c][...].astype(F32))
        o_ref[:, cs] = (ga * a + gb * b).astype(o_ref.dtype)


def _merge(oa_p, oa_s, ob_p, ob_s, z, wa, wb, layer, *, ga_col, gb_col, tg, name):
    mp, k = oa_p.shape
    tm = oa_s.shape[0]
    assert mp % tm == 0
    pt = mp // tm
    m = mp + tm
    n = wa.shape[2]
    assert ga_col % tg == 0 and gb_col % tg == 0 and n % tg == 0
    n_gate = n // tg
    nbytes = (2 * k * n * 4 + 2 * k * n * 2 + 4 * tm * k * 2 + 4 * tm * n * 2
              + 2 * tm * n * 2 + 6 * tm * tg * 4 + (4 << 20))

    def gate_spec(blk):
        return pl.BlockSpec((tm, tg), lambda i: (i, blk))

    def gate_specs(col):
        return [gate_spec(col // tg + c) for c in range(n_gate)]

    def wspec():
        return pl.BlockSpec((None, k, n), lambda i: (layer, 0, 0), pipeline_mode=pl.Buffered(1))

    def prompt_spec():
        return pl.BlockSpec((tm, k), lambda i: (jnp.minimum(i, pt - 1), 0))

    def sample_spec():
        return pl.BlockSpec((tm, k), lambda i: (0, 0))

    return pl.pallas_call(
        functools.partial(_merge_body, n_gate=n_gate, tg=tg, prompt_tiles=pt),
        grid=(m // tm,),
        in_specs=[prompt_spec(), sample_spec(), prompt_spec(), sample_spec()]
        + gate_specs(ga_col) + gate_specs(gb_col) + [wspec(), wspec()],
        out_specs=pl.BlockSpec((tm, n), lambda i: (i, 0)),
        out_shape=jax.ShapeDtypeStruct((m, n), BF16),
        scratch_shapes=[pltpu.VMEM((k, n), BF16), pltpu.VMEM((k, n), BF16)],
        compiler_params=_params(("arbitrary",), nbytes),
        name=name,
    )(oa_p, oa_s, ob_p, ob_s, *([z] * (2 * n_gate)), wa, wb)


def _rope_tables(pos):
    half = A_HEAD_DIM // 2
    inv = ROPE_THETA ** (-2.0 * jnp.arange(half, dtype=F32) / A_HEAD_DIM)
    ang = pos[:, None] * inv[None, :]
    c, s = jnp.cos(ang), jnp.sin(ang)
    cos = jnp.concatenate([c, c, c, c], axis=-1)
    sin = jnp.concatenate([-s, s, -s, s], axis=-1)
    return cos, sin


def _retrot_tables(pos):
    half = R_DIM // 2
    inv = 1.0 / (10000.0 ** jnp.linspace(0.0, 1.0, half, dtype=F32))
    ang = pos[:, None] * inv[None, :]
    c, s = jnp.cos(ang), jnp.sin(ang)
    cos = jnp.stack([c, c], axis=-1).reshape(pos.shape[0], R_DIM)
    sin = jnp.stack([-s, s], axis=-1).reshape(pos.shape[0], R_DIM)
    return cos, sin


def _rope128(x, cos, sin):
    lane = lax.broadcasted_iota(jnp.int32, x.shape, 1)
    first = (lane & (A_HEAD_DIM - 1)) < (A_HEAD_DIM // 2)
    partner = jnp.where(first, pltpu.roll(x, LANES - 32, 1), pltpu.roll(x, 32, 1))
    return x * cos + partner * sin


def _retrot128(x, cos, sin):
    lane = lax.broadcasted_iota(jnp.int32, x.shape, 1)
    even = (lane & 1) == 0
    partner = jnp.where(even, pltpu.roll(x, LANES - 1, 1), pltpu.roll(x, 1, 1))
    return x * cos + partner * sin


def _swa_heads(q_chunks, score_fn, out_fn, sink_ref, layer):
    rows = q_chunks[0].shape[0]
    lane = lax.broadcasted_iota(jnp.int32, (rows, LANES), 1)
    low = lane < A_HEAD_DIM
    outs = []
    for pair in range(A_HEADS // 2):
        halves = []
        for hpos in range(2):
            h = 2 * pair + hpos
            kv = h // A_GROUP
            c, kpos = kv // 2, kv % 2
            qc = q_chunks[pair]
            if hpos != kpos:
                qc = pltpu.roll(qc, A_HEAD_DIM, 1)
            keep = low if kpos == 0 else jnp.logical_not(low)
            qm = jnp.where(keep, qc, 0.0).astype(BF16)
            s = score_fn(qm, c)
            sink = sink_ref[layer, h]
            m = jnp.maximum(jnp.max(s, axis=-1, keepdims=True), sink)
            p = jnp.exp(s - m)
            pv, psum = out_fn(p, c)
            o = pv / (psum + jnp.exp(sink - m))
            if hpos != kpos:
                o = pltpu.roll(o, A_HEAD_DIM, 1)
            halves.append(o)
        outs.append(jnp.where(low, halves[0], halves[1]))
    return outs


def _swa_prompt_body(sink_ref, q_ref, k_ref, vc_ref, vp_ref, cos_ref, sin_ref,
                     o_ref, kr_ref, kprev_ref, *, layer):
    n = pl.program_id(1)
    cos = cos_ref[...]
    sin = sin_ref[...]

    @pl.when(n == 0)
    def _():
        kprev_ref[...] = jnp.zeros_like(kprev_ref)

    k_cur, k_both, v_both = [], [], []
    for c in range(2):
        sl = slice(c * LANES, (c + 1) * LANES)
        kc = _rope128(k_ref[:, sl].astype(F32), cos, sin)
        kr_ref[:, sl] = kc
        k_cur.append(kc.astype(BF16))
        k_both.append(jnp.concatenate([kprev_ref[:, sl], k_cur[c]], axis=0))
        v_both.append(jnp.concatenate([vp_ref[:, sl], vc_ref[:, sl]], axis=0))

    scale = A_HEAD_DIM ** -0.5
    q_chunks = [
        _rope128(q_ref[:, c * LANES:(c + 1) * LANES].astype(F32), cos, sin) * scale
        for c in range(A_HEADS // 2)]

    qi = lax.broadcasted_iota(jnp.int32, (WINDOW, WINDOW), 0)
    kj = lax.broadcasted_iota(jnp.int32, (WINDOW, WINDOW), 1)
    from_cur = kj <= qi
    prev_bias = jnp.where(n > 0, 0.0, -jnp.inf)

    def scores(qm, c):
        s = lax.dot_general(qm, k_both[c], NT_DIMS, preferred_element_type=F32)
        return jnp.where(from_cur, s[:, WINDOW:], s[:, :WINDOW] + prev_bias)

    def weighted_values(p, c):
        p_both = jnp.concatenate([jnp.where(from_cur, 0.0, p), jnp.where(from_cur, p, 0.0)],
                                 axis=1).astype(BF16)
        return (jnp.dot(p_both, v_both[c], preferred_element_type=F32),
                jnp.sum(p, axis=-1, keepdims=True))

    outs = _swa_heads(q_chunks, scores, weighted_values, sink_ref, layer)
    for c, o in enumerate(outs):
        o_ref[:, c * LANES:(c + 1) * LANES] = o.astype(o_ref.dtype)
    for c in range(2):
        kprev_ref[:, c * LANES:(c + 1) * LANES] = k_cur[c]


def _swa_prompt(z, sinks, layer, cos, sin, *, batch, seq, name):
    nb = seq // WINDOW
    a_width = A_HEADS * A_HEAD_DIM
    kv_width = A_KV_HEADS * A_HEAD_DIM
    k_blk = a_width // kv_width
    v_blk = k_blk + 1
    rows = batch * seq
    return pl.pallas_call(
        functools.partial(_swa_prompt_body, layer=layer),
        grid=(batch, nb),
        in_specs=[
            pl.BlockSpec(memory_space=pltpu.SMEM),
            pl.BlockSpec((WINDOW, a_width), lambda b, n: (b * nb + n, 0)),
            pl.BlockSpec((WINDOW, kv_width), lambda b, n: (b * nb + n, k_blk)),
            pl.BlockSpec((WINDOW, kv_width), lambda b, n: (b * nb + n, v_blk)),
            pl.BlockSpec((WINDOW, kv_width),
                         lambda b, n: (b * nb + jnp.maximum(n - 1, 0), v_blk)),
            pl.BlockSpec((WINDOW, LANES), lambda b, n: (n, 0)),
            pl.BlockSpec((WINDOW, LANES), lambda b, n: (n, 0)),
        ],
        out_specs=[
            pl.BlockSpec((WINDOW, a_width), lambda b, n: (b * nb + n, 0)),
            pl.BlockSpec((WINDOW, kv_width), lambda b, n: (b, 0)),
        ],
        out_shape=[jax.ShapeDtypeStruct((rows, a_width), BF16),
                   jax.ShapeDtypeStruct((batch * WINDOW, kv_width), F32)],
        scratch_shapes=[pltpu.VMEM((WINDOW, kv_width), BF16)],
        compiler_params=_params(("arbitrary", "arbitrary"), 0),
        name=name,
    )(sinks, z, z, z, z, cos, sin)


SWA_SAMPLE_SEQS = 8


def _swa_sample_body(sink_ref, q_ref, k_ref, v_ref, ck_ref, cv_ref, cos_ref,
                     sin_ref, o_ref, kr_ref, *, dec_seq, layer):
    cos = cos_ref[...]
    sin = sin_ref[...]
    g = SWA_SAMPLE_SEQS
    rows = g * dec_seq
    ncache = g * WINDOW
    k_chunks, v_chunks = [], []
    for c in range(2):
        sl = slice(c * LANES, (c + 1) * LANES)
        kc = _rope128(k_ref[:, sl].astype(F32), cos, sin)
        kr_ref[:, sl] = kc
        k_chunks.append(jnp.concatenate(
            [ck_ref[:, sl].astype(BF16), kc.astype(BF16)], axis=0))
        v_chunks.append(jnp.concatenate(
            [cv_ref[:, sl].astype(BF16), v_ref[:, sl]], axis=0))

    scale = A_HEAD_DIM ** -0.5
    q_chunks = [
        _rope128(q_ref[:, c * LANES:(c + 1) * LANES].astype(F32), cos, sin) * scale
        for c in range(A_HEADS // 2)]

    shape = (rows, ncache + rows)
    r = lax.broadcasted_iota(jnp.int32, shape, 0)
    col = lax.broadcasted_iota(jnp.int32, shape, 1)
    log_l = dec_seq.bit_length() - 1
    log_w = WINDOW.bit_length() - 1
    seq_q = r >> log_l
    i = r & (dec_seq - 1)
    in_cache = col < ncache
    cnew = col - ncache
    seq_k = jnp.where(in_cache, col >> log_w, cnew >> log_l)
    ok = ((in_cache & ((col & (WINDOW - 1)) > i))
          | (jnp.logical_not(in_cache) & ((cnew & (dec_seq - 1)) <= i)))
    valid = (seq_k == seq_q) & ok

    def scores(qm, c):
        s = lax.dot_general(qm, k_chunks[c], NT_DIMS, preferred_element_type=F32)
        return jnp.where(valid, s, -jnp.inf)

    def weighted_values(p, c):
        return (jnp.dot(p.astype(BF16), v_chunks[c], preferred_element_type=F32),
                jnp.sum(p, axis=-1, keepdims=True))

    outs = _swa_heads(q_chunks, scores, weighted_values, sink_ref, layer)
    for c, o in enumerate(outs):
        o_ref[:, c * LANES:(c + 1) * LANES] = o.astype(o_ref.dtype)


def _swa_sample(z, cache_k, cache_v, sinks, layer, cos, sin, *, row0, dec_batch, dec_seq, name):
    g = SWA_SAMPLE_SEQS
    rows = g * dec_seq
    a_width = A_HEADS * A_HEAD_DIM
    kv_width = A_KV_HEADS * A_HEAD_DIM
    k_blk = a_width // kv_width
    v_blk = k_blk + 1
    r0 = row0 // rows
    depth = cache_k.shape[0]
    ck = cache_k.reshape(depth, dec_batch * WINDOW, kv_width)
    cv = cache_v.reshape(depth, dec_batch * WINDOW, kv_width)
    return pl.pallas_call(
        functools.partial(_swa_sample_body, dec_seq=dec_seq, layer=layer),
        grid=(dec_batch // g,),
        in_specs=[
            pl.BlockSpec(memory_space=pltpu.SMEM),
            pl.BlockSpec((rows, a_width), lambda s: (r0 + s, 0)),
            pl.BlockSpec((rows, kv_width), lambda s: (r0 + s, k_blk)),
            pl.BlockSpec((rows, kv_width), lambda s: (r0 + s, v_blk)),
            pl.BlockSpec((None, g * WINDOW, kv_width), lambda s: (layer, s, 0)),
            pl.BlockSpec((None, g * WINDOW, kv_width), lambda s: (layer, s, 0)),
            pl.BlockSpec((rows, LANES), lambda s: (0, 0)),
            pl.BlockSpec((rows, LANES), lambda s: (0, 0)),
        ],
        out_specs=[
            pl.BlockSpec((rows, a_width), lambda s: (s, 0)),
            pl.BlockSpec((rows, kv_width), lambda s: (s, 0)),
        ],
        out_shape=[jax.ShapeDtypeStruct((dec_batch * dec_seq, a_width), BF16),
                   jax.ShapeDtypeStruct((dec_batch * dec_seq, kv_width), F32)],
        compiler_params=_params(("arbitrary",), 0),
        name=name,
    )(sinks, z, z, z, ck, cv, cos, sin)


RET_HEADS_PER_STEP = 4
RET_BLOCK = RET_HEADS_PER_STEP * R_DIM
RET_CHUNKS_PER_STEP = 4


def _decay_tables(length, nseq):
    log_g = np.log1p(-np.exp2(-5.0 - np.arange(R_HEADS, dtype=np.float64)))
    i = np.arange(length, dtype=np.float64)
    diff = i[:, None] - i[None, :]
    d = np.where(diff >= 0, np.exp(log_g[:, None, None] * np.maximum(diff, 0.0)), 0.0)
    d = np.einsum('st,hij->hsitj', np.eye(nseq), d).reshape(
        R_HEADS, nseq * length, nseq * length)
    q_dec = np.exp(log_g[:, None] * (i[None, :] + 1.0))
    k_dec = np.exp(log_g[:, None] * (length - 1.0 - i)[None, :])
    shape = (R_HEADS, nseq * length, R_DIM)
    qd = np.broadcast_to(np.tile(q_dec, (1, nseq))[:, :, None], shape)
    kd = np.broadcast_to(np.tile(k_dec, (1, nseq))[:, :, None], shape)
    g_len = np.exp(log_g * length)
    return tuple(jnp.asarray(t, F32) for t in (d, qd, kd, g_len))


def _group_norm_gate(o, gate):
    o = o * lax.rsqrt(jnp.mean(o * o, axis=-1, keepdims=True) + EPS)
    return o * (gate * jax.nn.sigmoid(gate))


def _ret_prompt_body(gl_ref, *refs):
    nb = R_HEADS // RET_HEADS_PER_STEP
    q_refs, k_refs, v_refs, g_refs = (refs[t * nb:(t + 1) * nb] for t in range(4))
    cos_ref, sin_ref, d_ref, qd_ref, kd_ref, o_ref, s_ref = refs[4 * nb:]
    c = pl.program_id(1)

    @pl.when(c == 0)
    def _():
        s_ref[...] = jnp.zeros_like(s_ref)

    scale = R_DIM ** -0.5
    for ck in range(RET_CHUNKS_PER_STEP):
        rs = slice(ck * R_CHUNK, (ck + 1) * R_CHUNK)
        cos = cos_ref[rs, :]
        sin = sin_ref[rs, :]
        for h in range(R_HEADS):
            blk = h // RET_HEADS_PER_STEP
            sl = slice((h % RET_HEADS_PER_STEP) * R_DIM, (h % RET_HEADS_PER_STEP + 1) * R_DIM)
            q = _retrot128(q_refs[blk][rs, sl].astype(F32), cos, sin)
            k = _retrot128(k_refs[blk][rs, sl].astype(F32), cos, sin) * scale
            v = v_refs[blk][rs, sl]
            qb = q.astype(BF16)
            state = s_ref[0, h]
            scores = lax.dot_general(qb, k.astype(BF16), NT_DIMS,
                                     preferred_element_type=F32) * d_ref[h]
            o = jnp.dot(scores.astype(BF16), v, preferred_element_type=F32)
            cross = jnp.dot(qb, state.astype(BF16), preferred_element_type=F32)
            o = o + cross * qd_ref[h]
            kd = (k * kd_ref[h]).astype(BF16)
            s_ref[0, h] = gl_ref[h] * state + lax.dot_general(
                kd, v, TN_DIMS, preferred_element_type=F32)
            o_ref[rs, h * R_DIM:(h + 1) * R_DIM] = _group_norm_gate(
                o, g_refs[blk][rs, sl].astype(F32)).astype(o_ref.dtype)


def _ret_prompt(z, cos, sin, tabs, *, col0, batch, seq, name):
    d, qd, kd, g_len = tabs
    out_rows = batch * seq
    rows = RET_CHUNKS_PER_STEP * R_CHUNK
    assert seq % rows == 0
    nc = seq // rows
    nb = R_HEADS // RET_HEADS_PER_STEP
    width = R_HEADS * R_DIM
    blks = [(col0 + t * width) // RET_BLOCK + i for t in range(4) for i in range(nb)]

    def zspec(blk):
        return pl.BlockSpec((rows, RET_BLOCK), lambda b, c: (b * nc + c, blk))

    def tspec():
        return pl.BlockSpec((R_HEADS, R_CHUNK, R_DIM), lambda b, c: (0, 0, 0))

    return pl.pallas_call(
        _ret_prompt_body,
        grid=(batch, nc),
        in_specs=[pl.BlockSpec(memory_space=pltpu.SMEM)]
        + [zspec(blk) for blk in blks]
        + [pl.BlockSpec((rows, R_DIM), lambda b, c: (c, 0)),
           pl.BlockSpec((rows, R_DIM), lambda b, c: (c, 0)),
           tspec(), tspec(), tspec()],
        out_specs=[
            pl.BlockSpec((rows, width), lambda b, c: (b * nc + c, 0)),
            pl.BlockSpec((1, R_HEADS, R_DIM, R_DIM), lambda b, c: (b, 0, 0, 0)),
        ],
        out_shape=[jax.ShapeDtypeStruct((out_rows, width), BF16),
                   jax.ShapeDtypeStruct((batch, R_HEADS, R_DIM, R_DIM), F32)],
        compiler_params=_params(("arbitrary", "arbitrary"), 0),
        name=name,
    )(g_len, *([z] * len(blks)), cos, sin, d, qd, kd)


def _ret_sample_body(gl_ref, q_ref, k_ref, v_ref, g_ref, cos_ref, sin_ref,
                     d_ref, qd_ref, kd_ref, s_ref, *rest, dec_seq):
    o_ref, so_ref = rest[-2:]
    hh = pl.program_id(1)
    cos = cos_ref[...]
    sin = sin_ref[...]
    nseq = R_CHUNK // dec_seq
    log_l = dec_seq.bit_length() - 1
    row_seq = lax.broadcasted_iota(jnp.int32, (R_CHUNK, R_DIM), 0) >> log_l
    scale = R_DIM ** -0.5
    for hl in range(RET_HEADS_PER_STEP):
        sl = slice(hl * R_DIM, (hl + 1) * R_DIM)
        q = _retrot128(q_ref[:, sl].astype(F32), cos, sin)
        k = _retrot128(k_ref[:, sl].astype(F32), cos, sin) * scale
        v = v_ref[:, sl]
        qb = q.astype(BF16)
        scores = lax.dot_general(qb, k.astype(BF16), NT_DIMS,
                                 preferred_element_type=F32) * d_ref[hl]
        o = jnp.dot(scores.astype(BF16), v, preferred_element_type=F32)
        kd = k * kd_ref[hl]
        gl = gl_ref[hh * RET_HEADS_PER_STEP + hl]
        cross = jnp.zeros((R_CHUNK, R_DIM), F32)
        for s in range(nseq):
            mine = row_seq == s
            state = s_ref[s, hl]
            cs = jnp.dot(qb, state.astype(BF16), preferred_element_type=F32)
            cross = jnp.where(mine, cs, cross)
            ks = jnp.where(mine, kd, 0.0).astype(BF16)
            so_ref[s, hl] = gl * state + lax.dot_general(
                ks, v, TN_DIMS, preferred_element_type=F32)
        o = o + cross * qd_ref[hl]
        o_ref[:, sl] = _group_norm_gate(o, g_ref[:, sl].astype(F32)).astype(o_ref.dtype)


def _ret_sample(z, state, layer, s_all, cos, sin, tabs, *, row0, col0, dec_batch, dec_seq,
                name):
    d, qd, kd, g_len = tabs
    nseq = R_CHUNK // dec_seq
    steps = dec_batch // nseq
    nh = R_HEADS // RET_HEADS_PER_STEP
    width = R_HEADS * R_DIM
    qb, kb, vb, gb = [(col0 + t * width) // RET_BLOCK for t in range(4)]
    r0 = row0 // R_CHUNK
    hp = RET_HEADS_PER_STEP

    def zspec(blk):
        return pl.BlockSpec((R_CHUNK, RET_BLOCK), lambda s, h: (r0 + s, blk + h))

    def tspec():
        return pl.BlockSpec((hp, R_CHUNK, R_DIM), lambda s, h: (h, 0, 0))

    sspec = pl.BlockSpec((None, nseq, hp, R_DIM, R_DIM), lambda s, h: (layer, s, h, 0, 0))
    args = [g_len, z, z, z, z, cos, sin, d, qd, kd, state]
    aliases = {}
    if s_all is not None:
        aliases[len(args)] = 1
        args.append(s_all)
    return pl.pallas_call(
        functools.partial(_ret_sample_body, dec_seq=dec_seq),
        grid=(steps, nh),
        in_specs=[
            pl.BlockSpec(memory_space=pltpu.SMEM),
            zspec(qb), zspec(kb), zspec(vb), zspec(gb),
            pl.BlockSpec((R_CHUNK, R_DIM), lambda s, h: (0, 0)),
            pl.BlockSpec((R_CHUNK, R_DIM), lambda s, h: (0, 0)),
            tspec(), tspec(), tspec(),
            sspec,
        ] + [pl.BlockSpec(memory_space=pl.ANY)] * (len(args) - 11),
        out_specs=[
            pl.BlockSpec((R_CHUNK, RET_BLOCK), lambda s, h: (s, h)),
            sspec,
        ],
        out_shape=[jax.ShapeDtypeStruct((dec_batch * dec_seq, width), BF16),
                   jax.ShapeDtypeStruct(state.shape, F32)],
        input_output_aliases=aliases,
        compiler_params=_params(("arbitrary", "arbitrary"),
                                4 * nseq * hp * R_DIM * R_DIM * 4 + (16 << 20)),
        name=name,
    )(*args)


def _gelu_tanh(x):
    c = (2.0 / jnp.pi) ** 0.5
    t = jnp.tanh(x * (c + (c * 0.044715) * (x * x)))
    y = 0.5 * x
    return y + y * t


def _ffn_up_body(h_ref, wu_ref, wg_ref, cw_ref, cb_ref, e0_ref, e1_ref,
                 f_ref, tail_ref, us_ref, wubf_ref, wgbf_ref, carry_ref, ug_ref,
                 *, tm, n_tiles, seq, batch, ms, dec_seq):
    i = pl.program_id(1)
    tn = f_ref.shape[1]
    mp = batch * seq

    @pl.when(i == 0)
    def _():
        wubf_ref[...] = wu_ref[...].astype(BF16)
        wgbf_ref[...] = wg_ref[...].astype(BF16)
        carry_ref[...] = jnp.zeros_like(carry_ref)

    last = n_tiles - 1
    s0 = mp - last * tm
    assert 0 <= s0 and s0 + ms == tm and s0 % FIX_ROWS == 0
    starts = [k * seq for k in range(1, batch)]
    for r in starts:
        assert r % tm != 0 and r % FIX_ROWS == 0, "mid-tile, packed-tile aligned starts only"

    def conv_gelu(u, p1, p2, w, bias):
        conv = bias + w[0:1, :] * p2
        conv = conv + w[1:2, :] * p1
        conv = conv + w[2:3, :] * u
        return _gelu_tanh(conv)

    def conv_gate(u, gate, p1, p2, w, bias):
        return (conv_gelu(u, p1, p2, w, bias) * gate).astype(f_ref.dtype)

    cuts = [0] + [tm * a // FFN_ROW_SPLIT[-1] for a in FFN_ROW_SPLIT]
    assert cuts[-1] == tm and all(r % FIX_ROWS == 0 for r in cuts)
    pieces = [(c, part) for c in range(tn // MXU_COLS) for part in range(len(cuts) - 1)]

    for c, part in pieces:
        cs = slice(c * MXU_COLS, (c + 1) * MXU_COLS)
        rs = slice(cuts[part], cuts[part + 1])
        ug_ref[2 * c, rs, :] = jnp.dot(h_ref[rs, :], wubf_ref[:, cs],
                                       preferred_element_type=F32)

    for c, part in pieces:
        cs = slice(c * MXU_COLS, (c + 1) * MXU_COLS)
        r0 = cuts[part]
        rs = slice(r0, cuts[part + 1])
        w = cw_ref[:, cs]
        bias = cb_ref[:, cs]
        u = ug_ref[2 * c, rs, :]
        if part == 0:
            before = carry_ref[:, cs]
        else:
            before = ug_ref[2 * c, r0 - SUBLANES:r0, :]
        last1 = before[SUBLANES - 1:SUBLANES, :]
        last2 = before[SUBLANES - 2:SUBLANES - 1, :]
        act = conv_gelu(u, pltpu.roll(u, 1, 0), pltpu.roll(u, 2, 0), w, bias)
        uh = u[:FIX_ROWS, :]
        pos = lax.broadcasted_iota(jnp.int32, uh.shape, 0)
        prev1 = jnp.where(pos == 0, last1, pltpu.roll(uh, 1, 0))
        prev2 = jnp.where(pos == 0, last2,
                          jnp.where(pos == 1, last1, pltpu.roll(uh, 2, 0)))
        act_head = conv_gelu(uh, prev1, prev2, w, bias)

        gate = jnp.dot(h_ref[rs, :], wgbf_ref[:, cs], preferred_element_type=F32)
        ug_ref[2 * c + 1, rs, :] = gate
        f_ref[rs, cs] = (act * gate).astype(f_ref.dtype)
        f_ref[r0:r0 + FIX_ROWS, cs] = (act_head * gate[:FIX_ROWS, :]).astype(f_ref.dtype)

    for c in range(tn // MXU_COLS):
        cs = slice(c * MXU_COLS, (c + 1) * MXU_COLS)
        carry_ref[:, cs] = ug_ref[2 * c, tm - SUBLANES:, :]

    def rewrite(rows, p1_fn, p2_fn):
        for c in range(tn // MXU_COLS):
            cs = slice(c * MXU_COLS, (c + 1) * MXU_COLS)
            us = ug_ref[2 * c, rows, :]
            f_ref[rows, cs] = conv_gate(us, ug_ref[2 * c + 1, rows, :], p1_fn(us, cs),
                                        p2_fn(us, cs), cw_ref[:, cs], cb_ref[:, cs])

    for r in starts:
        @pl.when(i == r // tm)
        def _(off=r % tm):
            pos = lax.broadcasted_iota(jnp.int32, (FIX_ROWS, MXU_COLS), 0)
            rewrite(slice(off, off + FIX_ROWS),
                    lambda us, cs: jnp.where(pos == 0, 0.0, pltpu.roll(us, 1, 0)),
                    lambda us, cs: jnp.where(pos <= 1, 0.0, pltpu.roll(us, 2, 0)))

    @pl.when(i == last)
    def _():
        pos = lax.broadcasted_iota(jnp.int32, (ms, MXU_COLS), 0) & (dec_seq - 1)
        rewrite(slice(s0, tm),
                lambda us, cs: jnp.where(pos == 0, e1_ref[:, cs], pltpu.roll(us, 1, 0)),
                lambda us, cs: jnp.where(
                    pos == 0, e0_ref[:, cs],
                    jnp.where(pos == 1, e1_ref[:, cs], pltpu.roll(us, 2, 0))))
        for c in range(tn // MXU_COLS):
            us_ref[:, c * MXU_COLS:(c + 1) * MXU_COLS] = ug_ref[2 * c, s0:, :]

    for b in range(batch):
        r = (b + 1) * seq - SUBLANES

        @pl.when(i == r // tm)
        def _(b=b, off=r % tm):
            for c in range(tn // MXU_COLS):
                tail_ref[b * SUBLANES:(b + 1) * SUBLANES, c * MXU_COLS:(c + 1) * MXU_COLS] = (
                    ug_ref[2 * c, off:off + SUBLANES, :])


def _ffn_up(h, w_up, conv_w, conv_b, layer, e, *, tm, tn, seq, batch, dec_seq, name):
    m, k = h.shape
    d_ff = w_up.shape[2] // 2
    ms = e.shape[2]
    nj = d_ff // tn
    nt = m // tm
    body = functools.partial(_ffn_up_body, tm=tm, n_tiles=nt, seq=seq, batch=batch,
                             ms=ms, dec_seq=dec_seq)
    nbytes = (2 * tm * k * 2 + 4 * k * tn * 4 + 2 * k * tn * 2
              + 16 * tm * tn * 4 + 4 * ms * tn * 4 + (4 << 20))
    return pl.pallas_call(
        body,
        grid=(nj, nt),
        in_specs=[
            pl.BlockSpec((tm, k), lambda j, i: (i, 0)),
            pl.BlockSpec((None, k, tn), lambda j, i: (layer, 0, j)),
            pl.BlockSpec((None, k, tn), lambda j, i: (layer, 0, nj + j)),
            pl.BlockSpec((None, CONV_W, tn), lambda j, i: (layer, 0, j)),
            pl.BlockSpec((None, 1, tn), lambda j, i: (layer, 0, j)),
            pl.BlockSpec((None, None, ms, tn), lambda j, i: (layer, 0, 0, j)),
            pl.BlockSpec((None, None, ms, tn), lambda j, i: (layer, 1, 0, j)),
        ],
        out_specs=[
            pl.BlockSpec((tm, tn), lambda j, i: (i, j)),
            pl.BlockSpec((batch * SUBLANES, tn), lambda j, i: (0, j)),
            pl.BlockSpec((ms, tn), lambda j, i: (0, j)),
        ],
        out_shape=[jax.ShapeDtypeStruct((m, d_ff), BF16),
                   jax.ShapeDtypeStruct((batch * SUBLANES, d_ff), F32),
                   jax.ShapeDtypeStruct((ms, d_ff), F32)],
        scratch_shapes=[pltpu.VMEM((k, tn), BF16), pltpu.VMEM((k, tn), BF16),
                        pltpu.VMEM((SUBLANES, tn), F32),
                        pltpu.VMEM((2 * tn // MXU_COLS, tm, MXU_COLS), F32)],
        compiler_params=_params(("arbitrary", "arbitrary"), nbytes),
        name=name,
    )(h, w_up, w_up, conv_w, conv_b, e, e)


TM = 1408
TM_NORM = 768
TM_DOWN = 704
TM_IN = 2816
TN_IN = 512
TN_UP = 512
GATE_BLOCK = 512
TM_OUT = 352
FFN_ROW_SPLIT = (2, 4)
TN_OUT = 512


def kernel(x_prompt, x_sample, cache_win_k, cache_win_v, state_ret, state_conv,
           g_mix, w_in, sinks, w_proj_a, w_proj_b, w_o, g_ffn, w_up, conv_w,
           conv_b, w_down, g_final):
    batch, seq, d_model = x_prompt.shape
    dec_batch, dec_seq, _ = x_sample.shape
    depth = w_in.shape[0]
    d_ff = w_down.shape[1]
    mp = batch * seq
    ms = dec_batch * dec_seq
    a_width = A_HEADS * A_HEAD_DIM
    kv_width = A_KV_HEADS * A_HEAD_DIM
    r_width = R_HEADS * R_DIM
    ret_col0 = a_width + 2 * kv_width
    ga_col = ret_col0 + 4 * r_width
    gb_col = ga_col + d_model
    assert dec_seq & (dec_seq - 1) == 0 and R_CHUNK % dec_seq == 0

    pos_p = jnp.arange(seq, dtype=F32)
    pos_s = PAST_LEN + jnp.arange(dec_seq, dtype=F32)
    rope_p = _rope_tables(pos_p)
    rope_s = tuple(jnp.tile(t, (SWA_SAMPLE_SEQS, 1)) for t in _rope_tables(pos_s))
    rrot_p = _retrot_tables(pos_p)
    nseq = R_CHUNK // dec_seq
    rrot_s = tuple(jnp.tile(t, (nseq, 1)) for t in _retrot_tables(pos_s))
    tabs_p = _decay_tables(R_CHUNK, 1)
    tabs_s = _decay_tables(dec_seq, nseq)
    g_mix3 = g_mix.reshape(depth, 1, d_model)
    g_ffn3 = g_ffn.reshape(depth, 1, d_model)
    conv_b3 = conv_b.reshape(depth, 1, d_ff)

    e_conv = jnp.repeat(jnp.swapaxes(state_conv, 1, 2), dec_seq, axis=2)
    v_col0 = a_width + kv_width

    kp_l, vp_l, sp_l, cp_l = [], [], [], []
    ks_l, vs_l, cs_l = [], [], []
    s_all = None
    for l in range(depth):
        if l == 0:
            x, h = _join_norm(x_prompt.reshape(mp, d_model), x_sample.reshape(ms, d_model),
                              g_mix3, l, name="join_norm_mix0")
        else:
            h = _rmsnorm(x, g_mix3, l, BF16, tm=TM_NORM, name=f"norm_mix{l}")
        z = _matmul(h, w_in, l, tm=TM_IN, tn=TN_IN, out_dtype=BF16, name=f"proj_in{l}")

        oa_p, kr_p = _swa_prompt(z, sinks, l, *rope_p, batch=batch, seq=seq,
                                 name=f"swa_prompt{l}")
        oa_s, kr_s = _swa_sample(z, cache_win_k, cache_win_v, sinks, l, *rope_s,
                                 row0=mp, dec_batch=dec_batch, dec_seq=dec_seq,
                                 name=f"swa_sample{l}")
        ob_p, s_p = _ret_prompt(z, *rrot_p, tabs_p, col0=ret_col0, batch=batch,
                                seq=seq, name=f"ret_prompt{l}")
        ob_s, s_all = _ret_sample(z, state_ret, l, s_all, *rrot_s, tabs_s, row0=mp,
                                  col0=ret_col0, dec_batch=dec_batch, dec_seq=dec_seq,
                                  name=f"ret_sample{l}")
        merged = _merge(oa_p, oa_s, ob_p, ob_s, z, w_proj_a, w_proj_b, l, ga_col=ga_col,
                        gb_col=gb_col, tg=GATE_BLOCK, name=f"merge{l}")
        x, h2 = _matmul_res_norm(merged, w_o, l, x, g_ffn3, tm=TM_OUT,
                                 name=f"proj_out_norm{l}")
        f, u_tail, u_s = _ffn_up(h2, w_up, conv_w, conv_b3, l, e_conv, tm=TM, tn=TN_UP,
                                 seq=seq, batch=batch, dec_seq=dec_seq, name=f"ffn_up{l}")
        x = _matmul(f, w_down, l, tm=TM_DOWN, tn=TN_OUT, out_dtype=F32, residual=x,
                    name=f"proj_down{l}")

        v_p = jnp.stack([lax.slice(z, ((b + 1) * seq - WINDOW, v_col0), ((b + 1) * seq, ret_col0))
                         for b in range(batch)])
        v_s = lax.slice(z, (mp, v_col0), (mp + ms, ret_col0))
        kp_l.append(kr_p.reshape(batch, WINDOW, A_KV_HEADS, A_HEAD_DIM))
        vp_l.append(v_p.astype(F32).reshape(batch, WINDOW, A_KV_HEADS, A_HEAD_DIM))
        sp_l.append(s_p)
        cp_l.append(u_tail.reshape(batch, SUBLANES, d_ff)[:, -(CONV_W - 1):])
        ks_l.append(kr_s.reshape(dec_batch, dec_seq, A_KV_HEADS, A_HEAD_DIM))
        vs_l.append(v_s.astype(F32).reshape(dec_batch, dec_seq, A_KV_HEADS, A_HEAD_DIM))
        cs_l.append(u_s.reshape(dec_batch, dec_seq, d_ff)[:, -(CONV_W - 1):])

    y_prompt, y_sample = _split_norm(x, g_final, mp=mp, name="norm_final")
    y_prompt = y_prompt.reshape(batch, seq, d_model)
    y_sample = y_sample.reshape(dec_batch, dec_seq, d_model)
    return (y_prompt, y_sample,
            jnp.stack(kp_l), jnp.stack(vp_l), jnp.stack(sp_l), jnp.stack(cp_l),
            jnp.stack(ks_l), jnp.stack(vs_l), s_all, jnp.stack(cs_l))
```

```python
import functools

import jax
import jax.numpy as jnp
import numpy as np
from jax import lax
from jax.experimental import pallas as pl
from jax.experimental.pallas import tpu as pltpu

F32 = jnp.float32
BF16 = jnp.bfloat16

LANES = 128
SUBLANES = 8
MXU_COLS = 256
FIX_ROWS = 2 * SUBLANES
VMEM_LIMIT_CAP = 56 * 1024 * 1024

WINDOW = 128
A_HEADS = 16
A_KV_HEADS = 4
A_HEAD_DIM = 64
A_GROUP = A_HEADS // A_KV_HEADS
R_HEADS = 8
R_DIM = 128
R_CHUNK = 128
CONV_W = 3
EPS = 1e-6
ROPE_THETA = 10000.0
PAST_LEN = 16384

NT_DIMS = (((1,), (1,)), ((), ()))
TN_DIMS = (((0,), (0,)), ((), ()))


def _vmem_limit(nbytes):
    return int(min(VMEM_LIMIT_CAP, max(32 * 1024 * 1024, nbytes)))


def _params(semantics, nbytes):
    return pltpu.CompilerParams(dimension_semantics=semantics,
                                vmem_limit_bytes=_vmem_limit(nbytes))


def _rmsnorm_body(x_ref, g_ref, o_ref):
    x = x_ref[...]
    ms = jnp.mean(x * x, axis=-1, keepdims=True)
    o_ref[...] = (x * lax.rsqrt(ms + EPS) * g_ref[...]).astype(o_ref.dtype)


def _rmsnorm(x, g, layer, out_dtype, *, tm, name):
    m, d = x.shape
    return pl.pallas_call(
        _rmsnorm_body,
        grid=(m // tm,),
        in_specs=[pl.BlockSpec((tm, d), lambda i: (i, 0)),
                  pl.BlockSpec((None, 1, d), lambda i: (layer, 0, 0))],
        out_specs=pl.BlockSpec((tm, d), lambda i: (i, 0)),
        out_shape=jax.ShapeDtypeStruct((m, d), out_dtype),
        compiler_params=_params(("arbitrary",), 6 * tm * d * 4),
        name=name,
    )(x, g)


def _join_norm_body(xp_ref, xs_ref, g_ref, x_ref, h_ref, *, prompt_tiles):
    i = pl.program_id(0)

    def emit(x):
        ms = jnp.mean(x * x, axis=-1, keepdims=True)
        x_ref[...] = x
        h_ref[...] = (x * lax.rsqrt(ms + EPS) * g_ref[...]).astype(h_ref.dtype)

    @pl.when(i < prompt_tiles)
    def _():
        emit(xp_ref[...])

    @pl.when(i >= prompt_tiles)
    def _():
        emit(xs_ref[...])


def _join_norm(xp, xs, g, layer, *, name):
    mp, d = xp.shape
    ms = xs.shape[0]
    assert mp % ms == 0
    pt = mp // ms
    return pl.pallas_call(
        functools.partial(_join_norm_body, prompt_tiles=pt),
        grid=(pt + 1,),
        in_specs=[pl.BlockSpec((ms, d), lambda i: (jnp.minimum(i, pt - 1), 0)),
                  pl.BlockSpec((ms, d), lambda i: (0, 0)),
                  pl.BlockSpec((None, 1, d), lambda i: (layer, 0, 0))],
        out_specs=[pl.BlockSpec((ms, d), lambda i: (i, 0)),
                   pl.BlockSpec((ms, d), lambda i: (i, 0))],
        out_shape=[jax.ShapeDtypeStruct((mp + ms, d), F32),
                   jax.ShapeDtypeStruct((mp + ms, d), BF16)],
        compiler_params=_params(("arbitrary",), 0),
        name=name,
    )(xp, xs, g)


def _split_norm_body(x_ref, g_ref, yp_ref, ys_ref, *, prompt_tiles):
    i = pl.program_id(0)
    x = x_ref[...]
    ms = jnp.mean(x * x, axis=-1, keepdims=True)
    y = x * lax.rsqrt(ms + EPS) * g_ref[...]

    @pl.when(i < prompt_tiles)
    def _():
        yp_ref[...] = y

    @pl.when(i >= prompt_tiles)
    def _():
        ys_ref[...] = y


def _split_norm(x, g, *, mp, name):
    m, d = x.shape
    ms = m - mp
    assert mp % ms == 0
    pt = mp // ms
    return pl.pallas_call(
        functools.partial(_split_norm_body, prompt_tiles=pt),
        grid=(pt + 1,),
        in_specs=[pl.BlockSpec((ms, d), lambda i: (i, 0)),
                  pl.BlockSpec((1, d), lambda i: (0, 0))],
        out_specs=[pl.BlockSpec((ms, d), lambda i: (jnp.minimum(i, pt - 1), 0)),
                   pl.BlockSpec((ms, d), lambda i: (0, 0))],
        out_shape=[jax.ShapeDtypeStruct((mp, d), F32),
                   jax.ShapeDtypeStruct((ms, d), F32)],
        compiler_params=_params(("arbitrary",), 0),
        name=name,
    )(x, g.reshape(1, d))


def _mm_body(x_ref, w_ref, o_ref, wbf_ref):
    @pl.when(pl.program_id(1) == 0)
    def _():
        wbf_ref[...] = w_ref[...].astype(BF16)

    o_ref[...] = jnp.dot(x_ref[...], wbf_ref[...],
                         preferred_element_type=F32).astype(o_ref.dtype)


def _mm_res_body(x_ref, w_ref, r_ref, o_ref, wbf_ref):
    @pl.when(pl.program_id(1) == 0)
    def _():
        wbf_ref[...] = w_ref[...].astype(BF16)

    o_ref[...] = r_ref[...] + jnp.dot(x_ref[...], wbf_ref[...],
                                      preferred_element_type=F32)


def _matmul(x, w, layer, *, tm, tn, out_dtype, residual=None, name):
    m, k = x.shape
    n = w.shape[2]
    grid = (n // tn, m // tm)
    in_specs = [pl.BlockSpec((tm, k), lambda j, i: (i, 0)),
                pl.BlockSpec((None, k, tn), lambda j, i: (layer, 0, j))]
    args = [x, w]
    body = _mm_body
    if residual is not None:
        in_specs.append(pl.BlockSpec((tm, tn), lambda j, i: (i, j)))
        args.append(residual)
        body = _mm_res_body
    nbytes = (2 * tm * k * 2 + 2 * k * tn * 4 + k * tn * 2
              + 6 * tm * tn * 4 + (4 << 20))
    return pl.pallas_call(
        body,
        grid=grid,
        in_specs=in_specs,
        out_specs=pl.BlockSpec((tm, tn), lambda j, i: (i, j)),
        out_shape=jax.ShapeDtypeStruct((m, n), out_dtype),
        scratch_shapes=[pltpu.VMEM((k, tn), BF16)],
        compiler_params=_params(("arbitrary", "arbitrary"), nbytes),
        name=name,
    )(*args)


def _out_norm_body(m_ref, w_ref, r_ref, g_ref, x_ref, h_ref, wbf_ref):
    @pl.when(pl.program_id(0) == 0)
    def _():
        wbf_ref[...] = w_ref[...].astype(BF16)

    x = r_ref[...] + jnp.dot(m_ref[...], wbf_ref[...], preferred_element_type=F32)
    x_ref[...] = x
    ms = jnp.mean(x * x, axis=-1, keepdims=True)
    h_ref[...] = (x * lax.rsqrt(ms + EPS) * g_ref[...]).astype(h_ref.dtype)


def _matmul_res_norm(a, w, layer, residual, g, *, tm, name):
    m, k = a.shape
    n = w.shape[2]
    nbytes = (k * n * 4 + k * n * 2 + 2 * tm * k * 2 + 4 * tm * n * 4
              + 2 * tm * n * 2 + 2 * tm * n * 4 + (4 << 20))
    return pl.pallas_call(
        _out_norm_body,
        grid=(m // tm,),
        in_specs=[pl.BlockSpec((tm, k), lambda i: (i, 0)),
                  pl.BlockSpec((None, k, n), lambda i: (layer, 0, 0),
                               pipeline_mode=pl.Buffered(1)),
                  pl.BlockSpec((tm, n), lambda i: (i, 0)),
                  pl.BlockSpec((None, 1, n), lambda i: (layer, 0, 0))],
        out_specs=[pl.BlockSpec((tm, n), lambda i: (i, 0)),
                   pl.BlockSpec((tm, n), lambda i: (i, 0))],
        out_shape=[jax.ShapeDtypeStruct((m, n), F32),
                   jax.ShapeDtypeStruct((m, n), BF16)],
        scratch_shapes=[pltpu.VMEM((k, n), BF16)],
        compiler_params=_params(("arbitrary",), nbytes),
        name=name,
    )(a, w, residual, g)


def _merge_body(*refs, n_gate, tg, prompt_tiles):
    oap_ref, oas_ref, obp_ref, obs_ref = refs[:4]
    ga_refs = refs[4:4 + n_gate]
    gb_refs = refs[4 + n_gate:4 + 2 * n_gate]
    wa_ref, wb_ref, o_ref, wabf_ref, wbbf_ref = refs[4 + 2 * n_gate:]
    i = pl.program_id(0)

    @pl.when(i == 0)
    def _():
        wabf_ref[...] = wa_ref[...].astype(BF16)
        wbbf_ref[...] = wb_ref[...].astype(BF16)

    is_prompt = i < prompt_tiles
    oa = jnp.where(is_prompt, oap_ref[...], oas_ref[...])
    ob = jnp.where(is_prompt, obp_ref[...], obs_ref[...])
    for c in range(n_gate):
        cs = slice(c * tg, (c + 1) * tg)
        a = jnp.dot(oa, wabf_ref[:, cs], preferred_element_type=F32)
        b = jnp.dot(ob, wbbf_ref[:, cs], preferred_element_type=F32)
        ga = jax.nn.sigmoid(ga_refs[c][...].astype(F32))
        gb = jax.nn.sigmoid(gb_refs[c][...].astype(F32))
        o_ref[:, cs] = (ga * a + gb * b).astype(o_ref.dtype)


def _merge(oa_p, oa_s, ob_p, ob_s, z, wa, wb, layer, *, ga_col, gb_col, tg, name):
    mp, k = oa_p.shape
    tm = oa_s.shape[0]
    assert mp % tm == 0
    pt = mp // tm
    m = mp + tm
    n = wa.shape[2]
    assert ga_col % tg == 0 and gb_col % tg == 0 and n % tg == 0
    n_gate = n // tg
    nbytes = (2 * k * n * 4 + 2 * k * n * 2 + 4 * tm * k * 2 + 4 * tm * n * 2
              + 2 * tm * n * 2 + 6 * tm * tg * 4 + (4 << 20))

    def gate_spec(blk):
        return pl.BlockSpec((tm, tg), lambda i: (i, blk))

    def gate_specs(col):
        return [gate_spec(col // tg + c) for c in range(n_gate)]

    def wspec():
        return pl.BlockSpec((None, k, n), lambda i: (layer, 0, 0), pipeline_mode=pl.Buffered(1))

    def prompt_spec():
        return pl.BlockSpec((tm, k), lambda i: (jnp.minimum(i, pt - 1), 0))

    def sample_spec():
        return pl.BlockSpec((tm, k), lambda i: (0, 0))

    return pl.pallas_call(
        functools.partial(_merge_body, n_gate=n_gate, tg=tg, prompt_tiles=pt),
        grid=(m // tm,),
        in_specs=[prompt_spec(), sample_spec(), prompt_spec(), sample_spec()]
        + gate_specs(ga_col) + gate_specs(gb_col) + [wspec(), wspec()],
        out_specs=pl.BlockSpec((tm, n), lambda i: (i, 0)),
        out_shape=jax.ShapeDtypeStruct((m, n), BF16),
        scratch_shapes=[pltpu.VMEM((k, n), BF16), pltpu.VMEM((k, n), BF16)],
        compiler_params=_params(("arbitrary",), nbytes),
        name=name,
    )(oa_p, oa_s, ob_p, ob_s, *([z] * (2 * n_gate)), wa, wb)


def _rope_tables(pos):
    half = A_HEAD_DIM // 2
    inv = ROPE_THETA ** (-2.0 * jnp.arange(half, dtype=F32) / A_HEAD_DIM)
    ang = pos[:, None] * inv[None, :]
    c, s = jnp.cos(ang), jnp.sin(ang)
    cos = jnp.concatenate([c, c, c, c], axis=-1)
    sin = jnp.concatenate([-s, s, -s, s], axis=-1)
    return cos, sin


def _retrot_tables(pos):
    half = R_DIM // 2
    inv = 1.0 / (10000.0 ** jnp.linspace(0.0, 1.0, half, dtype=F32))
    ang = pos[:, None] * inv[None, :]
    c, s = jnp.cos(ang), jnp.sin(ang)
    cos = jnp.stack([c, c], axis=-1).reshape(pos.shape[0], R_DIM)
    sin = jnp.stack([-s, s], axis=-1).reshape(pos.shape[0], R_DIM)
    return cos, sin


def _rope128(x, cos, sin):
    lane = lax.broadcasted_iota(jnp.int32, x.shape, 1)
    first = (lane & (A_HEAD_DIM - 1)) < (A_HEAD_DIM // 2)
    partner = jnp.where(first, pltpu.roll(x, LANES - 32, 1), pltpu.roll(x, 32, 1))
    return x * cos + partner * sin


def _retrot128(x, cos, sin):
    lane = lax.broadcasted_iota(jnp.int32, x.shape, 1)
    even = (lane & 1) == 0
    partner = jnp.where(even, pltpu.roll(x, LANES - 1, 1), pltpu.roll(x, 1, 1))
    return x * cos + partner * sin


def _swa_heads(q_chunks, score_fn, out_fn, sink_ref, layer):
    rows = q_chunks[0].shape[0]
    lane = lax.broadcasted_iota(jnp.int32, (rows, LANES), 1)
    low = lane < A_HEAD_DIM
    outs = []
    for pair in range(A_HEADS // 2):
        halves = []
        for hpos in range(2):
            h = 2 * pair + hpos
            kv = h // A_GROUP
            c, kpos = kv // 2, kv % 2
            qc = q_chunks[pair]
            if hpos != kpos:
                qc = pltpu.roll(qc, A_HEAD_DIM, 1)
            keep = low if kpos == 0 else jnp.logical_not(low)
            qm = jnp.where(keep, qc, 0.0).astype(BF16)
            s = score_fn(qm, c)
            sink = sink_ref[layer, h]
            m = jnp.maximum(jnp.max(s, axis=-1, keepdims=True), sink)
            p = jnp.exp(s - m)
            pv, psum = out_fn(p, c)
            o = pv / (psum + jnp.exp(sink - m))
            if hpos != kpos:
                o = pltpu.roll(o, A_HEAD_DIM, 1)
            halves.append(o)
        outs.append(jnp.where(low, halves[0], halves[1]))
    return outs


def _swa_prompt_body(sink_ref, q_ref, k_ref, vc_ref, vp_ref, cos_ref, sin_ref,
                     o_ref, kr_ref, kprev_ref, *, layer):
    n = pl.program_id(1)
    cos = cos_ref[...]
    sin = sin_ref[...]

    @pl.when(n == 0)
    def _():
        kprev_ref[...] = jnp.zeros_like(kprev_ref)

    k_cur, k_both, v_both = [], [], []
    for c in range(2):
        sl = slice(c * LANES, (c + 1) * LANES)
        kc = _rope128(k_ref[:, sl].astype(F32), cos, sin)
        kr_ref[:, sl] = kc
        k_cur.append(kc.astype(BF16))
        k_both.append(jnp.concatenate([kprev_ref[:, sl], k_cur[c]], axis=0))
        v_both.append(jnp.concatenate([vp_ref[:, sl], vc_ref[:, sl]], axis=0))

    scale = A_HEAD_DIM ** -0.5
    q_chunks = [
        _rope128(q_ref[:, c * LANES:(c + 1) * LANES].astype(F32), cos, sin) * scale
        for c in range(A_HEADS // 2)]

    qi = lax.broadcasted_iota(jnp.int32, (WINDOW, WINDOW), 0)
    kj = lax.broadcasted_iota(jnp.int32, (WINDOW, WINDOW), 1)
    from_cur = kj <= qi
    prev_bias = jnp.where(n > 0, 0.0, -jnp.inf)

    def scores(qm, c):
        s = lax.dot_general(qm, k_both[c], NT_DIMS, preferred_element_type=F32)
        return jnp.where(from_cur, s[:, WINDOW:], s[:, :WINDOW] + prev_bias)

    def weighted_values(p, c):
        p_both = jnp.concatenate([jnp.where(from_cur, 0.0, p), jnp.where(from_cur, p, 0.0)],
                                 axis=1).astype(BF16)
        return (jnp.dot(p_both, v_both[c], preferred_element_type=F32),
                jnp.sum(p, axis=-1, keepdims=True))

    outs = _swa_heads(q_chunks, scores, weighted_values, sink_ref, layer)
    for c, o in enumerate(outs):
        o_ref[:, c * LANES:(c + 1) * LANES] = o.astype(o_ref.dtype)
    for c in range(2):
        kprev_ref[:, c * LANES:(c + 1) * LANES] = k_cur[c]


def _swa_prompt(z, sinks, layer, cos, sin, *, batch, seq, name):
    nb = seq // WINDOW
    a_width = A_HEADS * A_HEAD_DIM
    kv_width = A_KV_HEADS * A_HEAD_DIM
    k_blk = a_width // kv_width
    v_blk = k_blk + 1
    rows = batch * seq
    return pl.pallas_call(
        functools.partial(_swa_prompt_body, layer=layer),
        grid=(batch, nb),
        in_specs=[
            pl.BlockSpec(memory_space=pltpu.SMEM),
            pl.BlockSpec((WINDOW, a_width), lambda b, n: (b * nb + n, 0)),
            pl.BlockSpec((WINDOW, kv_width), lambda b, n: (b * nb + n, k_blk)),
            pl.BlockSpec((WINDOW, kv_width), lambda b, n: (b * nb + n, v_blk)),
            pl.BlockSpec((WINDOW, kv_width),
                         lambda b, n: (b * nb + jnp.maximum(n - 1, 0), v_blk)),
            pl.BlockSpec((WINDOW, LANES), lambda b, n: (n, 0)),
            pl.BlockSpec((WINDOW, LANES), lambda b, n: (n, 0)),
        ],
        out_specs=[
            pl.BlockSpec((WINDOW, a_width), lambda b, n: (b * nb + n, 0)),
            pl.BlockSpec((WINDOW, kv_width), lambda b, n: (b, 0)),
        ],
        out_shape=[jax.ShapeDtypeStruct((rows, a_width), BF16),
                   jax.ShapeDtypeStruct((batch * WINDOW, kv_width), F32)],
        scratch_shapes=[pltpu.VMEM((WINDOW, kv_width), BF16)],
        compiler_params=_params(("arbitrary", "arbitrary"), 0),
        name=name,
    )(sinks, z, z, z, z, cos, sin)


SWA_SAMPLE_SEQS = 8


def _swa_sample_body(sink_ref, q_ref, k_ref, v_ref, ck_ref, cv_ref, cos_ref,
                     sin_ref, o_ref, kr_ref, *, dec_seq, layer):
    cos = cos_ref[...]
    sin = sin_ref[...]
    g = SWA_SAMPLE_SEQS
    rows = g * dec_seq
    ncache = g * WINDOW
    k_chunks, v_chunks = [], []
    for c in range(2):
        sl = slice(c * LANES, (c + 1) * LANES)
        kc = _rope128(k_ref[:, sl].astype(F32), cos, sin)
        kr_ref[:, sl] = kc
        k_chunks.append(jnp.concatenate(
            [ck_ref[:, sl].astype(BF16), kc.astype(BF16)], axis=0))
        v_chunks.append(jnp.concatenate(
            [cv_ref[:, sl].astype(BF16), v_ref[:, sl]], axis=0))

    scale = A_HEAD_DIM ** -0.5
    q_chunks = [
        _rope128(q_ref[:, c * LANES:(c + 1) * LANES].astype(F32), cos, sin) * scale
        for c in range(A_HEADS // 2)]

    shape = (rows, ncache + rows)
    r = lax.broadcasted_iota(jnp.int32, shape, 0)
    col = lax.broadcasted_iota(jnp.int32, shape, 1)
    log_l = dec_seq.bit_length() - 1
    log_w = WINDOW.bit_length() - 1
    seq_q = r >> log_l
    i = r & (dec_seq - 1)
    in_cache = col < ncache
    cnew = col - ncache
    seq_k = jnp.where(in_cache, col >> log_w, cnew >> log_l)
    ok = ((in_cache & ((col & (WINDOW - 1)) > i))
          | (jnp.logical_not(in_cache) & ((cnew & (dec_seq - 1)) <= i)))
    valid = (seq_k == seq_q) & ok

    def scores(qm, c):
        s = lax.dot_general(qm, k_chunks[c], NT_DIMS, preferred_element_type=F32)
        return jnp.where(valid, s, -jnp.inf)

    def weighted_values(p, c):
        return (jnp.dot(p.astype(BF16), v_chunks[c], preferred_element_type=F32),
                jnp.sum(p, axis=-1, keepdims=True))

    outs = _swa_heads(q_chunks, scores, weighted_values, sink_ref, layer)
    for c, o in enumerate(outs):
        o_ref[:, c * LANES:(c + 1) * LANES] = o.astype(o_ref.dtype)


def _swa_sample(z, cache_k, cache_v, sinks, layer, cos, sin, *, row0, dec_batch, dec_seq, name):
    g = SWA_SAMPLE_SEQS
    rows = g * dec_seq
    a_width = A_HEADS * A_HEAD_DIM
    kv_width = A_KV_HEADS * A_HEAD_DIM
    k_blk = a_width // kv_width
    v_blk = k_blk + 1
    r0 = row0 // rows
    depth = cache_k.shape[0]
    ck = cache_k.reshape(depth, dec_batch * WINDOW, kv_width)
    cv = cache_v.reshape(depth, dec_batch * WINDOW, kv_width)
    return pl.pallas_call(
        functools.partial(_swa_sample_body, dec_seq=dec_seq, layer=layer),
        grid=(dec_batch // g,),
        in_specs=[
            pl.BlockSpec(memory_space=pltpu.SMEM),
            pl.BlockSpec((rows, a_width), lambda s: (r0 + s, 0)),
            pl.BlockSpec((rows, kv_width), lambda s: (r0 + s, k_blk)),
            pl.BlockSpec((rows, kv_width), lambda s: (r0 + s, v_blk)),
            pl.BlockSpec((None, g * WINDOW, kv_width), lambda s: (layer, s, 0)),
            pl.BlockSpec((None, g * WINDOW, kv_width), lambda s: (layer, s, 0)),
            pl.BlockSpec((rows, LANES), lambda s: (0, 0)),
            pl.BlockSpec((rows, LANES), lambda s: (0, 0)),
        ],
        out_specs=[
            pl.BlockSpec((rows, a_width), lambda s: (s, 0)),
            pl.BlockSpec((rows, kv_width), lambda s: (s, 0)),
        ],
        out_shape=[jax.ShapeDtypeStruct((dec_batch * dec_seq, a_width), BF16),
                   jax.ShapeDtypeStruct((dec_batch * dec_seq, kv_width), F32)],
        compiler_params=_params(("arbitrary",), 0),
        name=name,
    )(sinks, z, z, z, ck, cv, cos, sin)


RET_HEADS_PER_STEP = 4
RET_BLOCK = RET_HEADS_PER_STEP * R_DIM
RET_CHUNKS_PER_STEP = 4


def _decay_tables(length, nseq):
    log_g = np.log1p(-np.exp2(-5.0 - np.arange(R_HEADS, dtype=np.float64)))
    i = np.arange(length, dtype=np.float64)
    diff = i[:, None] - i[None, :]
    d = np.where(diff >= 0, np.exp(log_g[:, None, None] * np.maximum(diff, 0.0)), 0.0)
    d = np.einsum('st,hij->hsitj', np.eye(nseq), d).reshape(
        R_HEADS, nseq * length, nseq * length)
    q_dec = np.exp(log_g[:, None] * (i[None, :] + 1.0))
    k_dec = np.exp(log_g[:, None] * (length - 1.0 - i)[None, :])
    shape = (R_HEADS, nseq * length, R_DIM)
    qd = np.broadcast_to(np.tile(q_dec, (1, nseq))[:, :, None], shape)
    kd = np.broadcast_to(np.tile(k_dec, (1, nseq))[:, :, None], shape)
    g_len = np.exp(log_g * length)
    return tuple(jnp.asarray(t, F32) for t in (d, qd, kd, g_len))


def _group_norm_gate(o, gate):
    o = o * lax.rsqrt(jnp.mean(o * o, axis=-1, keepdims=True) + EPS)
    return o * (gate * jax.nn.sigmoid(gate))


def _ret_prompt_body(gl_ref, *refs):
    nb = R_HEADS // RET_HEADS_PER_STEP
    q_refs, k_refs, v_refs, g_refs = (refs[t * nb:(t + 1) * nb] for t in range(4))
    cos_ref, sin_ref, d_ref, qd_ref, kd_ref, o_ref, s_ref = refs[4 * nb:]
    c = pl.program_id(1)

    @pl.when(c == 0)
    def _():
        s_ref[...] = jnp.zeros_like(s_ref)

    scale = R_DIM ** -0.5
    for ck in range(RET_CHUNKS_PER_STEP):
        rs = slice(ck * R_CHUNK, (ck + 1) * R_CHUNK)
        cos = cos_ref[rs, :]
        sin = sin_ref[rs, :]
        for h in range(R_HEADS):
            blk = h // RET_HEADS_PER_STEP
            sl = slice((h % RET_HEADS_PER_STEP) * R_DIM, (h % RET_HEADS_PER_STEP + 1) * R_DIM)
            q = _retrot128(q_refs[blk][rs, sl].astype(F32), cos, sin)
            k = _retrot128(k_refs[blk][rs, sl].astype(F32), cos, sin) * scale
            v = v_refs[blk][rs, sl]
            qb = q.astype(BF16)
            state = s_ref[0, h]
            scores = lax.dot_general(qb, k.astype(BF16), NT_DIMS,
                                     preferred_element_type=F32) * d_ref[h]
            o = jnp.dot(scores.astype(BF16), v, preferred_element_type=F32)
            cross = jnp.dot(qb, state.astype(BF16), preferred_element_type=F32)
            o = o + cross * qd_ref[h]
            kd = (k * kd_ref[h]).astype(BF16)
            s_ref[0, h] = gl_ref[h] * state + lax.dot_general(
                kd, v, TN_DIMS, preferred_element_type=F32)
            o_ref[rs, h * R_DIM:(h + 1) * R_DIM] = _group_norm_gate(
                o, g_refs[blk][rs, sl].astype(F32)).astype(o_ref.dtype)


def _ret_prompt(z, cos, sin, tabs, *, col0, batch, seq, name):
    d, qd, kd, g_len = tabs
    out_rows = batch * seq
    rows = RET_CHUNKS_PER_STEP * R_CHUNK
    assert seq % rows == 0
    nc = seq // rows
    nb = R_HEADS // RET_HEADS_PER_STEP
    width = R_HEADS * R_DIM
    blks = [(col0 + t * width) // RET_BLOCK + i for t in range(4) for i in range(nb)]

    def zspec(blk):
        return pl.BlockSpec((rows, RET_BLOCK), lambda b, c: (b * nc + c, blk))

    def tspec():
        return pl.BlockSpec((R_HEADS, R_CHUNK, R_DIM), lambda b, c: (0, 0, 0))

    return pl.pallas_call(
        _ret_prompt_body,
        grid=(batch, nc),
        in_specs=[pl.BlockSpec(memory_space=pltpu.SMEM)]
        + [zspec(blk) for blk in blks]
        + [pl.BlockSpec((rows, R_DIM), lambda b, c: (c, 0)),
           pl.BlockSpec((rows, R_DIM), lambda b, c: (c, 0)),
           tspec(), tspec(), tspec()],
        out_specs=[
            pl.BlockSpec((rows, width), lambda b, c: (b * nc + c, 0)),
            pl.BlockSpec((1, R_HEADS, R_DIM, R_DIM), lambda b, c: (b, 0, 0, 0)),
        ],
        out_shape=[jax.ShapeDtypeStruct((out_rows, width), BF16),
                   jax.ShapeDtypeStruct((batch, R_HEADS, R_DIM, R_DIM), F32)],
        compiler_params=_params(("arbitrary", "arbitrary"), 0),
        name=name,
    )(g_len, *([z] * len(blks)), cos, sin, d, qd, kd)


def _ret_sample_body(gl_ref, q_ref, k_ref, v_ref, g_ref, cos_ref, sin_ref,
                     d_ref, qd_ref, kd_ref, s_ref, o_ref, so_ref, *, dec_seq):
    hh = pl.program_id(1)
    cos = cos_ref[...]
    sin = sin_ref[...]
    nseq = R_CHUNK // dec_seq
    log_l = dec_seq.bit_length() - 1
    row_seq = lax.broadcasted_iota(jnp.int32, (R_CHUNK, R_DIM), 0) >> log_l
    scale = R_DIM ** -0.5
    for hl in range(RET_HEADS_PER_STEP):
        sl = slice(hl * R_DIM, (hl + 1) * R_DIM)
        q = _retrot128(q_ref[:, sl].astype(F32), cos, sin)
        k = _retrot128(k_ref[:, sl].astype(F32), cos, sin) * scale
        v = v_ref[:, sl]
        qb = q.astype(BF16)
        scores = lax.dot_general(qb, k.astype(BF16), NT_DIMS,
                                 preferred_element_type=F32) * d_ref[hl]
        o = jnp.dot(scores.astype(BF16), v, preferred_element_type=F32)
        kd = k * kd_ref[hl]
        gl = gl_ref[hh * RET_HEADS_PER_STEP + hl]
        cross = jnp.zeros((R_CHUNK, R_DIM), F32)
        for s in range(nseq):
            mine = row_seq == s
            state = s_ref[s, hl]
            cs = jnp.dot(qb, state.astype(BF16), preferred_element_type=F32)
            cross = jnp.where(mine, cs, cross)
            ks = jnp.where(mine, kd, 0.0).astype(BF16)
            so_ref[s, hl] = gl * state + lax.dot_general(
                ks, v, TN_DIMS, preferred_element_type=F32)
        o = o + cross * qd_ref[hl]
        o_ref[:, sl] = _group_norm_gate(o, g_ref[:, sl].astype(F32)).astype(o_ref.dtype)


def _ret_sample(z, state, layer, cos, sin, tabs, *, row0, col0, dec_batch, dec_seq, name):
    d, qd, kd, g_len = tabs
    nseq = R_CHUNK // dec_seq
    steps = dec_batch // nseq
    nh = R_HEADS // RET_HEADS_PER_STEP
    width = R_HEADS * R_DIM
    qb, kb, vb, gb = [(col0 + t * width) // RET_BLOCK for t in range(4)]
    r0 = row0 // R_CHUNK
    hp = RET_HEADS_PER_STEP

    def zspec(blk):
        return pl.BlockSpec((R_CHUNK, RET_BLOCK), lambda s, h: (r0 + s, blk + h))

    def tspec():
        return pl.BlockSpec((hp, R_CHUNK, R_DIM), lambda s, h: (h, 0, 0))

    sspec = pl.BlockSpec((None, nseq, hp, R_DIM, R_DIM), lambda s, h: (layer, s, h, 0, 0))
    return pl.pallas_call(
        functools.partial(_ret_sample_body, dec_seq=dec_seq),
        grid=(steps, nh),
        in_specs=[
            pl.BlockSpec(memory_space=pltpu.SMEM),
            zspec(qb), zspec(kb), zspec(vb), zspec(gb),
            pl.BlockSpec((R_CHUNK, R_DIM), lambda s, h: (0, 0)),
            pl.BlockSpec((R_CHUNK, R_DIM), lambda s, h: (0, 0)),
            tspec(), tspec(), tspec(),
            sspec,
        ],
        out_specs=[
            pl.BlockSpec((R_CHUNK, RET_BLOCK), lambda s, h: (s, h)),
            pl.BlockSpec((nseq, hp, R_DIM, R_DIM), lambda s, h: (s, h, 0, 0)),
        ],
        out_shape=[jax.ShapeDtypeStruct((dec_batch * dec_seq, width), BF16),
                   jax.ShapeDtypeStruct(state.shape[1:], F32)],
        compiler_params=_params(("arbitrary", "arbitrary"),
                                4 * nseq * hp * R_DIM * R_DIM * 4 + (16 << 20)),
        name=name,
    )(g_len, z, z, z, z, cos, sin, d, qd, kd, state)


def _gelu_tanh(x):
    c = (2.0 / jnp.pi) ** 0.5
    t = jnp.tanh(x * (c + (c * 0.044715) * (x * x)))
    y = 0.5 * x
    return y + y * t


def _ffn_up_body(h_ref, wu_ref, wg_ref, cw_ref, cb_ref, e0_ref, e1_ref,
                 f_ref, tail_ref, us_ref, wubf_ref, wgbf_ref, carry_ref, ug_ref,
                 *, tm, n_tiles, seq, batch, ms, dec_seq):
    i = pl.program_id(1)
    tn = f_ref.shape[1]
    mp = batch * seq

    @pl.when(i == 0)
    def _():
        wubf_ref[...] = wu_ref[...].astype(BF16)
        wgbf_ref[...] = wg_ref[...].astype(BF16)
        carry_ref[...] = jnp.zeros_like(carry_ref)

    last = n_tiles - 1
    s0 = mp - last * tm
    assert 0 <= s0 and s0 + ms == tm and s0 % FIX_ROWS == 0
    starts = [k * seq for k in range(1, batch)]
    for r in starts:
        assert r % tm != 0 and r % FIX_ROWS == 0, "mid-tile, packed-tile aligned starts only"

    def conv_gelu(u, p1, p2, w, bias):
        conv = bias + w[0:1, :] * p2
        conv = conv + w[1:2, :] * p1
        conv = conv + w[2:3, :] * u
        return _gelu_tanh(conv)

    def conv_gate(u, gate, p1, p2, w, bias):
        return (conv_gelu(u, p1, p2, w, bias) * gate).astype(f_ref.dtype)

    cuts = [0] + [tm * a // FFN_ROW_SPLIT[-1] for a in FFN_ROW_SPLIT]
    assert cuts[-1] == tm and all(r % FIX_ROWS == 0 for r in cuts)
    pieces = [(c, part) for c in range(tn // MXU_COLS) for part in range(len(cuts) - 1)]

    for c, part in pieces:
        cs = slice(c * MXU_COLS, (c + 1) * MXU_COLS)
        rs = slice(cuts[part], cuts[part + 1])
        ug_ref[2 * c, rs, :] = jnp.dot(h_ref[rs, :], wubf_ref[:, cs],
                                       preferred_element_type=F32)

    for c, part in pieces:
        cs = slice(c * MXU_COLS, (c + 1) * MXU_COLS)
        r0 = cuts[part]
        rs = slice(r0, cuts[part + 1])
        w = cw_ref[:, cs]
        bias = cb_ref[:, cs]
        u = ug_ref[2 * c, rs, :]
        if part == 0:
            before = carry_ref[:, cs]
        else:
            before = ug_ref[2 * c, r0 - SUBLANES:r0, :]
        last1 = before[SUBLANES - 1:SUBLANES, :]
        last2 = before[SUBLANES - 2:SUBLANES - 1, :]
        act = conv_gelu(u, pltpu.roll(u, 1, 0), pltpu.roll(u, 2, 0), w, bias)
        uh = u[:FIX_ROWS, :]
        pos = lax.broadcasted_iota(jnp.int32, uh.shape, 0)
        prev1 = jnp.where(pos == 0, last1, pltpu.roll(uh, 1, 0))
        prev2 = jnp.where(pos == 0, last2,
                          jnp.where(pos == 1, last1, pltpu.roll(uh, 2, 0)))
        act_head = conv_gelu(uh, prev1, prev2, w, bias)

        gate = jnp.dot(h_ref[rs, :], wgbf_ref[:, cs], preferred_element_type=F32)
        ug_ref[2 * c + 1, rs, :] = gate
        f_ref[rs, cs] = (act * gate).astype(f_ref.dtype)
        f_ref[r0:r0 + FIX_ROWS, cs] = (act_head * gate[:FIX_ROWS, :]).astype(f_ref.dtype)

    for c in range(tn // MXU_COLS):
        cs = slice(c * MXU_COLS, (c + 1) * MXU_COLS)
        carry_ref[:, cs] = ug_ref[2 * c, tm - SUBLANES:, :]

    def rewrite(rows, p1_fn, p2_fn):
        for c in range(tn // MXU_COLS):
            cs = slice(c * MXU_COLS, (c + 1) * MXU_COLS)
            us = ug_ref[2 * c, rows, :]
            f_ref[rows, cs] = conv_gate(us, ug_ref[2 * c + 1, rows, :], p1_fn(us, cs),
                                        p2_fn(us, cs), cw_ref[:, cs], cb_ref[:, cs])

    for r in starts:
        @pl.when(i == r // tm)
        def _(off=r % tm):
            pos = lax.broadcasted_iota(jnp.int32, (FIX_ROWS, MXU_COLS), 0)
            rewrite(slice(off, off + FIX_ROWS),
                    lambda us, cs: jnp.where(pos == 0, 0.0, pltpu.roll(us, 1, 0)),
                    lambda us, cs: jnp.where(pos <= 1, 0.0, pltpu.roll(us, 2, 0)))

    @pl.when(i == last)
    def _():
        pos = lax.broadcasted_iota(jnp.int32, (ms, MXU_COLS), 0) & (dec_seq - 1)
        rewrite(slice(s0, tm),
                lambda us, cs: jnp.where(pos == 0, e1_ref[:, cs], pltpu.roll(us, 1, 0)),
                lambda us, cs: jnp.where(
                    pos == 0, e0_ref[:, cs],
                    jnp.where(pos == 1, e1_ref[:, cs], pltpu.roll(us, 2, 0))))
        for c in range(tn // MXU_COLS):
            us_ref[:, c * MXU_COLS:(c + 1) * MXU_COLS] = ug_ref[2 * c, s0:, :]

    for b in range(batch):
        r = (b + 1) * seq - SUBLANES

        @pl.when(i == r // tm)
        def _(b=b, off=r % tm):
            for c in range(tn // MXU_COLS):
                tail_ref[b * SUBLANES:(b + 1) * SUBLANES, c * MXU_COLS:(c + 1) * MXU_COLS] = (
                    ug_ref[2 * c, off:off + SUBLANES, :])


def _ffn_up(h, w_up, conv_w, conv_b, layer, e, *, tm, tn, seq, batch, dec_seq, name):
    m, k = h.shape
    d_ff = w_up.shape[2] // 2
    ms = e.shape[2]
    nj = d_ff // tn
    nt = m // tm
    body = functools.partial(_ffn_up_body, tm=tm, n_tiles=nt, seq=seq, batch=batch,
                             ms=ms, dec_seq=dec_seq)
    nbytes = (2 * tm * k * 2 + 4 * k * tn * 4 + 2 * k * tn * 2
              + 16 * tm * tn * 4 + 4 * ms * tn * 4 + (4 << 20))
    return pl.pallas_call(
        body,
        grid=(nj, nt),
        in_specs=[
            pl.BlockSpec((tm, k), lambda j, i: (i, 0)),
            pl.BlockSpec((None, k, tn), lambda j, i: (layer, 0, j)),
            pl.BlockSpec((None, k, tn), lambda j, i: (layer, 0, nj + j)),
            pl.BlockSpec((None, CONV_W, tn), lambda j, i: (layer, 0, j)),
            pl.BlockSpec((None, 1, tn), lambda j, i: (layer, 0, j)),
            pl.BlockSpec((None, None, ms, tn), lambda j, i: (layer, 0, 0, j)),
            pl.BlockSpec((None, None, ms, tn), lambda j, i: (layer, 1, 0, j)),
        ],
        out_specs=[
            pl.BlockSpec((tm, tn), lambda j, i: (i, j)),
            pl.BlockSpec((batch * SUBLANES, tn), lambda j, i: (0, j)),
            pl.BlockSpec((ms, tn), lambda j, i: (0, j)),
        ],
        out_shape=[jax.ShapeDtypeStruct((m, d_ff), BF16),
                   jax.ShapeDtypeStruct((batch * SUBLANES, d_ff), F32),
                   jax.ShapeDtypeStruct((ms, d_ff), F32)],
        scratch_shapes=[pltpu.VMEM((k, tn), BF16), pltpu.VMEM((k, tn), BF16),
                        pltpu.VMEM((SUBLANES, tn), F32),
                        pltpu.VMEM((2 * tn // MXU_COLS, tm, MXU_COLS), F32)],
        compiler_params=_params(("arbitrary", "arbitrary"), nbytes),
        name=name,
    )(h, w_up, w_up, conv_w, conv_b, e, e)


TM = 1408
TM_NORM = 768
TM_DOWN = 704
TM_IN = 2816
TN_IN = 512
TN_UP = 512
GATE_BLOCK = 512
TM_OUT = 352
FFN_ROW_SPLIT = (2, 4)
TN_OUT = 512


def kernel(x_prompt, x_sample, cache_win_k, cache_win_v, state_ret, state_conv,
           g_mix, w_in, sinks, w_proj_a, w_proj_b, w_o, g_ffn, w_up, conv_w,
           conv_b, w_down, g_final):
    batch, seq, d_model = x_prompt.shape
    dec_batch, dec_seq, _ = x_sample.shape
    depth = w_in.shape[0]
    d_ff = w_down.shape[1]
    mp = batch * seq
    ms = dec_batch * dec_seq
    a_width = A_HEADS * A_HEAD_DIM
    kv_width = A_KV_HEADS * A_HEAD_DIM
    r_width = R_HEADS * R_DIM
    ret_col0 = a_width + 2 * kv_width
    ga_col = ret_col0 + 4 * r_width
    gb_col = ga_col + d_model
    assert dec_seq & (dec_seq - 1) == 0 and R_CHUNK % dec_seq == 0

    pos_p = jnp.arange(seq, dtype=F32)
    pos_s = PAST_LEN + jnp.arange(dec_seq, dtype=F32)
    rope_p = _rope_tables(pos_p)
    rope_s = tuple(jnp.tile(t, (SWA_SAMPLE_SEQS, 1)) for t in _rope_tables(pos_s))
    rrot_p = _retrot_tables(pos_p)
    nseq = R_CHUNK // dec_seq
    rrot_s = tuple(jnp.tile(t, (nseq, 1)) for t in _retrot_tables(pos_s))
    tabs_p = _decay_tables(R_CHUNK, 1)
    tabs_s = _decay_tables(dec_seq, nseq)
    g_mix3 = g_mix.reshape(depth, 1, d_model)
    g_ffn3 = g_ffn.reshape(depth, 1, d_model)
    conv_b3 = conv_b.reshape(depth, 1, d_ff)

    e_conv = jnp.repeat(jnp.swapaxes(state_conv, 1, 2), dec_seq, axis=2)
    v_col0 = a_width + kv_width

    kp_l, vp_l, sp_l, cp_l = [], [], [], []
    ks_l, vs_l, ss_l, cs_l = [], [], [], []
    for l in range(depth):
        if l == 0:
            x, h = _join_norm(x_prompt.reshape(mp, d_model), x_sample.reshape(ms, d_model),
                              g_mix3, l, name="join_norm_mix0")
        else:
            h = _rmsnorm(x, g_mix3, l, BF16, tm=TM_NORM, name=f"norm_mix{l}")
        z = _matmul(h, w_in, l, tm=TM_IN, tn=TN_IN, out_dtype=BF16, name=f"proj_in{l}")

        oa_p, kr_p = _swa_prompt(z, sinks, l, *rope_p, batch=batch, seq=seq,
                                 name=f"swa_prompt{l}")
        oa_s, kr_s = _swa_sample(z, cache_win_k, cache_win_v, sinks, l, *rope_s,
                                 row0=mp, dec_batch=dec_batch, dec_seq=dec_seq,
                                 name=f"swa_sample{l}")
        ob_p, s_p = _ret_prompt(z, *rrot_p, tabs_p, col0=ret_col0, batch=batch,
                                seq=seq, name=f"ret_prompt{l}")
        ob_s, s_s = _ret_sample(z, state_ret, l, *rrot_s, tabs_s, row0=mp,
                                col0=ret_col0, dec_batch=dec_batch, dec_seq=dec_seq,
                                name=f"ret_sample{l}")
        merged = _merge(oa_p, oa_s, ob_p, ob_s, z, w_proj_a, w_proj_b, l, ga_col=ga_col,
                        gb_col=gb_col, tg=GATE_BLOCK, name=f"merge{l}")
        x, h2 = _matmul_res_norm(merged, w_o, l, x, g_ffn3, tm=TM_OUT,
                                 name=f"proj_out_norm{l}")
        f, u_tail, u_s = _ffn_up(h2, w_up, conv_w, conv_b3, l, e_conv, tm=TM, tn=TN_UP,
                                 seq=seq, batch=batch, dec_seq=dec_seq, name=f"ffn_up{l}")
        x = _matmul(f, w_down, l, tm=TM_DOWN, tn=TN_OUT, out_dtype=F32, residual=x,
                    name=f"proj_down{l}")

        v_p = jnp.stack([lax.slice(z, ((b + 1) * seq - WINDOW, v_col0), ((b + 1) * seq, ret_col0))
                         for b in range(batch)])
        v_s = lax.slice(z, (mp, v_col0), (mp + ms, ret_col0))
        kp_l.append(kr_p.reshape(batch, WINDOW, A_KV_HEADS, A_HEAD_DIM))
        vp_l.append(v_p.astype(F32).reshape(batch, WINDOW, A_KV_HEADS, A_HEAD_DIM))
        sp_l.append(s_p)
        cp_l.append(u_tail.reshape(batch, SUBLANES, d_ff)[:, -(CONV_W - 1):])
        ks_l.append(kr_s.reshape(dec_batch, dec_seq, A_KV_HEADS, A_HEAD_DIM))
        vs_l.append(v_s.astype(F32).reshape(dec_batch, dec_seq, A_KV_HEADS, A_HEAD_DIM))
        ss_l.append(s_s)
        cs_l.append(u_s.reshape(dec_batch, dec_seq, d_ff)[:, -(CONV_W - 1):])

    y_prompt, y_sample = _split_norm(x, g_final, mp=mp, name="norm_final")
    y_prompt = y_prompt.reshape(batch, seq, d_model)
    y_sample = y_sample.reshape(dec_batch, dec_seq, d_model)
    return (y_prompt, y_sample,
            jnp.stack(kp_l), jnp.stack(vp_l), jnp.stack(sp_l), jnp.stack(cp_l),
            jnp.stack(ks_l), jnp.stack(vs_l), jnp.stack(ss_l), jnp.stack(cs_l))
```
